```python
import jax, jax.numpy as jnp
from jax import lax
import numpy as np

D_MODEL = 1024
BATCH = 32
SEQ = 256
DEPTH = 2
DEC_BATCH = 8
DEC_SEQ = 2048
PAST_LEN = 512

GRID_W = 64
N_MIXERS = 4
W_GRP = D_MODEL // N_MIXERS
D_MIX = N_MIXERS * W_GRP
HEADS = 4
HD = W_GRP // HEADS
N_IN = 9 * W_GRP
CONV_A = 31
CONV_B = 4
CONV_D = 3
CHUNK = 128
LRU_C = 8.0
N_EXPERTS = 32
TOP_K = 4
D_FF = D_MODEL
SWIGLU_LIMIT = 7.0
SWIGLU_ALPHA = 1.702
MOE_BLK = 128
EPS = 1e-6

kernel_name = 'hybrid_diffusion_parallel_groups_step'

F32 = jnp.float32


def rmsnorm(x, g):
    xf = x.astype(F32)
    y = xf * lax.rsqrt(jnp.mean(xf * xf, axis=-1, keepdims=True) + EPS)
    return (y * g.astype(F32)).astype(x.dtype)


def layernorm(x, g, b, groups):
    bz, t, ch = x.shape
    xf = x.astype(F32).reshape(bz, t, groups, ch // groups)
    xc = xf - jnp.mean(xf, axis=-1, keepdims=True)
    y = xc * lax.rsqrt(jnp.mean(xc * xc, axis=-1, keepdims=True) + EPS)
    y = y.reshape(bz, t, ch) * g.astype(F32) + b.astype(F32)
    return y.astype(x.dtype)


def dwconv(x, w, pad_l):
    k = w.shape[0]
    ch = x.shape[-1]
    return lax.conv_general_dilated(
        x, w[:, None, :].astype(x.dtype), window_strides=(1,),
        padding=[(pad_l, k - 1 - pad_l)],
        dimension_numbers=('NWC', 'WIO', 'NWC'), feature_group_count=ch)


def conformer_conv(a_val, a_gate, w_conv, b_conv, g_n, b_n, on_grid):
    z = a_val * jax.nn.sigmoid(a_gate)
    bz, t, ch = z.shape
    if on_grid:
        rows = t // GRID_W
        z = dwconv(z.reshape(bz * rows, GRID_W, ch), w_conv, CONV_A // 2).reshape(bz, t, ch)
    else:
        z = dwconv(z, w_conv, CONV_A // 2)
    z = layernorm(z + b_conv, g_n, b_n, HEADS)
    return jax.nn.silu(z)


def _lin_combine(e1, e2):
    a1, b1 = e1
    a2, b2 = e2
    return a1 * a2, a2 * b1 + b2


def rglru_scan(x, w_r, b_r, w_i, b_i, lam, h0, reverse):
    bz, t, ch = x.shape
    xf = x.astype(F32)
    xh = xf.reshape(bz, t, HEADS, HD)
    r = jax.nn.sigmoid(jnp.einsum('bthd,hde->bthe', xh, w_r.astype(F32)).reshape(bz, t, ch) + b_r.astype(F32))
    i = jax.nn.sigmoid(jnp.einsum('bthd,hde->bthe', xh, w_i.astype(F32)).reshape(bz, t, ch) + b_i.astype(F32))
    log_a = -LRU_C * r * jax.nn.softplus(-lam.astype(F32))
    a = jnp.exp(log_a)
    b = jnp.sqrt(jnp.maximum(-jnp.expm1(2.0 * log_a), 0.0)) * (i * xf)
    first = t - 1 if reverse else 0
    b = b.at[:, first].add(a[:, first] * h0.astype(F32))
    _, h = lax.associative_scan(_lin_combine, (a, b), axis=1, reverse=reverse)
    last = 0 if reverse else t - 1
    return h, h[:, last]


def recurrent_branch(xb, gb, w_conv, b_conv, w_r, b_r, w_i, b_i, lam, h0):
    xc = dwconv(xb, w_conv, 2) + b_conv
    hf, sf = rglru_scan(xc, w_r[0], b_r[0], w_i[0], b_i[0], lam[0], h0[:, 0], False)
    hb, sb = rglru_scan(xc, w_r[1], b_r[1], w_i[1], b_i[1], lam[1], h0[:, 1], True)
    y = jax.nn.gelu(gb) * (hf + hb).astype(xb.dtype)
    return y, jnp.stack([sf, sb], axis=1).astype(xb.dtype)


def chunk_sgu(u, v, g_n, b_n, w_s, b_s):
    bz, t, ch = v.shape
    v = layernorm(v, g_n, b_n, 1)
    vh = v.reshape(bz, t // CHUNK, CHUNK, HEADS, HD)
    s = jnp.einsum('hpq,bnqhd->bnphd', w_s.astype(v.dtype), vh) + b_s.T[None, None, :, :, None]
    return u * s.reshape(bz, t, ch)


def short_gated_conv(bg, cg, xd, w_conv):
    return bg * dwconv(cg * xd, w_conv, CONV_D // 2)


def moe(h, w_router, b_router, w_gu, b_gu, w_dn, b_dn):
    bz, t, d = h.shape
    n_tok = bz * t
    x2 = h.reshape(n_tok, d)
    logits = (x2 @ w_router).astype(F32) + b_router.astype(F32)
    top_v, top_i = lax.top_k(logits, TOP_K)
    probs = jax.nn.softmax(top_v, axis=-1)
    n_asg = n_tok * TOP_K
    flat_e = top_i.reshape(-1)
    flat_tok = jnp.repeat(jnp.arange(n_tok, dtype=jnp.int32), TOP_K)
    flat_p = probs.reshape(-1)
    order = jnp.argsort(flat_e)
    se, stok, sp = flat_e[order], flat_tok[order], flat_p[order]
    counts = jnp.bincount(flat_e, length=N_EXPERTS)
    starts = jnp.cumsum(counts) - counts
    padded = (counts + MOE_BLK - 1) // MOE_BLK * MOE_BLK
    pad_end = jnp.cumsum(padded)
    pad_start = pad_end - padded
    dest = pad_start[se] + jnp.arange(n_asg, dtype=jnp.int32) - starts[se]
    n_pad = (n_asg + N_EXPERTS * (MOE_BLK - 1) + MOE_BLK - 1) // MOE_BLK * MOE_BLK
    n_blk = n_pad // MOE_BLK
    tok_buf = jnp.zeros((n_pad,), jnp.int32).at[dest].set(stok)
    p_buf = jnp.zeros((n_pad,), F32).at[dest].set(sp)
    blk_e = jnp.minimum(jnp.searchsorted(pad_end, jnp.arange(n_blk, dtype=jnp.int32) * MOE_BLK, side='right'),
                        N_EXPERTS - 1)
    x_buf = x2[tok_buf].reshape(n_blk, MOE_BLK, d)

    def expert_block(args):
        xb, e = args
        gu = xb @ w_gu[e] + b_gu[e]
        g = jnp.minimum(gu[:, :D_FF], SWIGLU_LIMIT)
        u = jnp.clip(gu[:, D_FF:], -SWIGLU_LIMIT, SWIGLU_LIMIT)
        y = (u + 1.0) * (g * jax.nn.sigmoid(SWIGLU_ALPHA * g))
        return y @ w_dn[e] + b_dn[e]

    y_buf = lax.map(expert_block, (x_buf, blk_e)).reshape(n_pad, d)
    out = jnp.zeros_like(x2).at[tok_buf].add(y_buf * p_buf[:, None].astype(y_buf.dtype))
    return out.reshape(bz, t, d)


def mix_layer(hn, on_grid, h0, lp):
    proj = hn @ lp['w_in']

    def sl(k):
        return proj[..., k * W_GRP:(k + 1) * W_GRP]

    ya = conformer_conv(sl(0), sl(1), lp['conv_a_w'], lp['conv_a_b'], lp['norm_a_g'], lp['norm_a_b'], on_grid)
    yb, st = recurrent_branch(sl(2), sl(3), lp['conv_b_w'], lp['conv_b_b'], lp['lru_wr'], lp['lru_br'],
                              lp['lru_wi'], lp['lru_bi'], lp['lru_lam'], h0)
    yc = chunk_sgu(sl(4), sl(5), lp['sgu_norm_g'], lp['sgu_norm_b'], lp['sgu_w'], lp['sgu_b'])
    yd = short_gated_conv(sl(6), sl(7), sl(8), lp['conv_d_w'])
    y = jnp.concatenate([ya, yb, yc, yd], axis=-1) @ lp['w_out']
    return y, st


def trunk_layer(x, mod, on_grid, h0, lp):
    sh1, sc1, g1, sh2, sc2, g2 = jnp.split(mod[:, None, :], 6, axis=-1)
    hn = rmsnorm(x, lp['norm1_g']) * (1.0 + sc1) + sh1
    y, st = mix_layer(hn, on_grid, h0, lp)
    x = x + g1 * y
    hn2 = rmsnorm(x, lp['norm2_g']) * (1.0 + sc2) + sh2
    x = x + g2 * moe(hn2, lp['w_router'], lp['b_router'], lp['w_gu'], lp['b_gu'], lp['w_dn'], lp['b_dn'])
    return x, st


def setup_inputs(seed: int = 0) -> dict:
    key = jax.random.key(seed)
    ks = jax.random.split(key, 40)

    def nrm(k, shape, scale):
        return jax.random.normal(k, shape, F32) * scale

    u = jax.random.uniform(ks[20], (DEPTH, 2, W_GRP), F32, minval=0.9, maxval=0.999)
    a = u ** (1.0 / LRU_C)
    lru_lam = jnp.log(a) - jnp.log1p(-a)
    return {
        'x_prompt': nrm(ks[0], (BATCH, SEQ, D_MODEL), 1.0),
        'x_sample': nrm(ks[1], (DEC_BATCH, DEC_SEQ, D_MODEL), 1.0),
        'state_rglru': nrm(ks[2], (DEC_BATCH, DEPTH, 2, W_GRP), 0.5),
        'c': nrm(ks[3], (DEC_BATCH, D_MODEL), 1.0),
        'c_ctx': nrm(ks[4], (D_MODEL,), 1.0),
        'w_ada': nrm(ks[5], (DEPTH, D_MODEL, 6 * D_MODEL), 0.3 * D_MODEL ** -0.5),
        'b_ada': nrm(ks[6], (DEPTH, 6 * D_MODEL), 0.02),
        'norm1_g': 1.0 + nrm(ks[7], (DEPTH, D_MODEL), 0.02),
        'norm2_g': 1.0 + nrm(ks[8], (DEPTH, D_MODEL), 0.02),
        'w_in': nrm(ks[9], (DEPTH, D_MODEL, N_IN), D_MODEL ** -0.5),
        'conv_a_w': nrm(ks[10], (DEPTH, CONV_A, W_GRP), CONV_A ** -0.5),
        'conv_a_b': nrm(ks[11], (DEPTH, W_GRP), 0.02),
        'norm_a_g': 1.0 + nrm(ks[12], (DEPTH, W_GRP), 0.02),
        'norm_a_b': nrm(ks[13], (DEPTH, W_GRP), 0.02),
        'conv_b_w': nrm(ks[14], (DEPTH, CONV_B, W_GRP), CONV_B ** -0.5),
        'conv_b_b': nrm(ks[15], (DEPTH, W_GRP), 0.02),
        'lru_wr': nrm(ks[16], (DEPTH, 2, HEADS, HD, HD), HD ** -0.5),
        'lru_br': nrm(ks[17], (DEPTH, 2, W_GRP), 0.02),
        'lru_wi': nrm(ks[18], (DEPTH, 2, HEADS, HD, HD), HD ** -0.5),
        'lru_bi': nrm(ks[19], (DEPTH, 2, W_GRP), 0.02),
        'lru_lam': lru_lam,
        'sgu_norm_g': 1.0 + nrm(ks[21], (DEPTH, W_GRP), 0.02),
        'sgu_norm_b': nrm(ks[22], (DEPTH, W_GRP), 0.02),
        'sgu_w': nrm(ks[23], (DEPTH, HEADS, CHUNK, CHUNK), 0.5 * CHUNK ** -0.5),
        'sgu_b': 1.0 + nrm(ks[24], (DEPTH, HEADS, CHUNK), 0.02),
        'conv_d_w': nrm(ks[25], (DEPTH, CONV_D, W_GRP), CONV_D ** -0.5),
        'w_out': nrm(ks[26], (DEPTH, D_MIX, D_MODEL), D_MIX ** -0.5),
        'w_router': nrm(ks[27], (DEPTH, D_MODEL, N_EXPERTS), D_MODEL ** -0.5),
        'b_router': nrm(ks[28], (DEPTH, N_EXPERTS), 0.01),
        'w_gu': nrm(ks[29], (DEPTH, N_EXPERTS, D_MODEL, 2 * D_FF), D_MODEL ** -0.5),
        'b_gu': nrm(ks[30], (DEPTH, N_EXPERTS, 2 * D_FF), 0.01),
        'w_dn': nrm(ks[31], (DEPTH, N_EXPERTS, D_FF, D_MODEL), D_FF ** -0.5),
        'b_dn': nrm(ks[32], (DEPTH, N_EXPERTS, D_MODEL), 0.01),
        'final_g': 1.0 + nrm(ks[33], (D_MODEL,), 0.02),
    }


def reference(x_prompt, x_sample, state_rglru, c, c_ctx, w_ada, b_ada, norm1_g, norm2_g, w_in,
              conv_a_w, conv_a_b, norm_a_g, norm_a_b, conv_b_w, conv_b_b, lru_wr, lru_br, lru_wi,
              lru_bi, lru_lam, sgu_norm_g, sgu_norm_b, sgu_w, sgu_b, conv_d_w, w_out, w_router,
              b_router, w_gu, b_gu, w_dn, b_dn, final_g):
    xp = x_prompt
    xs = x_sample
    cond_ctx = jax.nn.silu(c_ctx)[None, :]
    cond_lat = jax.nn.silu(c)
    h0_ctx = jnp.zeros((xp.shape[0], 2, W_GRP), xp.dtype)
    ctx_states = []
    for l in range(DEPTH):
        lp = dict(norm1_g=norm1_g[l], norm2_g=norm2_g[l], w_in=w_in[l],
                  conv_a_w=conv_a_w[l], conv_a_b=conv_a_b[l], norm_a_g=norm_a_g[l], norm_a_b=norm_a_b[l],
                  conv_b_w=conv_b_w[l], conv_b_b=conv_b_b[l], lru_wr=lru_wr[l], lru_br=lru_br[l],
                  lru_wi=lru_wi[l], lru_bi=lru_bi[l], lru_lam=lru_lam[l],
                  sgu_norm_g=sgu_norm_g[l], sgu_norm_b=sgu_norm_b[l], sgu_w=sgu_w[l], sgu_b=sgu_b[l],
                  conv_d_w=conv_d_w[l], w_out=w_out[l], w_router=w_router[l], b_router=b_router[l],
                  w_gu=w_gu[l], b_gu=b_gu[l], w_dn=w_dn[l], b_dn=b_dn[l])
        mod_ctx = cond_ctx @ w_ada[l] + b_ada[l]
        mod_lat = cond_lat @ w_ada[l] + b_ada[l]
        xp, st = trunk_layer(xp, mod_ctx, False, h0_ctx, lp)
        ctx_states.append(st)
        xs, _ = trunk_layer(xs, mod_lat, True, state_rglru[:, l], lp)
    new_state_rglru = jnp.stack(ctx_states, axis=1)
    y_prompt = rmsnorm(xp, final_g)
    y_sample = rmsnorm(xs, final_g)
    return (y_prompt, y_sample, new_state_rglru)
```

```python
import functools

import jax
import jax.numpy as jnp
from jax import lax
from jax.experimental import pallas as pl
from jax.experimental.pallas import tpu as pltpu

F32 = jnp.float32
BF16 = jnp.bfloat16

D_MODEL = 1024
DEPTH = 2
GRID_W = 64
N_MIXERS = 4
W_GRP = D_MODEL // N_MIXERS
HEADS = 4
HD = W_GRP // HEADS
N_IN = 9 * W_GRP
CONV_A = 31
CONV_B = 4
CONV_D = 3
CHUNK = 128
LRU_C = 8.0
N_EXPERTS = 32
TOP_K = 4
D_FF = D_MODEL
SWIGLU_LIMIT = 7.0
SWIGLU_ALPHA = 1.702
EPS = 1e-6

V7X_SUBLANES = 8
V7X_VMEM_BYTES = 64 * 1024 * 1024
VMEM_LIMIT = V7X_VMEM_BYTES * 7 // 8

COND_ROWS = 16
ADA_TN = 1536
ROW_TILE = 512
CONV_ROWS = 64
CONV_PAD = 16
MOE_TM = 512


def _rmsnorm(x, g):
    return x * lax.rsqrt(jnp.mean(x * x, axis=-1, keepdims=True) + EPS) * g


def _group_mean(x, m_ref):
    hi = x.astype(BF16)
    lo = (x - hi.astype(F32)).astype(BF16)
    m = m_ref[...]
    return (jnp.dot(hi, m, preferred_element_type=F32) + jnp.dot(lo, m, preferred_element_type=F32))


def _layernorm(x, g, b, m_ref):
    xc = x - _group_mean(x, m_ref)
    var = _group_mean(xc * xc, m_ref)
    return xc * lax.rsqrt(var + EPS) * g + b


def _ada_kernel(c_ref, w_ref, b_ref, o_ref):
    c = c_ref[...]
    cond = (c * jax.nn.sigmoid(c)).astype(BF16)
    o_ref[...] = jnp.dot(cond, w_ref[...].astype(BF16), preferred_element_type=F32) + b_ref[...]


def _ada_mod(cond_rows, w_ada, b_ada):
    n_col = w_ada.shape[-1]
    return pl.pallas_call(
        _ada_kernel,
        grid=(DEPTH, n_col // ADA_TN),
        in_specs=[
            pl.BlockSpec((COND_ROWS, D_MODEL), lambda l, j: (0, 0)),
            pl.BlockSpec((None, D_MODEL, ADA_TN), lambda l, j: (l, 0, j)),
            pl.BlockSpec((None, 1, ADA_TN), lambda l, j: (l, 0, j)),
        ],
        out_specs=pl.BlockSpec((None, COND_ROWS, ADA_TN), lambda l, j: (l, 0, j)),
        out_shape=jax.ShapeDtypeStruct((DEPTH, COND_ROWS, n_col), F32),
        name="ada_mod",
    )(cond_rows, w_ada, b_ada.reshape(DEPTH, 1, n_col))


def _inproj_kernel(*refs, has_res):
    if has_res:
        x_ref, moe_ref, modp_ref, mod_ref, g_ref, w_ref, xo_ref, p_ref = refs
        x = x_ref[...] + modp_ref[5:6, :] * moe_ref[...]
        xo_ref[...] = x
    else:
        x_ref, mod_ref, g_ref, w_ref, p_ref = refs
        x = x_ref[...]
    hn = _rmsnorm(x, g_ref[...]) * (1.0 + mod_ref[1:2, :]) + mod_ref[0:1, :]
    p_ref[...] = jnp.dot(hn.astype(BF16), w_ref[...], preferred_element_type=F32)


def _mod_spec(mod, seq, tm):
    if mod.shape[0] == 1:
        return pl.BlockSpec((None, 6, D_MODEL), lambda i: (0, 0, 0))
    return pl.BlockSpec((None, 6, D_MODEL), lambda i: ((i * tm) // seq, 0, 0))


def _in_proj(x, moe, mod_prev, mod, g, w_bf16, seq):
    n = x.shape[0]
    tm = min(ROW_TILE, seq)
    row = pl.BlockSpec((tm, D_MODEL), lambda i: (i, 0))
    has_res = moe is not None
    args, specs = [x], [row]
    if has_res:
        args += [moe, mod_prev]
        specs += [row, _mod_spec(mod_prev, seq, tm)]
    args += [mod, g.reshape(1, D_MODEL), w_bf16]
    specs += [_mod_spec(mod, seq, tm),
              pl.BlockSpec((1, D_MODEL), lambda i: (0, 0)),
              pl.BlockSpec((D_MODEL, N_IN), lambda i: (0, 0))]
    proj_shape = jax.ShapeDtypeStruct((n, N_IN), F32)
    proj_spec = pl.BlockSpec((tm, N_IN), lambda i: (i, 0))
    if has_res:
        out_shape = (jax.ShapeDtypeStruct((n, D_MODEL), F32), proj_shape)
        out_specs = (row, proj_spec)
    else:
        out_shape, out_specs = proj_shape, proj_spec
    out = pl.pallas_call(
        functools.partial(_inproj_kernel, has_res=has_res),
        grid=(n // tm,),
        in_specs=specs,
        out_specs=out_specs,
        out_shape=out_shape,
        compiler_params=pltpu.CompilerParams(vmem_limit_bytes=VMEM_LIMIT),
        name="in_proj",
    )(*args)
    return out if has_res else (x, out)


def _group_scan(a, b, reverse):
    rows = a.shape[0]
    ri = lax.broadcasted_iota(jnp.int32, a.shape, 0) & (V7X_SUBLANES - 1)
    for d in (1, 2, 4):
        if reverse:
            keep = ri < V7X_SUBLANES - d
            ra, rb = pltpu.roll(a, rows - d, 0), pltpu.roll(b, rows - d, 0)
        else:
            keep = ri >= d
            ra, rb = pltpu.roll(a, d, 0), pltpu.roll(b, d, 0)
        b = a * jnp.where(keep, rb, 0.0) + b
        a = a * jnp.where(keep, ra, 1.0)
    return a, b


def _mixer_kernel(s0, s1, s2, caw, cab, nag, nab, cbw, cbb, wg, bg, lam, h0, sng, snb, sw, sbias, cdw,
                  m_head, m_all, y_ref, st_ref, pad_s, af_s, bf_s, ab_s, bb_s, *, seq, on_grid):
    mixer = pl.program_id(1)
    n_conv = seq // CONV_ROWS
    win_rows = CONV_ROWS + 2 * CONV_PAD

    def rows_at(c, size):
        return pl.ds(pl.multiple_of(c * size, size), size)

    def fill_padded(fn):
        zeros = jnp.zeros((CONV_PAD, W_GRP), F32)
        pad_s[0:CONV_PAD, :] = zeros
        pad_s[CONV_PAD + seq:2 * CONV_PAD + seq, :] = zeros

        def body(c, carry):
            dst = pl.ds(pl.multiple_of(c * CONV_ROWS + CONV_PAD, V7X_SUBLANES), CONV_ROWS)
            pad_s[dst, :] = fn(rows_at(c, CONV_ROWS))
            return carry

        lax.fori_loop(0, n_conv, body, 0)

    def conv_rows(c, w_ref, taps, pad_l, row_mask):
        win = pad_s[pl.ds(pl.multiple_of(c * CONV_ROWS, CONV_ROWS), win_rows), :]
        acc = jnp.zeros((CONV_ROWS, W_GRP), F32)
        ri = lax.broadcasted_iota(jnp.int32, (CONV_ROWS, W_GRP), 0)
        for k in range(taps):
            off = k - pad_l
            sh = win[CONV_PAD + off:CONV_PAD + off + CONV_ROWS, :]
            if row_mask:
                sh = jnp.where((ri >= -off) & (ri < CONV_ROWS - off), sh, 0.0)
            acc = acc + w_ref[k:k + 1, :] * sh
        return acc

    @pl.when(mixer == 0)
    def _conformer():
        def glu(rows):
            return s0[rows, :] * jax.nn.sigmoid(s1[rows, :])

        fill_padded(glu)

        def body(c, carry):
            z = conv_rows(c, caw, CONV_A, CONV_A // 2, on_grid) + cab[...]
            z = _layernorm(z, nag[...], nab[...], m_head)
            y_ref[rows_at(c, CONV_ROWS), :] = (z * jax.nn.sigmoid(z)).astype(BF16)
            return carry

        lax.fori_loop(0, n_conv, body, 0)

    @pl.when(mixer == 1)
    def _rglru():
        fill_padded(lambda rows: s0[rows, :])
        lam_v = lam[...]
        softplus_neg = jnp.maximum(-lam_v, 0.0) + jnp.log1p(jnp.exp(-jnp.abs(lam_v)))

        def gates(c, carry):
            rows = rows_at(c, CONV_ROWS)
            xc = conv_rows(c, cbw, CONV_B, 2, False) + cbb[...]
            gt = jax.nn.sigmoid(jnp.dot(xc.astype(BF16), wg[...], preferred_element_type=F32) + bg[...])
            for d, (a_s, b_s) in enumerate(((af_s, bf_s), (ab_s, bb_s))):
                r_gate = gt[:, (2 * d) * W_GRP:(2 * d + 1) * W_GRP]
                i_gate = gt[:, (2 * d + 1) * W_GRP:(2 * d + 2) * W_GRP]
                log_a = -LRU_C * r_gate * softplus_neg[d:d + 1, :]
                a = jnp.exp(log_a)
                b = jnp.sqrt(jnp.maximum(-jnp.tanh(log_a) * (a * a + 1.0), 0.0)) * (i_gate * xc)
                a, b = _group_scan(a, b, reverse=(d == 1))
                a_s[rows, :] = a
                b_s[rows, :] = b
            return carry

        lax.fori_loop(0, n_conv, gates, 0)

        n_grp = seq // V7X_SUBLANES

        def chain(g, carry):
            cf, cb = carry
            rf = rows_at(g, V7X_SUBLANES)
            rb = rows_at(n_grp - 1 - g, V7X_SUBLANES)
            hf = af_s[rf, :] * cf + bf_s[rf, :]
            bf_s[rf, :] = hf
            hb = ab_s[rb, :] * cb + bb_s[rb, :]
            bb_s[rb, :] = hb
            cf = jnp.broadcast_to(hf[V7X_SUBLANES - 1:V7X_SUBLANES, :], (V7X_SUBLANES, W_GRP))
            cb = jnp.broadcast_to(hb[0:1, :], (V7X_SUBLANES, W_GRP))
            return cf, cb

        init = (jnp.broadcast_to(h0[0:1, :], (V7X_SUBLANES, W_GRP)),
                jnp.broadcast_to(h0[1:2, :], (V7X_SUBLANES, W_GRP)))
        cf, cb = lax.fori_loop(0, n_grp, chain, init, unroll=4)
        st_ref[0:1, :] = cf[0:1, :]
        st_ref[1:2, :] = cb[0:1, :]

        def out(c, carry):
            rows = rows_at(c, CONV_ROWS)
            y_ref[rows, :] = (jax.nn.gelu(s1[rows, :]) * (bf_s[rows, :] + bb_s[rows, :])).astype(BF16)
            return carry

        lax.fori_loop(0, n_conv, out, 0)

    @pl.when(mixer == 2)
    def _sgu():
        lane = lax.broadcasted_iota(jnp.int32, (CHUNK, W_GRP), 1)

        def body(n, carry):
            rows = rows_at(n, CHUNK)
            v = _layernorm(s1[rows, :], sng[...], snb[...], m_all).astype(BF16)
            s = sbias[...]
            for h in range(HEADS):
                sh = jnp.dot(sw[h], v, preferred_element_type=F32)
                s = s + jnp.where((lane >= h * HD) & (lane < (h + 1) * HD), sh, 0.0)
            y_ref[rows, :] = (s0[rows, :] * s).astype(BF16)
            return carry

        lax.fori_loop(0, seq // CHUNK, body, 0)

    @pl.when(mixer == 3)
    def _gated_conv():
        fill_padded(lambda rows: s1[rows, :] * s2[rows, :])

        def body(c, carry):
            rows = rows_at(c, CONV_ROWS)
            y_ref[rows, :] = (s0[rows, :] * conv_rows(c, cdw, CONV_D, CONV_D // 2, False)).astype(BF16)
            return carry

        lax.fori_loop(0, n_conv, body, 0)


def _mixers(proj, h0, lw, batch, seq, on_grid):
    proj3 = proj.reshape(batch, seq, N_IN)
    slab = (None, seq, W_GRP)

    def const(shape):
        return pl.BlockSpec(shape, lambda b, m: (0,) * len(shape))

    in_specs = [
        pl.BlockSpec(slab, lambda b, m: (b, 0, 2 * m)),
        pl.BlockSpec(slab, lambda b, m: (b, 0, 2 * m + 1)),
        pl.BlockSpec(slab, lambda b, m: (b, 0, 8)),
        const((CONV_A, W_GRP)), const((1, W_GRP)), const((1, W_GRP)), const((1, W_GRP)),
        const((CONV_B, W_GRP)), const((1, W_GRP)),
        const((W_GRP, 4 * W_GRP)), const((1, 4 * W_GRP)), const((2, W_GRP)),
        pl.BlockSpec((None, 2, W_GRP), lambda b, m: (b, 0, 0)),
        const((1, W_GRP)), const((1, W_GRP)), const((HEADS, CHUNK, CHUNK)), const((CHUNK, W_GRP)),
        const((CONV_D, W_GRP)), const((W_GRP, W_GRP)), const((W_GRP, W_GRP)),
    ]
    y, st = pl.pallas_call(
        functools.partial(_mixer_kernel, seq=seq, on_grid=on_grid),
        grid=(batch, N_MIXERS),
        in_specs=in_specs,
        out_specs=(pl.BlockSpec(slab, lambda b, m: (b, 0, m)),
                   pl.BlockSpec((None, 2, W_GRP), lambda b, m: (b, 0, 0))),
        out_shape=(jax.ShapeDtypeStruct((batch, seq, D_MODEL), BF16),
                   jax.ShapeDtypeStruct((batch, 2, W_GRP), F32)),
        scratch_shapes=[pltpu.VMEM((seq + 2 * CONV_PAD, W_GRP), F32)] + [pltpu.VMEM((seq, W_GRP), F32)] * 4,
        compiler_params=pltpu.CompilerParams(dimension_semantics=("arbitrary", "arbitrary"),
                                             vmem_limit_bytes=VMEM_LIMIT),
        name="mixers",
    )(proj3, proj3, proj3, lw["conv_a_w"], lw["conv_a_b"], lw["norm_a_g"], lw["norm_a_b"],
      lw["conv_b_w"], lw["conv_b_b"], lw["gate_w"], lw["gate_b"], lw["lru_lam"], h0,
      lw["sgu_norm_g"], lw["sgu_norm_b"], lw["sgu_w"], lw["sgu_bias"], lw["conv_d_w"],
      lw["m_head"], lw["m_all"])
    return y.reshape(batch * seq, D_MODEL), st


def _outproj_kernel(y_ref, x_ref, mod_ref, g_ref, wo_ref, wr_ref, br_ref, xm_ref, hn_ref, lg_ref):
    y = jnp.dot(y_ref[...], wo_ref[...], preferred_element_type=F32)
    x = x_ref[...] + mod_ref[2:3, :] * y
    xm_ref[...] = x
    hn = _rmsnorm(x, g_ref[...]) * (1.0 + mod_ref[4:5, :]) + mod_ref[3:4, :]
    hn_ref[...] = hn.astype(BF16)
    lg_ref[...] = jnp.dot(hn, wr_ref[...], preferred_element_type=F32,
                          precision=lax.Precision.HIGHEST) + br_ref[...]


def _out_proj(y, x, mod, g, wo_bf16, w_router, b_router, seq):
    n = x.shape[0]
    tm = min(ROW_TILE, seq)
    row = pl.BlockSpec((tm, D_MODEL), lambda i: (i, 0))
    return pl.pallas_call(
        _outproj_kernel,
        grid=(n // tm,),
        in_specs=[row, row, _mod_spec(mod, seq, tm),
                  pl.BlockSpec((1, D_MODEL), lambda i: (0, 0)),
                  pl.BlockSpec((D_MODEL, D_MODEL), lambda i: (0, 0)),
                  pl.BlockSpec((D_MODEL, N_EXPERTS), lambda i: (0, 0)),
                  pl.BlockSpec((1, N_EXPERTS), lambda i: (0, 0))],
        out_specs=(row, row, pl.BlockSpec((tm, N_EXPERTS), lambda i: (i, 0))),
        out_shape=(jax.ShapeDtypeStruct((n, D_MODEL), F32),
                   jax.ShapeDtypeStruct((n, D_MODEL), BF16),
                   jax.ShapeDtypeStruct((n, N_EXPERTS), F32)),
        name="out_proj",
    )(y, x, mod, g.reshape(1, D_MODEL), wo_bf16, w_router, b_router.reshape(1, N_EXPERTS))


def _moe_kernel(blk_e, blk_new, blk_on, x_ref, p_ref, wgu_ref, bgu_ref, wdn_ref, bdn_ref, o_ref, wgu_s, wdn_s):
    i = pl.program_id(0)

    @pl.when(blk_on[i] == 0)
    def _():
        o_ref[...] = jnp.zeros_like(o_ref)

    @pl.when(blk_on[i] == 1)
    def _():
        @pl.when(blk_new[i] == 1)
        def _():
            wgu_s[...] = wgu_ref[...].astype(BF16)
            wdn_s[...] = wdn_ref[...].astype(BF16)

        gu = jnp.dot(x_ref[...], wgu_s[...], preferred_element_type=F32) + bgu_ref[...]
        g = jnp.minimum(gu[:, :D_FF], SWIGLU_LIMIT)
        u = jnp.clip(gu[:, D_FF:], -SWIGLU_LIMIT, SWIGLU_LIMIT)
        act = (u + 1.0) * (g * jax.nn.sigmoid(SWIGLU_ALPHA * g))
        y = jnp.dot(act.astype(BF16), wdn_s[...], preferred_element_type=F32) + bdn_ref[...]
        o_ref[...] = y * p_ref[...]


def _moe_blocks(x_buf, p_buf, blk_e, blk_new, blk_on, w_gu, b_gu, w_dn, b_dn):
    n_pad = x_buf.shape[0]
    n_blk = n_pad // MOE_TM
    grid_spec = pltpu.PrefetchScalarGridSpec(
        num_scalar_prefetch=3,
        grid=(n_blk,),
        in_specs=[
            pl.BlockSpec((MOE_TM, D_MODEL), lambda i, e, nw, on: (i, 0)),
            pl.BlockSpec((MOE_TM, 1), lambda i, e, nw, on: (i, 0)),
            pl.BlockSpec((None, D_MODEL, 2 * D_FF), lambda i, e, nw, on: (e[i], 0, 0)),
            pl.BlockSpec((None, 1, 2 * D_FF), lambda i, e, nw, on: (e[i], 0, 0)),
            pl.BlockSpec((None, D_FF, D_MODEL), lambda i, e, nw, on: (e[i], 0, 0)),
            pl.BlockSpec((None, 1, D_MODEL), lambda i, e, nw, on: (e[i], 0, 0)),
        ],
        out_specs=pl.BlockSpec((MOE_TM, D_MODEL), lambda i, e, nw, on: (i, 0)),
        scratch_shapes=[pltpu.VMEM((D_MODEL, 2 * D_FF), BF16), pltpu.VMEM((D_FF, D_MODEL), BF16)],
    )
    return pl.pallas_call(
        _moe_kernel,
        grid_spec=grid_spec,
        out_shape=jax.ShapeDtypeStruct((n_pad, D_MODEL), F32),
        compiler_params=pltpu.CompilerParams(dimension_semantics=("arbitrary",),
                                             vmem_limit_bytes=VMEM_LIMIT),
        name="moe_experts",
    )(blk_e, blk_new, blk_on, x_buf, p_buf, w_gu, b_gu.reshape(N_EXPERTS, 1, 2 * D_FF),
      w_dn, b_dn.reshape(N_EXPERTS, 1, D_MODEL))


def _moe(hn_bf16, logits, w_gu, b_gu, w_dn, b_dn):
    n_tok = hn_bf16.shape[0]
    n_asg = n_tok * TOP_K
    n_pad = (n_asg + N_EXPERTS * (MOE_TM - 1) + MOE_TM - 1) // MOE_TM * MOE_TM
    n_blk = n_pad // MOE_TM
    top_v, top_i = lax.top_k(logits, TOP_K)
    probs = jax.nn.softmax(top_v, axis=-1)
    flat_e = top_i.reshape(-1)
    onehot = (flat_e[:, None] == jnp.arange(N_EXPERTS, dtype=jnp.int32)[None, :]).astype(jnp.int32)
    running = jnp.cumsum(onehot, axis=0)
    counts = running[-1]
    rank = jnp.take_along_axis(running, flat_e[:, None], axis=1)[:, 0] - 1
    padded = (counts + MOE_TM - 1) // MOE_TM * MOE_TM
    pad_end = jnp.cumsum(padded)
    pad_start = pad_end - padded
    dest = pad_start[flat_e] + rank
    tok = jnp.arange(n_asg, dtype=jnp.int32) // TOP_K
    tok_buf = jnp.zeros((n_pad,), jnp.int32).at[dest].set(tok)
    p_buf = jnp.zeros((n_pad,), F32).at[dest].set(probs.reshape(-1))
    blk_row = jnp.arange(n_blk, dtype=jnp.int32) * MOE_TM
    blk_e = jnp.minimum(jnp.searchsorted(pad_end, blk_row, side="right"), N_EXPERTS - 1).astype(jnp.int32)
    blk_on = (blk_row < pad_end[-1]).astype(jnp.int32)
    blk_new = jnp.concatenate([jnp.ones((1,), jnp.int32), (blk_e[1:] != blk_e[:-1]).astype(jnp.int32)])
    x_buf = hn_bf16[tok_buf]
    y_buf = _moe_blocks(x_buf, p_buf[:, None], blk_e, blk_new, blk_on, w_gu, b_gu, w_dn, b_dn)
    return jnp.sum(y_buf[dest.reshape(n_tok, TOP_K)], axis=1)


def _final_kernel(x_ref, moe_ref, mod_ref, g_ref, o_ref):
    x = x_ref[...] + mod_ref[5:6, :] * moe_ref[...]
    o_ref[...] = _rmsnorm(x, g_ref[...])


def _final_norm(x, moe, mod, g, seq):
    n = x.shape[0]
    tm = min(ROW_TILE, seq)
    row = pl.BlockSpec((tm, D_MODEL), lambda i: (i, 0))
    return pl.pallas_call(
        _final_kernel,
        grid=(n // tm,),
        in_specs=[row, row, _mod_spec(mod, seq, tm), pl.BlockSpec((1, D_MODEL), lambda i: (0, 0))],
        out_specs=row,
        out_shape=jax.ShapeDtypeStruct((n, D_MODEL), F32),
        name="final_norm",
    )(x, moe, mod, g.reshape(1, D_MODEL))


def _block_diag(w):
    eye = jnp.eye(HEADS, dtype=w.dtype)
    return (eye[:, None, :, None] * w[:, :, None, :]).reshape(W_GRP, W_GRP)


def _layer_weights(l, p):
    gate_w = jnp.concatenate([_block_diag(p["lru_wr"][l, 0]), _block_diag(p["lru_wi"][l, 0]),
                              _block_diag(p["lru_wr"][l, 1]), _block_diag(p["lru_wi"][l, 1])], axis=1)
    gate_b = jnp.concatenate([p["lru_br"][l, 0], p["lru_bi"][l, 0], p["lru_br"][l, 1], p["lru_bi"][l, 1]])
    head_of = jnp.arange(W_GRP) // HD
    row = lambda v: v.reshape(1, W_GRP)
    return dict(
        conv_a_w=p["conv_a_w"][l], conv_a_b=row(p["conv_a_b"][l]),
        norm_a_g=row(p["norm_a_g"][l]), norm_a_b=row(p["norm_a_b"][l]),
        conv_b_w=p["conv_b_w"][l], conv_b_b=row(p["conv_b_b"][l]),
        gate_w=gate_w.astype(BF16), gate_b=gate_b.reshape(1, 4 * W_GRP), lru_lam=p["lru_lam"][l],
        sgu_norm_g=row(p["sgu_norm_g"][l]), sgu_norm_b=row(p["sgu_norm_b"][l]),
        sgu_w=p["sgu_w"][l].astype(BF16), sgu_bias=jnp.repeat(p["sgu_b"][l].T, HD, axis=1),
        conv_d_w=p["conv_d_w"][l],
        m_head=((head_of[:, None] == head_of[None, :]).astype(F32) / HD).astype(BF16),
        m_all=jnp.full((W_GRP, W_GRP), 1.0 / W_GRP, BF16),
    )


def kernel(x_prompt, x_sample, state_rglru, c, c_ctx, w_ada, b_ada, norm1_g, norm2_g, w_in, conv_a_w,
           conv_a_b, norm_a_g, norm_a_b, conv_b_w, conv_b_b, lru_wr, lru_br, lru_wi, lru_bi, lru_lam,
           sgu_norm_g, sgu_norm_b, sgu_w, sgu_b, conv_d_w, w_out, w_router, b_router, w_gu, b_gu, w_dn,
           b_dn, final_g):
    p = dict(conv_a_w=conv_a_w, conv_a_b=conv_a_b, norm_a_g=norm_a_g, norm_a_b=norm_a_b,
             conv_b_w=conv_b_w, conv_b_b=conv_b_b, lru_wr=lru_wr, lru_br=lru_br, lru_wi=lru_wi,
             lru_bi=lru_bi, lru_lam=lru_lam, sgu_norm_g=sgu_norm_g, sgu_norm_b=sgu_norm_b,
             sgu_w=sgu_w, sgu_b=sgu_b, conv_d_w=conv_d_w)
    bp, tp, _ = x_prompt.shape
    bs, ts, _ = x_sample.shape
    n_p, n_s = bp * tp, bs * ts

    cond_rows = jnp.zeros((COND_ROWS, D_MODEL), F32).at[0].set(c_ctx).at[1:1 + bs].set(c)
    mod = _ada_mod(cond_rows, w_ada, b_ada).reshape(DEPTH, COND_ROWS, 6, D_MODEL)

    xp = x_prompt.reshape(n_p, D_MODEL)
    xs = x_sample.reshape(n_s, D_MODEL)
    h0_ctx = jnp.zeros((bp, 2, W_GRP), F32)
    moe_p = moe_s = mod_p_prev = mod_s_prev = None
    states = []
    for l in range(DEPTH):
        lw = _layer_weights(l, p)
        mod_p, mod_s = mod[l, 0:1], mod[l, 1:1 + bs]
        w_in_l = w_in[l].astype(BF16)
        w_out_l = w_out[l].astype(BF16)
        xp, proj_p = _in_proj(xp, moe_p, mod_p_prev, mod_p, norm1_g[l], w_in_l, tp)
        xs, proj_s = _in_proj(xs, moe_s, mod_s_prev, mod_s, norm1_g[l], w_in_l, ts)
        y_p, st = _mixers(proj_p, h0_ctx, lw, bp, tp, False)
        y_s, _ = _mixers(proj_s, state_rglru[:, l], lw, bs, ts, True)
        states.append(st)
        xp, hn_p, lg_p = _out_proj(y_p, xp, mod_p, norm2_g[l], w_out_l, w_router[l], b_router[l], tp)
        xs, hn_s, lg_s = _out_proj(y_s, xs, mod_s, norm2_g[l], w_out_l, w_router[l], b_router[l], ts)
        moe = _moe(jnp.concatenate([hn_p, hn_s]), jnp.concatenate([lg_p, lg_s]),
                   w_gu[l], b_gu[l], w_dn[l], b_dn[l])
        moe_p, moe_s = moe[:n_p], moe[n_p:]
        mod_p_prev, mod_s_prev = mod_p, mod_s
    y_prompt = _final_norm(xp, moe_p, mod_p_prev, final_g, tp).reshape(bp, tp, D_MODEL)
    y_sample = _final_norm(xs, moe_s, mod_s_prev, final_g, ts).reshape(bs, ts, D_MODEL)
    return y_prompt, y_sample, jnp.stack(states, axis=1)
```

```python
import functools

import jax
import jax.numpy as jnp
from jax import lax
from jax.experimental import pallas as pl
from jax.experimental.pallas import tpu as pltpu

F32 = jnp.float32
BF16 = jnp.bfloat16

D_MODEL = 1024
DEPTH = 2
GRID_W = 64
N_MIXERS = 4
W_GRP = D_MODEL // N_MIXERS
HEADS = 4
HD = W_GRP // HEADS
N_IN = 9 * W_GRP
CONV_A = 31
CONV_B = 4
CONV_D = 3
CHUNK = 128
LRU_C = 8.0
N_EXPERTS = 32
TOP_K = 4
D_FF = D_MODEL
SWIGLU_LIMIT = 7.0
SWIGLU_ALPHA = 1.702
EPS = 1e-6

V7X_SUBLANES = 8
V7X_VMEM_BYTES = 64 * 1024 * 1024
VMEM_LIMIT = V7X_VMEM_BYTES * 7 // 8

COND_ROWS = 16
ADA_TN = 1536
ROW_TILE = 512
CONV_ROWS = 64
CONV_PAD = 16
MOE_TM = 512
DISPATCH_TM = 512
ROUTE_LANES = 128


def _rmsnorm(x, g):
    return x * lax.rsqrt(jnp.mean(x * x, axis=-1, keepdims=True) + EPS) * g


def _group_mean(x, m_ref):
    hi = x.astype(BF16)
    lo = (x - hi.astype(F32)).astype(BF16)
    m = m_ref[...]
    return (jnp.dot(hi, m, preferred_element_type=F32) + jnp.dot(lo, m, preferred_element_type=F32))


def _layernorm(x, g, b, m_ref):
    xc = x - _group_mean(x, m_ref)
    var = _group_mean(xc * xc, m_ref)
    return xc * lax.rsqrt(var + EPS) * g + b


def _ada_kernel(c_ref, w_ref, b_ref, o_ref):
    c = c_ref[...]
    cond = (c * jax.nn.sigmoid(c)).astype(BF16)
    o_ref[...] = jnp.dot(cond, w_ref[...].astype(BF16), preferred_element_type=F32) + b_ref[...]


def _ada_mod(cond_rows, w_ada, b_ada):
    n_col = w_ada.shape[-1]
    return pl.pallas_call(
        _ada_kernel,
        grid=(DEPTH, n_col // ADA_TN),
        in_specs=[
            pl.BlockSpec((COND_ROWS, D_MODEL), lambda l, j: (0, 0)),
            pl.BlockSpec((None, D_MODEL, ADA_TN), lambda l, j: (l, 0, j)),
            pl.BlockSpec((None, 1, ADA_TN), lambda l, j: (l, 0, j)),
        ],
        out_specs=pl.BlockSpec((None, COND_ROWS, ADA_TN), lambda l, j: (l, 0, j)),
        out_shape=jax.ShapeDtypeStruct((DEPTH, COND_ROWS, n_col), F32),
        name="ada_mod",
    )(cond_rows, w_ada, b_ada.reshape(DEPTH, 1, n_col))


def _combine_experts(dest_cur, dest_nxt, pr_ref, ybuf, rows, sems):
    i = pl.program_id(0)
    n_steps = pl.num_programs(0)
    tm = rows.shape[2]
    slot = i % 2

    def row_copy(dref, j, k, slot_):
        return pltpu.make_async_copy(ybuf.at[pl.ds(dref[0, j * TOP_K + k], 1)],
                                     rows.at[slot_, k, pl.ds(j, 1)], sems.at[slot_])

    def issue(dref, slot_):
        def body(j, carry):
            for k in range(TOP_K):
                row_copy(dref, j, k, slot_).start()
            return carry

        lax.fori_loop(0, tm, body, 0, unroll=4)

    @pl.when(i == 0)
    def _():
        issue(dest_cur, 0)

    @pl.when(i + 1 < n_steps)
    def _():
        issue(dest_nxt, 1 - slot)

    for k in range(TOP_K):
        pltpu.make_async_copy(ybuf.at[pl.ds(0, tm)], rows.at[slot, k], sems.at[slot]).wait()
    moe = pr_ref[:, 0:1] * rows[slot, 0]
    for k in range(1, TOP_K):
        moe = moe + pr_ref[:, k:k + 1] * rows[slot, k]
    return moe


def _inproj_kernel(*refs, has_res):
    if has_res:
        (dest_cur, dest_nxt, x_ref, pr_ref, ybuf, modp_ref, mod_ref, g_ref, w_ref,
         xo_ref, p_ref, rows, sems) = refs
        x = x_ref[...] + modp_ref[5:6, :] * _combine_experts(dest_cur, dest_nxt, pr_ref, ybuf, rows, sems)
        xo_ref[...] = x
    else:
        x_ref, mod_ref, g_ref, w_ref, p_ref = refs
        x = x_ref[...]
    hn = _rmsnorm(x, g_ref[...]) * (1.0 + mod_ref[1:2, :]) + mod_ref[0:1, :]
    p_ref[...] = jnp.dot(hn.astype(BF16), w_ref[...], preferred_element_type=F32)


def _mod_spec(mod, seq, tm):
    if mod.shape[0] == 1:
        return pl.BlockSpec((None, 6, D_MODEL), lambda i: (0, 0, 0))
    return pl.BlockSpec((None, 6, D_MODEL), lambda i: ((i * tm) // seq, 0, 0))


def _combine_operands(dest, probs, y_buf, n, tm):
    n_steps = n // tm
    dest3 = dest.reshape(n_steps, 1, tm * TOP_K)
    smem = functools.partial(pl.BlockSpec, (None, 1, tm * TOP_K), memory_space=pltpu.SMEM)
    args = [dest3, dest3, probs, y_buf]
    specs = [smem(lambda i: (i, 0, 0)),
             smem(lambda i: (jnp.minimum(i + 1, n_steps - 1), 0, 0)),
             pl.BlockSpec((tm, ROUTE_LANES), lambda i: (i, 0)),
             pl.BlockSpec(memory_space=pl.ANY)]
    scratch = [pltpu.VMEM((2, TOP_K, tm, D_MODEL), F32), pltpu.SemaphoreType.DMA((2,))]
    return args, specs, scratch


def _in_proj(x, route, mod_prev, mod, g, w_bf16, seq):
    n = x.shape[0]
    tm = min(ROW_TILE, seq)
    row = pl.BlockSpec((tm, D_MODEL), lambda i: (i, 0))
    has_res = route is not None
    args, specs, scratch = [], [], []
    if has_res:
        args, specs, scratch = _combine_operands(*route, n, tm)
        args = args[:2] + [x] + args[2:] + [mod_prev]
        specs = specs[:2] + [row] + specs[2:] + [_mod_spec(mod_prev, seq, tm)]
    else:
        args, specs = [x], [row]
    args += [mod, g.reshape(1, D_MODEL), w_bf16]
    specs += [_mod_spec(mod, seq, tm),
              pl.BlockSpec((1, D_MODEL), lambda i: (0, 0)),
              pl.BlockSpec((D_MODEL, N_IN), lambda i: (0, 0))]
    proj_shape = jax.ShapeDtypeStruct((n, N_IN), F32)
    proj_spec = pl.BlockSpec((tm, N_IN), lambda i: (i, 0))
    if has_res:
        out_shape = (jax.ShapeDtypeStruct((n, D_MODEL), F32), proj_shape)
        out_specs = (row, proj_spec)
    else:
        out_shape, out_specs = proj_shape, proj_spec
    out = pl.pallas_call(
        functools.partial(_inproj_kernel, has_res=has_res),
        grid=(n // tm,),
        in_specs=specs,
        out_specs=out_specs,
        out_shape=out_shape,
        scratch_shapes=scratch,
        compiler_params=pltpu.CompilerParams(dimension_semantics=("arbitrary",), vmem_limit_bytes=VMEM_LIMIT),
        name="in_proj",
    )(*args)
    return out if has_res else (x, out)


def _group_scan(a, b, reverse):
    rows = a.shape[0]
    ri = lax.broadcasted_iota(jnp.int32, a.shape, 0) & (V7X_SUBLANES - 1)
    for d in (1, 2, 4):
        if reverse:
            keep = ri < V7X_SUBLANES - d
            ra, rb = pltpu.roll(a, rows - d, 0), pltpu.roll(b, rows - d, 0)
        else:
            keep = ri >= d
            ra, rb = pltpu.roll(a, d, 0), pltpu.roll(b, d, 0)
        b = a * jnp.where(keep, rb, 0.0) + b
        a = a * jnp.where(keep, ra, 1.0)
    return a, b


def _mixer_kernel(s0, s1, s2, caw, cab, nag, nab, cbw, cbb, wg, bg, lam, h0, sng, snb, sw, sbias, cdw,
                  m_head, m_all, y_ref, st_ref, pad_s, af_s, bf_s, ab_s, bb_s, *, seq, on_grid):
    mixer = pl.program_id(1)
    n_conv = seq // CONV_ROWS
    win_rows = CONV_ROWS + 2 * CONV_PAD

    def rows_at(c, size):
        return pl.ds(pl.multiple_of(c * size, size), size)

    def fill_padded(fn):
        zeros = jnp.zeros((CONV_PAD, W_GRP), F32)
        pad_s[0:CONV_PAD, :] = zeros
        pad_s[CONV_PAD + seq:2 * CONV_PAD + seq, :] = zeros

        def body(c, carry):
            dst = pl.ds(pl.multiple_of(c * CONV_ROWS + CONV_PAD, V7X_SUBLANES), CONV_ROWS)
            pad_s[dst, :] = fn(rows_at(c, CONV_ROWS))
            return carry

        lax.fori_loop(0, n_conv, body, 0)

    def conv_rows(c, w_ref, taps, pad_l, row_mask):
        win = pad_s[pl.ds(pl.multiple_of(c * CONV_ROWS, CONV_ROWS), win_rows), :]
        acc = jnp.zeros((CONV_ROWS, W_GRP), F32)
        ri = lax.broadcasted_iota(jnp.int32, (CONV_ROWS, W_GRP), 0)
        for k in range(taps):
            off = k - pad_l
            sh = win[CONV_PAD + off:CONV_PAD + off + CONV_ROWS, :]
            if row_mask:
                sh = jnp.where((ri >= -off) & (ri < CONV_ROWS - off), sh, 0.0)
            acc = acc + w_ref[k:k + 1, :] * sh
        return acc

    @pl.when(mixer == 0)
    def _conformer():
        def glu(rows):
            return s0[rows, :] * jax.nn.sigmoid(s1[rows, :])

        fill_padded(glu)

        def body(c, carry):
            z = conv_rows(c, caw, CONV_A, CONV_A // 2, on_grid) + cab[...]
            z = _layernorm(z, nag[...], nab[...], m_head)
            y_ref[rows_at(c, CONV_ROWS), :] = (z * jax.nn.sigmoid(z)).astype(BF16)
            return carry

        lax.fori_loop(0, n_conv, body, 0)

    @pl.when(mixer == 1)
    def _rglru():
        fill_padded(lambda rows: s0[rows, :])
        lam_v = lam[...]
        softplus_neg = jnp.maximum(-lam_v, 0.0) + jnp.log1p(jnp.exp(-jnp.abs(lam_v)))

        def gates(c, carry):
            rows = rows_at(c, CONV_ROWS)
            xc = conv_rows(c, cbw, CONV_B, 2, False) + cbb[...]
            gt = jax.nn.sigmoid(jnp.dot(xc.astype(BF16), wg[...], preferred_element_type=F32) + bg[...])
            for d, (a_s, b_s) in enumerate(((af_s, bf_s), (ab_s, bb_s))):
                r_gate = gt[:, (2 * d) * W_GRP:(2 * d + 1) * W_GRP]
                i_gate = gt[:, (2 * d + 1) * W_GRP:(2 * d + 2) * W_GRP]
                log_a = -LRU_C * r_gate * softplus_neg[d:d + 1, :]
                a = jnp.exp(log_a)
                b = jnp.sqrt(jnp.maximum(-jnp.tanh(log_a) * (a * a + 1.0), 0.0)) * (i_gate * xc)
                a, b = _group_scan(a, b, reverse=(d == 1))
                a_s[rows, :] = a
                b_s[rows, :] = b
            return carry

        lax.fori_loop(0, n_conv, gates, 0)

        n_grp = seq // V7X_SUBLANES

        def chain(g, carry):
            cf, cb = carry
            rf = rows_at(g, V7X_SUBLANES)
            rb = rows_at(n_grp - 1 - g, V7X_SUBLANES)
            hf = af_s[rf, :] * cf + bf_s[rf, :]
            bf_s[rf, :] = hf
            hb = ab_s[rb, :] * cb + bb_s[rb, :]
            bb_s[rb, :] = hb
            cf = jnp.broadcast_to(hf[V7X_SUBLANES - 1:V7X_SUBLANES, :], (V7X_SUBLANES, W_GRP))
            cb = jnp.broadcast_to(hb[0:1, :], (V7X_SUBLANES, W_GRP))
            return cf, cb

        init = (jnp.broadcast_to(h0[0:1, :], (V7X_SUBLANES, W_GRP)),
                jnp.broadcast_to(h0[1:2, :], (V7X_SUBLANES, W_GRP)))
        cf, cb = lax.fori_loop(0, n_grp, chain, init, unroll=4)
        st_ref[0:1, :] = cf[0:1, :]
        st_ref[1:2, :] = cb[0:1, :]

        def out(c, carry):
            rows = rows_at(c, CONV_ROWS)
            y_ref[rows, :] = (jax.nn.gelu(s1[rows, :]) * (bf_s[rows, :] + bb_s[rows, :])).astype(BF16)
            return carry

        lax.fori_loop(0, n_conv, out, 0)

    @pl.when(mixer == 2)
    def _sgu():
        lane = lax.broadcasted_iota(jnp.int32, (CHUNK, W_GRP), 1)

        def body(n, carry):
            rows = rows_at(n, CHUNK)
            v = _layernorm(s1[rows, :], sng[...], snb[...], m_all).astype(BF16)
            s = sbias[...]
            for h in range(HEADS):
                sh = jnp.dot(sw[h], v, preferred_element_type=F32)
                s = s + jnp.where((lane >= h * HD) & (lane < (h + 1) * HD), sh, 0.0)
            y_ref[rows, :] = (s0[rows, :] * s).astype(BF16)
            return carry

        lax.fori_loop(0, seq // CHUNK, body, 0)

    @pl.when(mixer == 3)
    def _gated_conv():
        fill_padded(lambda rows: s1[rows, :] * s2[rows, :])

        def body(c, carry):
            rows = rows_at(c, CONV_ROWS)
            y_ref[rows, :] = (s0[rows, :] * conv_rows(c, cdw, CONV_D, CONV_D // 2, False)).astype(BF16)
            return carry

        lax.fori_loop(0, n_conv, body, 0)


def _mixers(proj, h0, lw, batch, seq, on_grid):
    proj3 = proj.reshape(batch, seq, N_IN)
    slab = (None, seq, W_GRP)

    def const(shape):
        return pl.BlockSpec(shape, lambda b, m: (0,) * len(shape))

    in_specs = [
        pl.BlockSpec(slab, lambda b, m: (b, 0, 2 * m)),
        pl.BlockSpec(slab, lambda b, m: (b, 0, 2 * m + 1)),
        pl.BlockSpec(slab, lambda b, m: (b, 0, 8)),
        const((CONV_A, W_GRP)), const((1, W_GRP)), const((1, W_GRP)), const((1, W_GRP)),
        const((CONV_B, W_GRP)), const((1, W_GRP)),
        const((W_GRP, 4 * W_GRP)), const((1, 4 * W_GRP)), const((2, W_GRP)),
        pl.BlockSpec((None, 2, W_GRP), lambda b, m: (b, 0, 0)),
        const((1, W_GRP)), const((1, W_GRP)), const((HEADS, CHUNK, CHUNK)), const((CHUNK, W_GRP)),
        const((CONV_D, W_GRP)), const((W_GRP, W_GRP)), const((W_GRP, W_GRP)),
    ]
    y, st = pl.pallas_call(
        functools.partial(_mixer_kernel, seq=seq, on_grid=on_grid),
        grid=(batch, N_MIXERS),
        in_specs=in_specs,
        out_specs=(pl.BlockSpec(slab, lambda b, m: (b, 0, m)),
                   pl.BlockSpec((None, 2, W_GRP), lambda b, m: (b, 0, 0))),
        out_shape=(jax.ShapeDtypeStruct((batch, seq, D_MODEL), BF16),
                   jax.ShapeDtypeStruct((batch, 2, W_GRP), F32)),
        scratch_shapes=[pltpu.VMEM((seq + 2 * CONV_PAD, W_GRP), F32)] + [pltpu.VMEM((seq, W_GRP), F32)] * 4,
        compiler_params=pltpu.CompilerParams(dimension_semantics=("arbitrary", "arbitrary"),
                                             vmem_limit_bytes=VMEM_LIMIT),
        name="mixers",
    )(proj3, proj3, proj3, lw["conv_a_w"], lw["conv_a_b"], lw["norm_a_g"], lw["norm_a_b"],
      lw["conv_b_w"], lw["conv_b_b"], lw["gate_w"], lw["gate_b"], lw["lru_lam"], h0,
      lw["sgu_norm_g"], lw["sgu_norm_b"], lw["sgu_w"], lw["sgu_bias"], lw["conv_d_w"],
      lw["m_head"], lw["m_all"])
    return y.reshape(batch * seq, D_MODEL), st


def _outproj_kernel(y_ref, x_ref, mod_ref, g_ref, wo_ref, wr_ref, br_ref, tri_ref, cin_ref,
                    xm_ref, hn_ref, ei_ref, pr_ref, cnt_ref, seen_s):
    tm = x_ref.shape[0]

    @pl.when(pl.program_id(0) == 0)
    def _():
        seen_s[...] = cin_ref[...]

    y = jnp.dot(y_ref[...], wo_ref[...], preferred_element_type=F32)
    x = x_ref[...] + mod_ref[2:3, :] * y
    xm_ref[...] = x
    hn = _rmsnorm(x, g_ref[...]) * (1.0 + mod_ref[4:5, :]) + mod_ref[3:4, :]
    hn_ref[...] = hn
    logits = jnp.dot(hn, wr_ref[...], preferred_element_type=F32,
                     precision=lax.Precision.HIGHEST) + br_ref[...]

    lane = lax.broadcasted_iota(jnp.int32, (tm, N_EXPERTS), 1).astype(F32)
    work = logits
    vals, ids, sels = [], [], []
    for _ in range(TOP_K):
        mx = jnp.max(work, axis=-1, keepdims=True)
        idx = jnp.min(jnp.where(work == mx, lane, float(N_EXPERTS)), axis=-1, keepdims=True)
        sel = lane == idx
        work = jnp.where(sel, -jnp.inf, work)
        vals.append(mx)
        ids.append(idx)
        sels.append(sel)
    exps = [jnp.exp(v - vals[0]) for v in vals]
    den = exps[0] + exps[1] + exps[2] + exps[3]

    chosen = jnp.where(sels[0] | sels[1] | sels[2] | sels[3], 1.0, 0.0)
    earlier = jnp.dot(tri_ref[...], chosen.astype(BF16), preferred_element_type=F32) + seen_s[...]
    seen_s[...] = earlier[tm - 1:tm, :] + chosen[tm - 1:tm, :]
    cnt_ref[...] = seen_s[...]

    out_lane = lax.broadcasted_iota(jnp.int32, (tm, ROUTE_LANES), 1)
    ei = jnp.zeros((tm, ROUTE_LANES), F32)
    pr = jnp.zeros((tm, ROUTE_LANES), F32)
    for k in range(TOP_K):
        rank = jnp.sum(jnp.where(sels[k], earlier, 0.0), axis=-1, keepdims=True)
        ei = jnp.where(out_lane == k, ids[k], ei)
        ei = jnp.where(out_lane == TOP_K + k, rank, ei)
        pr = jnp.where(out_lane == k, exps[k] / den, pr)
    ei_ref[...] = ei.astype(jnp.int32)
    pr_ref[...] = pr


def _out_proj(y, x, mod, g, wo_bf16, w_router, b_router, seen, seq):
    n = x.shape[0]
    tm = min(ROW_TILE, seq)
    row = pl.BlockSpec((tm, D_MODEL), lambda i: (i, 0))
    route = pl.BlockSpec((tm, ROUTE_LANES), lambda i: (i, 0))
    cnt = pl.BlockSpec((1, N_EXPERTS), lambda i: (0, 0))
    tri = jnp.tri(tm, k=-1, dtype=BF16)
    return pl.pallas_call(
        _outproj_kernel,
        grid=(n // tm,),
        in_specs=[row, row, _mod_spec(mod, seq, tm),
                  pl.BlockSpec((1, D_MODEL), lambda i: (0, 0)),
                  pl.BlockSpec((D_MODEL, D_MODEL), lambda i: (0, 0)),
                  pl.BlockSpec((D_MODEL, N_EXPERTS), lambda i: (0, 0)),
                  cnt, pl.BlockSpec((tm, tm), lambda i: (0, 0)), cnt],
        out_specs=(row, row, route, route, cnt),
        out_shape=(jax.ShapeDtypeStruct((n, D_MODEL), F32),
                   jax.ShapeDtypeStruct((n, D_MODEL), F32),
                   jax.ShapeDtypeStruct((n, ROUTE_LANES), jnp.int32),
                   jax.ShapeDtypeStruct((n, ROUTE_LANES), F32),
                   jax.ShapeDtypeStruct((1, N_EXPERTS), F32)),
        scratch_shapes=[pltpu.VMEM((1, N_EXPERTS), F32)],
        compiler_params=pltpu.CompilerParams(dimension_semantics=("arbitrary",)),
        name="out_proj",
    )(y, x, mod, g.reshape(1, D_MODEL), wo_bf16, w_router, b_router.reshape(1, N_EXPERTS), tri, seen)


def _dispatch_kernel(zero_blk, zero_on, dest_ref, hn_p, hn_s, xbuf, zeros_s, sem, zsem, *, tiles_p):
    i = pl.program_id(0)
    n_steps = pl.num_programs(0)

    @pl.when(i == 0)
    def _():
        zeros_s[...] = jnp.zeros_like(zeros_s)

        def fill(j):
            return pltpu.make_async_copy(
                zeros_s, xbuf.at[pl.ds(pl.multiple_of(zero_blk[j] * MOE_TM, MOE_TM), MOE_TM)], zsem)

        for j in range(zero_blk.shape[0]):
            @pl.when(zero_on[j] == 1)
            def _():
                fill(j).start()

        for j in range(zero_blk.shape[0]):
            @pl.when(zero_on[j] == 1)
            def _():
                fill(j).wait()

    def issue(src, tile):
        def body(j, carry):
            row = src.at[pl.ds(tile * DISPATCH_TM + j, 1)]
            for k in range(TOP_K):
                pltpu.make_async_copy(row, xbuf.at[pl.ds(dest_ref[0, j * TOP_K + k], 1)], sem).start()
            return carry

        lax.fori_loop(0, DISPATCH_TM, body, 0, unroll=4)

    @pl.when(i < tiles_p)
    def _():
        issue(hn_p, i)

    @pl.when(i >= tiles_p)
    def _():
        issue(hn_s, i - tiles_p)

    def wait_tile():
        rows = TOP_K * DISPATCH_TM
        pltpu.make_async_copy(hn_s.at[pl.ds(0, rows)], xbuf.at[pl.ds(0, rows)], sem).wait()

    @pl.when(i > 0)
    def _():
        wait_tile()

    @pl.when(i == n_steps - 1)
    def _():
        wait_tile()


def _dispatch(dest, zero_blk, zero_on, hn_p, hn_s, n_pad):
    n_p, n_s = hn_p.shape[0], hn_s.shape[0]
    n_steps = (n_p + n_s) // DISPATCH_TM
    grid_spec = pltpu.PrefetchScalarGridSpec(
        num_scalar_prefetch=2,
        grid=(n_steps,),
        in_specs=[pl.BlockSpec((None, 1, DISPATCH_TM * TOP_K), lambda i, zb, zo: (i, 0, 0),
                               memory_space=pltpu.SMEM),
                  pl.BlockSpec(memory_space=pl.ANY), pl.BlockSpec(memory_space=pl.ANY)],
        out_specs=pl.BlockSpec(memory_space=pl.ANY),
        scratch_shapes=[pltpu.VMEM((MOE_TM, D_MODEL), F32), pltpu.SemaphoreType.DMA(()),
                        pltpu.SemaphoreType.DMA(())],
    )
    return pl.pallas_call(
        functools.partial(_dispatch_kernel, tiles_p=n_p // DISPATCH_TM),
        grid_spec=grid_spec,
        out_shape=jax.ShapeDtypeStruct((n_pad, D_MODEL), F32),
        compiler_params=pltpu.CompilerParams(dimension_semantics=("arbitrary",)),
        name="moe_dispatch",
    )(zero_blk, zero_on, dest.reshape(n_steps, 1, DISPATCH_TM * TOP_K), hn_p, hn_s)


def _moe_kernel(blk_e, blk_new, blk_on, x_ref, wgu_ref, bgu_ref, wdn_ref, bdn_ref, o_ref, wgu_s, wdn_s):
    i = pl.program_id(0)

    @pl.when(blk_on[i] == 0)
    def _():
        o_ref[...] = jnp.zeros_like(o_ref)

    @pl.when(blk_on[i] == 1)
    def _():
        @pl.when(blk_new[i] == 1)
        def _():
            wgu_s[...] = wgu_ref[...].astype(BF16)
            wdn_s[...] = wdn_ref[...].astype(BF16)

        gu = jnp.dot(x_ref[...].astype(BF16), wgu_s[...], preferred_element_type=F32) + bgu_ref[...]
        g = jnp.minimum(gu[:, :D_FF], SWIGLU_LIMIT)
        u = jnp.clip(gu[:, D_FF:], -SWIGLU_LIMIT, SWIGLU_LIMIT)
        act = (u + 1.0) * (g * jax.nn.sigmoid(SWIGLU_ALPHA * g))
        o_ref[...] = jnp.dot(act.astype(BF16), wdn_s[...], preferred_element_type=F32) + bdn_ref[...]


def _moe_blocks(x_buf, blk_e, blk_new, blk_on, w_gu, b_gu, w_dn, b_dn):
    n_pad = x_buf.shape[0]
    n_blk = n_pad // MOE_TM
    grid_spec = pltpu.PrefetchScalarGridSpec(
        num_scalar_prefetch=3,
        grid=(n_blk,),
        in_specs=[
            pl.BlockSpec((MOE_TM, D_MODEL), lambda i, e, nw, on: (i, 0)),
            pl.BlockSpec((None, D_MODEL, 2 * D_FF), lambda i, e, nw, on: (e[i], 0, 0)),
            pl.BlockSpec((None, 1, 2 * D_FF), lambda i, e, nw, on: (e[i], 0, 0)),
            pl.BlockSpec((None, D_FF, D_MODEL), lambda i, e, nw, on: (e[i], 0, 0)),
            pl.BlockSpec((None, 1, D_MODEL), lambda i, e, nw, on: (e[i], 0, 0)),
        ],
        out_specs=pl.BlockSpec((MOE_TM, D_MODEL), lambda i, e, nw, on: (i, 0)),
        scratch_shapes=[pltpu.VMEM((D_MODEL, 2 * D_FF), BF16), pltpu.VMEM((D_FF, D_MODEL), BF16)],
    )
    return pl.pallas_call(
        _moe_kernel,
        grid_spec=grid_spec,
        out_shape=jax.ShapeDtypeStruct((n_pad, D_MODEL), F32),
        compiler_params=pltpu.CompilerParams(dimension_semantics=("arbitrary",),
                                             vmem_limit_bytes=VMEM_LIMIT),
        name="moe_experts",
    )(blk_e, blk_new, blk_on, x_buf, w_gu, b_gu.reshape(N_EXPERTS, 1, 2 * D_FF),
      w_dn, b_dn.reshape(N_EXPERTS, 1, D_MODEL))


def _moe(hn_p, hn_s, ei_p, ei_s, counts, w_gu, b_gu, w_dn, b_dn):
    n_asg = (hn_p.shape[0] + hn_s.shape[0]) * TOP_K
    n_pad = (n_asg + N_EXPERTS * (MOE_TM - 1) + MOE_TM - 1) // MOE_TM * MOE_TM
    n_blk = n_pad // MOE_TM
    counts = counts.reshape(N_EXPERTS).astype(jnp.int32)
    padded = (counts + MOE_TM - 1) // MOE_TM * MOE_TM
    pad_end = jnp.cumsum(padded)
    pad_start = pad_end - padded
    blk_row = jnp.arange(n_blk, dtype=jnp.int32) * MOE_TM
    blk_e = jnp.minimum(jnp.sum((blk_row[:, None] >= pad_end[None, :]).astype(jnp.int32), axis=1),
                        N_EXPERTS - 1)
    blk_on = (blk_row < pad_end[-1]).astype(jnp.int32)
    blk_new = jnp.concatenate([jnp.ones((1,), jnp.int32), (blk_e[1:] != blk_e[:-1]).astype(jnp.int32)])

    def slots(ei):
        expert, rank = ei[:, 0:TOP_K], ei[:, TOP_K:2 * TOP_K]
        start = jnp.sum(jnp.where(expert[:, :, None] == jnp.arange(N_EXPERTS, dtype=jnp.int32),
                                  pad_start, 0), axis=-1)
        return (start + rank).reshape(-1)

    tail = n_blk - n_asg // MOE_TM
    last_on = (counts % MOE_TM != 0).astype(jnp.int32)
    tail_blk = pad_end[-1] // MOE_TM + jnp.arange(tail, dtype=jnp.int32)
    tail_on = (tail_blk < n_blk).astype(jnp.int32)
    zero_blk = jnp.concatenate([(pad_end // MOE_TM - 1) * last_on, tail_blk * tail_on])
    zero_on = jnp.concatenate([last_on, tail_on])

    dest_p, dest_s = slots(ei_p), slots(ei_s)
    x_buf = _dispatch(jnp.concatenate([dest_p, dest_s]), zero_blk, zero_on, hn_p, hn_s, n_pad)
    y_buf = _moe_blocks(x_buf, blk_e, blk_new, blk_on, w_gu, b_gu, w_dn, b_dn)
    return dest_p, dest_s, y_buf


def _final_kernel(dest_cur, dest_nxt, x_ref, pr_ref, ybuf, mod_ref, g_ref, o_ref, rows, sems):
    x = x_ref[...] + mod_ref[5:6, :] * _combine_experts(dest_cur, dest_nxt, pr_ref, ybuf, rows, sems)
    o_ref[...] = _rmsnorm(x, g_ref[...])


def _final_norm(x, route, mod, g, seq):
    n = x.shape[0]
    tm = min(ROW_TILE, seq)
    row = pl.BlockSpec((tm, D_MODEL), lambda i: (i, 0))
    args, specs, scratch = _combine_operands(*route, n, tm)
    return pl.pallas_call(
        _final_kernel,
        grid=(n // tm,),
        in_specs=specs[:2] + [row] + specs[2:] + [_mod_spec(mod, seq, tm),
                                                  pl.BlockSpec((1, D_MODEL), lambda i: (0, 0))],
        out_specs=row,
        out_shape=jax.ShapeDtypeStruct((n, D_MODEL), F32),
        scratch_shapes=scratch,
        compiler_params=pltpu.CompilerParams(dimension_semantics=("arbitrary",), vmem_limit_bytes=VMEM_LIMIT),
        name="final_norm",
    )(*args[:2], x, *args[2:], mod, g.reshape(1, D_MODEL))


def _block_diag(w):
    eye = jnp.eye(HEADS, dtype=w.dtype)
    return (eye[:, None, :, None] * w[:, :, None, :]).reshape(W_GRP, W_GRP)


def _layer_weights(l, p):
    gate_w = jnp.concatenate([_block_diag(p["lru_wr"][l, 0]), _block_diag(p["lru_wi"][l, 0]),
                              _block_diag(p["lru_wr"][l, 1]), _block_diag(p["lru_wi"][l, 1])], axis=1)
    gate_b = jnp.concatenate([p["lru_br"][l, 0], p["lru_bi"][l, 0], p["lru_br"][l, 1], p["lru_bi"][l, 1]])
    head_of = jnp.arange(W_GRP) // HD
    row = lambda v: v.reshape(1, W_GRP)
    return dict(
        conv_a_w=p["conv_a_w"][l], conv_a_b=row(p["conv_a_b"][l]),
        norm_a_g=row(p["norm_a_g"][l]), norm_a_b=row(p["norm_a_b"][l]),
        conv_b_w=p["conv_b_w"][l], conv_b_b=row(p["conv_b_b"][l]),
        gate_w=gate_w.astype(BF16), gate_b=gate_b.reshape(1, 4 * W_GRP), lru_lam=p["lru_lam"][l],
        sgu_norm_g=row(p["sgu_norm_g"][l]), sgu_norm_b=row(p["sgu_norm_b"][l]),
        sgu_w=p["sgu_w"][l].astype(BF16), sgu_bias=jnp.repeat(p["sgu_b"][l].T, HD, axis=1),
        conv_d_w=p["conv_d_w"][l],
        m_head=((head_of[:, None] == head_of[None, :]).astype(F32) / HD).astype(BF16),
        m_all=jnp.full((W_GRP, W_GRP), 1.0 / W_GRP, BF16),
    )


def kernel(x_prompt, x_sample, state_rglru, c, c_ctx, w_ada, b_ada, norm1_g, norm2_g, w_in, conv_a_w,
           conv_a_b, norm_a_g, norm_a_b, conv_b_w, conv_b_b, lru_wr, lru_br, lru_wi, lru_bi, lru_lam,
           sgu_norm_g, sgu_norm_b, sgu_w, sgu_b, conv_d_w, w_out, w_router, b_router, w_gu, b_gu, w_dn,
           b_dn, final_g):
    p = dict(conv_a_w=conv_a_w, conv_a_b=conv_a_b, norm_a_g=norm_a_g, norm_a_b=norm_a_b,
             conv_b_w=conv_b_w, conv_b_b=conv_b_b, lru_wr=lru_wr, lru_br=lru_br, lru_wi=lru_wi,
             lru_bi=lru_bi, lru_lam=lru_lam, sgu_norm_g=sgu_norm_g, sgu_norm_b=sgu_norm_b,
             sgu_w=sgu_w, sgu_b=sgu_b, conv_d_w=conv_d_w)
    bp, tp, _ = x_prompt.shape
    bs, ts, _ = x_sample.shape
    n_p, n_s = bp * tp, bs * ts

    cond_rows = jnp.zeros((COND_ROWS, D_MODEL), F32).at[0].set(c_ctx).at[1:1 + bs].set(c)
    mod = _ada_mod(cond_rows, w_ada, b_ada).reshape(DEPTH, COND_ROWS, 6, D_MODEL)

    xp = x_prompt.reshape(n_p, D_MODEL)
    xs = x_sample.reshape(n_s, D_MODEL)
    h0_ctx = jnp.zeros((bp, 2, W_GRP), F32)
    route_p = route_s = mod_p_prev = mod_s_prev = None
    no_tokens_seen = jnp.zeros((1, N_EXPERTS), F32)
    states = []
    for l in range(DEPTH):
        lw = _layer_weights(l, p)
        mod_p, mod_s = mod[l, 0:1], mod[l, 1:1 + bs]
        w_in_l = w_in[l].astype(BF16)
        w_out_l = w_out[l].astype(BF16)
        xp, proj_p = _in_proj(xp, route_p, mod_p_prev, mod_p, norm1_g[l], w_in_l, tp)
        xs, proj_s = _in_proj(xs, route_s, mod_s_prev, mod_s, norm1_g[l], w_in_l, ts)
        y_p, st = _mixers(proj_p, h0_ctx, lw, bp, tp, False)
        y_s, _ = _mixers(proj_s, state_rglru[:, l], lw, bs, ts, True)
        states.append(st)
        xp, hn_p, ei_p, pr_p, seen = _out_proj(y_p, xp, mod_p, norm2_g[l], w_out_l, w_router[l],
                                               b_router[l], no_tokens_seen, tp)
        xs, hn_s, ei_s, pr_s, counts = _out_proj(y_s, xs, mod_s, norm2_g[l], w_out_l, w_router[l],
                                                 b_router[l], seen, ts)
        dest_p, dest_s, y_buf = _moe(hn_p, hn_s, ei_p, ei_s, counts, w_gu[l], b_gu[l], w_dn[l], b_dn[l])
        route_p, route_s = (dest_p, pr_p, y_buf), (dest_s, pr_s, y_buf)
        mod_p_prev, mod_s_prev = mod_p, mod_s
    y_prompt = _final_norm(xp, route_p, mod_p_prev, final_g, tp).reshape(bp, tp, D_MODEL)
    y_sample = _final_norm(xs, route_s, mod_s_prev, final_g, ts).reshape(bs, ts, D_MODEL)
    return y_prompt, y_sample, jnp.stack(states, axis=1)
```

```python
import functools

import jax
import jax.numpy as jnp
from jax import lax
from jax.experimental import pallas as pl
from jax.experimental.pallas import tpu as pltpu

F32 = jnp.float32
BF16 = jnp.bfloat16

D_MODEL = 1024
DEPTH = 2
GRID_W = 64
N_MIXERS = 4
W_GRP = D_MODEL // N_MIXERS
HEADS = 4
HD = W_GRP // HEADS
N_IN = 9 * W_GRP
CONV_A = 31
CONV_B = 4
CONV_D = 3
CHUNK = 128
LRU_C = 8.0
N_EXPERTS = 32
TOP_K = 4
D_FF = D_MODEL
SWIGLU_LIMIT = 7.0
SWIGLU_ALPHA = 1.702
EPS = 1e-6

V7X_SUBLANES = 8
V7X_VMEM_BYTES = 64 * 1024 * 1024
VMEM_LIMIT = V7X_VMEM_BYTES * 7 // 8

COND_ROWS = 16
ADA_TN = 1536
ROW_TILE = 512
CONV_ROWS = GRID_W
CONV_PAD = 16
NORM_ROWS = 256
MOE_TM = 512
DISPATCH_TM = 512
ROUTE_LANES = 128


def _rmsnorm(x, g):
    return x * lax.rsqrt(jnp.mean(x * x, axis=-1, keepdims=True) + EPS) * g


def _group_mean(x, m_ref):
    hi = x.astype(BF16)
    lo = (x - hi.astype(F32)).astype(BF16)
    m = m_ref[...]
    return (jnp.dot(hi, m, preferred_element_type=F32) + jnp.dot(lo, m, preferred_element_type=F32))


def _layernorm(x, g, b, m_ref):
    xc = x - _group_mean(x, m_ref)
    var = _group_mean(xc * xc, m_ref)
    return xc * lax.rsqrt(var + EPS) * g + b


def _ada_kernel(c_ref, w_ref, b_ref, o_ref):
    c = c_ref[...]
    cond = (c * jax.nn.sigmoid(c)).astype(BF16)
    o_ref[...] = jnp.dot(cond, w_ref[...].astype(BF16), preferred_element_type=F32) + b_ref[...]


def _ada_mod(cond_rows, w_ada, b_ada):
    n_col = w_ada.shape[-1]
    return pl.pallas_call(
        _ada_kernel,
        grid=(DEPTH, n_col // ADA_TN),
        in_specs=[
            pl.BlockSpec((COND_ROWS, D_MODEL), lambda l, j: (0, 0)),
            pl.BlockSpec((None, D_MODEL, ADA_TN), lambda l, j: (l, 0, j)),
            pl.BlockSpec((None, 1, ADA_TN), lambda l, j: (l, 0, j)),
        ],
        out_specs=pl.BlockSpec((None, COND_ROWS, ADA_TN), lambda l, j: (l, 0, j)),
        out_shape=jax.ShapeDtypeStruct((DEPTH, COND_ROWS, n_col), F32),
        name="ada_mod",
    )(cond_rows, w_ada, b_ada.reshape(DEPTH, 1, n_col))


def _combine_experts(dest_cur, dest_nxt, pr_ref, ybuf, rows, sems):
    i = pl.program_id(0)
    n_steps = pl.num_programs(0)
    tm = rows.shape[2]
    slot = i % 2

    def row_copy(dref, j, k, slot_):
        return pltpu.make_async_copy(ybuf.at[pl.ds(dref[0, j * TOP_K + k], 1)],
                                     rows.at[slot_, k, pl.ds(j, 1)], sems.at[slot_])

    def issue(dref, slot_):
        def body(j, carry):
            for k in range(TOP_K):
                row_copy(dref, j, k, slot_).start()
            return carry

        lax.fori_loop(0, tm, body, 0, unroll=4)

    @pl.when(i == 0)
    def _():
        issue(dest_cur, 0)

    @pl.when(i + 1 < n_steps)
    def _():
        issue(dest_nxt, 1 - slot)

    for k in range(TOP_K):
        pltpu.make_async_copy(ybuf.at[pl.ds(0, tm)], rows.at[slot, k], sems.at[slot]).wait()
    moe = pr_ref[:, 0:1] * rows[slot, 0]
    for k in range(1, TOP_K):
        moe = moe + pr_ref[:, k:k + 1] * rows[slot, k]
    return moe


def _inproj_kernel(*refs, has_res):
    if has_res:
        (dest_cur, dest_nxt, x_ref, pr_ref, ybuf, modp_ref, mod_ref, g_ref, w_ref,
         xo_ref, p_ref, rows, sems) = refs
        x = x_ref[...] + modp_ref[5:6, :] * _combine_experts(dest_cur, dest_nxt, pr_ref, ybuf, rows, sems)
        xo_ref[...] = x
    else:
        x_ref, mod_ref, g_ref, w_ref, p_ref = refs
        x = x_ref[...]
    hn = _rmsnorm(x, g_ref[...]) * (1.0 + mod_ref[1:2, :]) + mod_ref[0:1, :]
    p_ref[...] = jnp.dot(hn.astype(BF16), w_ref[...], preferred_element_type=F32)


def _mod_spec(mod, seq, tm):
    if mod.shape[0] == 1:
        return pl.BlockSpec((None, 6, D_MODEL), lambda i: (0, 0, 0))
    return pl.BlockSpec((None, 6, D_MODEL), lambda i: ((i * tm) // seq, 0, 0))


def _combine_operands(dest, probs, y_buf, n, tm):
    n_steps = n // tm
    dest3 = dest.reshape(n_steps, 1, tm * TOP_K)
    smem = functools.partial(pl.BlockSpec, (None, 1, tm * TOP_K), memory_space=pltpu.SMEM)
    args = [dest3, dest3, probs, y_buf]
    specs = [smem(lambda i: (i, 0, 0)),
             smem(lambda i: (jnp.minimum(i + 1, n_steps - 1), 0, 0)),
             pl.BlockSpec((tm, ROUTE_LANES), lambda i: (i, 0)),
             pl.BlockSpec(memory_space=pl.ANY)]
    scratch = [pltpu.VMEM((2, TOP_K, tm, D_MODEL), F32), pltpu.SemaphoreType.DMA((2,))]
    return args, specs, scratch


def _in_proj(x, route, mod_prev, mod, g, w_bf16, seq):
    n = x.shape[0]
    tm = min(ROW_TILE, seq)
    row = pl.BlockSpec((tm, D_MODEL), lambda i: (i, 0))
    has_res = route is not None
    args, specs, scratch = [], [], []
    if has_res:
        args, specs, scratch = _combine_operands(*route, n, tm)
        args = args[:2] + [x] + args[2:] + [mod_prev]
        specs = specs[:2] + [row] + specs[2:] + [_mod_spec(mod_prev, seq, tm)]
    else:
        args, specs = [x], [row]
    args += [mod, g.reshape(1, D_MODEL), w_bf16]
    specs += [_mod_spec(mod, seq, tm),
              pl.BlockSpec((1, D_MODEL), lambda i: (0, 0)),
              pl.BlockSpec((D_MODEL, N_IN), lambda i: (0, 0))]
    proj_shape = jax.ShapeDtypeStruct((n, N_IN), F32)
    proj_spec = pl.BlockSpec((tm, N_IN), lambda i: (i, 0))
    if has_res:
        out_shape = (jax.ShapeDtypeStruct((n, D_MODEL), F32), proj_shape)
        out_specs = (row, proj_spec)
    else:
        out_shape, out_specs = proj_shape, proj_spec
    out = pl.pallas_call(
        functools.partial(_inproj_kernel, has_res=has_res),
        grid=(n // tm,),
        in_specs=specs,
        out_specs=out_specs,
        out_shape=out_shape,
        scratch_shapes=scratch,
        compiler_params=pltpu.CompilerParams(dimension_semantics=("arbitrary",), vmem_limit_bytes=VMEM_LIMIT),
        name="in_proj",
    )(*args)
    return out if has_res else (x, out)


def _group_scan(a, b, reverse):
    shape = a.shape
    grouped = (shape[0] // V7X_SUBLANES, V7X_SUBLANES, shape[1])
    a, b = a.reshape(grouped), b.reshape(grouped)
    ri = lax.broadcasted_iota(jnp.int32, grouped, 1)
    for d in (1, 2, 4):
        shift = V7X_SUBLANES - d if reverse else d
        keep = ri < V7X_SUBLANES - d if reverse else ri >= d
        ra, rb = pltpu.roll(a, shift, 1), pltpu.roll(b, shift, 1)
        b = a * jnp.where(keep, rb, 0.0) + b
        a = a * jnp.where(keep, ra, 1.0)
    return a.reshape(shape), b.reshape(shape)


def _mixer_kernel(s0, s1, s2, caw, cab, nag, nab, cbw, cbb, wg, bg, lam, h0, sng, snb, sw, sbias, cdw,
                  m_head, m_all, y_ref, st_ref, pad_s, af_s, bf_s, ab_s, bb_s, *, seq, on_grid):
    mixer = pl.program_id(1)
    n_conv = seq // CONV_ROWS
    win_rows = CONV_ROWS + 2 * CONV_PAD

    def rows_at(c, size):
        return pl.ds(pl.multiple_of(c * size, size), size)

    def fill_padded(fn):
        zeros = jnp.zeros((CONV_PAD, W_GRP), F32)
        pad_s[0:CONV_PAD, :] = zeros
        pad_s[CONV_PAD + seq:2 * CONV_PAD + seq, :] = zeros

        def body(c, carry):
            dst = pl.ds(pl.multiple_of(c * CONV_ROWS + CONV_PAD, V7X_SUBLANES), CONV_ROWS)
            pad_s[dst, :] = fn(rows_at(c, CONV_ROWS))
            return carry

        lax.fori_loop(0, n_conv, body, 0)

    def conv_window(win, w_ref, taps, pad_l):
        acc = jnp.zeros((CONV_ROWS, W_GRP), F32)
        for mis in range(V7X_SUBLANES):
            starts = [(k, CONV_PAD - pad_l + k) for k in range(taps)
                      if (CONV_PAD - pad_l + k) % V7X_SUBLANES == mis]
            if not starts:
                continue
            shifted = pltpu.roll(win, win_rows - mis, 0) if mis else win
            for k, start in starts:
                acc = acc + w_ref[k:k + 1, :] * shifted[start - mis:start - mis + CONV_ROWS, :]
        return acc

    def conv_rows(c, w_ref, taps, pad_l):
        win = pad_s[pl.ds(pl.multiple_of(c * CONV_ROWS, CONV_ROWS), win_rows), :]
        return conv_window(win, w_ref, taps, pad_l)

    @pl.when(mixer == 0)
    def _conformer():
        def glu(rows):
            return s0[rows, :] * jax.nn.sigmoid(s1[rows, :])

        if not on_grid:
            fill_padded(glu)

        def body(c, carry):
            if on_grid:
                edge = jnp.zeros((CONV_PAD, W_GRP), F32)
                win = jnp.concatenate([edge, glu(rows_at(c, CONV_ROWS)), edge], axis=0)
                z = conv_window(win, caw, CONV_A, CONV_A // 2)
            else:
                z = conv_rows(c, caw, CONV_A, CONV_A // 2)
            af_s[rows_at(c, CONV_ROWS), :] = z + cab[...]
            return carry

        lax.fori_loop(0, n_conv, body, 0)

        def norm(c, carry):
            rows = rows_at(c, NORM_ROWS)
            z = _layernorm(af_s[rows, :], nag[...], nab[...], m_head)
            y_ref[rows, :] = (z * jax.nn.sigmoid(z)).astype(BF16)
            return carry

        lax.fori_loop(0, seq // NORM_ROWS, norm, 0, unroll=2)

    @pl.when(mixer == 1)
    def _rglru():
        fill_padded(lambda rows: s0[rows, :])
        lam_v = lam[...]
        softplus_neg = jnp.maximum(-lam_v, 0.0) + jnp.log1p(jnp.exp(-jnp.abs(lam_v)))

        def gates(c, carry):
            rows = rows_at(c, CONV_ROWS)
            xc = conv_rows(c, cbw, CONV_B, 2) + cbb[...]
            gt = jax.nn.sigmoid(jnp.dot(xc.astype(BF16), wg[...], preferred_element_type=F32) + bg[...])
            for d, (a_s, b_s) in enumerate(((af_s, bf_s), (ab_s, bb_s))):
                r_gate = gt[:, (2 * d) * W_GRP:(2 * d + 1) * W_GRP]
                i_gate = gt[:, (2 * d + 1) * W_GRP:(2 * d + 2) * W_GRP]
                log_a = -LRU_C * r_gate * softplus_neg[d:d + 1, :]
                a = jnp.exp(log_a)
                b = jnp.sqrt(jnp.maximum(-jnp.tanh(log_a) * (a * a + 1.0), 0.0)) * (i_gate * xc)
                a, b = _group_scan(a, b, reverse=(d == 1))
                a_s[rows, :] = a
                b_s[rows, :] = b
            return carry

        lax.fori_loop(0, n_conv, gates, 0)

        n_grp = seq // V7X_SUBLANES

        def chain(g, carry):
            cf, cb = carry
            rf = rows_at(g, V7X_SUBLANES)
            rb = rows_at(n_grp - 1 - g, V7X_SUBLANES)
            hf = af_s[rf, :] * cf + bf_s[rf, :]
            bf_s[rf, :] = hf
            hb = ab_s[rb, :] * cb + bb_s[rb, :]
            bb_s[rb, :] = hb
            cf = jnp.broadcast_to(hf[V7X_SUBLANES - 1:V7X_SUBLANES, :], (V7X_SUBLANES, W_GRP))
            cb = jnp.broadcast_to(hb[0:1, :], (V7X_SUBLANES, W_GRP))
            return cf, cb

        init = (jnp.broadcast_to(h0[0:1, :], (V7X_SUBLANES, W_GRP)),
                jnp.broadcast_to(h0[1:2, :], (V7X_SUBLANES, W_GRP)))
        cf, cb = lax.fori_loop(0, n_grp, chain, init, unroll=4)
        st_ref[0:1, :] = cf[0:1, :]
        st_ref[1:2, :] = cb[0:1, :]

        def out(c, carry):
            rows = rows_at(c, CONV_ROWS)
            y_ref[rows, :] = (jax.nn.gelu(s1[rows, :]) * (bf_s[rows, :] + bb_s[rows, :])).astype(BF16)
            return carry

        lax.fori_loop(0, n_conv, out, 0)

    @pl.when(mixer == 2)
    def _sgu():
        lane = lax.broadcasted_iota(jnp.int32, (CHUNK, W_GRP), 1)

        def body(n, carry):
            rows = rows_at(n, CHUNK)
            v = _layernorm(s1[rows, :], sng[...], snb[...], m_all).astype(BF16)
            s = sbias[...]
            for h in range(HEADS):
                sh = jnp.dot(sw[h], v, preferred_element_type=F32)
                s = s + jnp.where((lane >= h * HD) & (lane < (h + 1) * HD), sh, 0.0)
            y_ref[rows, :] = (s0[rows, :] * s).astype(BF16)
            return carry

        lax.fori_loop(0, seq // CHUNK, body, 0, unroll=4)

    @pl.when(mixer == 3)
    def _gated_conv():
        fill_padded(lambda rows: s1[rows, :] * s2[rows, :])

        def body(c, carry):
            rows = rows_at(c, CONV_ROWS)
            y_ref[rows, :] = (s0[rows, :] * conv_rows(c, cdw, CONV_D, CONV_D // 2)).astype(BF16)
            return carry

        lax.fori_loop(0, n_conv, body, 0)


def _mixers(proj, h0, lw, batch, seq, on_grid):
    proj3 = proj.reshape(batch, seq, N_IN)
    slab = (None, seq, W_GRP)

    def const(shape):
        return pl.BlockSpec(shape, lambda b, m: (0,) * len(shape))

    in_specs = [
        pl.BlockSpec(slab, lambda b, m: (b, 0, 2 * m)),
        pl.BlockSpec(slab, lambda b, m: (b, 0, 2 * m + 1)),
        pl.BlockSpec(slab, lambda b, m: (b, 0, 8)),
        const((CONV_A, W_GRP)), const((1, W_GRP)), const((1, W_GRP)), const((1, W_GRP)),
        const((CONV_B, W_GRP)), const((1, W_GRP)),
        const((W_GRP, 4 * W_GRP)), const((1, 4 * W_GRP)), const((2, W_GRP)),
        pl.BlockSpec((None, 2, W_GRP), lambda b, m: (b, 0, 0)),
        const((1, W_GRP)), const((1, W_GRP)), const((HEADS, CHUNK, CHUNK)), const((CHUNK, W_GRP)),
        const((CONV_D, W_GRP)), const((W_GRP, W_GRP)), const((W_GRP, W_GRP)),
    ]
    y, st = pl.pallas_call(
        functools.partial(_mixer_kernel, seq=seq, on_grid=on_grid),
        grid=(batch, N_MIXERS),
        in_specs=in_specs,
        out_specs=(pl.BlockSpec(slab, lambda b, m: (b, 0, m)),
                   pl.BlockSpec((None, 2, W_GRP), lambda b, m: (b, 0, 0))),
        out_shape=(jax.ShapeDtypeStruct((batch, seq, D_MODEL), BF16),
                   jax.ShapeDtypeStruct((batch, 2, W_GRP), F32)),
        scratch_shapes=[pltpu.VMEM((seq + 2 * CONV_PAD, W_GRP), F32)] + [pltpu.VMEM((seq, W_GRP), F32)] * 4,
        compiler_params=pltpu.CompilerParams(dimension_semantics=("arbitrary", "arbitrary"),
                                             vmem_limit_bytes=VMEM_LIMIT),
        name="mixers",
    )(proj3, proj3, proj3, lw["conv_a_w"], lw["conv_a_b"], lw["norm_a_g"], lw["norm_a_b"],
      lw["conv_b_w"], lw["conv_b_b"], lw["gate_w"], lw["gate_b"], lw["lru_lam"], h0,
      lw["sgu_norm_g"], lw["sgu_norm_b"], lw["sgu_w"], lw["sgu_bias"], lw["conv_d_w"],
      lw["m_head"], lw["m_all"])
    return y.reshape(batch * seq, D_MODEL), st


def _outproj_kernel(y_ref, x_ref, mod_ref, g_ref, wo_ref, wr_ref, br_ref, tri_ref, cin_ref,
                    xm_ref, hn_ref, ei_ref, pr_ref, cnt_ref, seen_s):
    tm = x_ref.shape[0]

    @pl.when(pl.program_id(0) == 0)
    def _():
        seen_s[...] = cin_ref[...]

    y = jnp.dot(y_ref[...], wo_ref[...], preferred_element_type=F32)
    x = x_ref[...] + mod_ref[2:3, :] * y
    xm_ref[...] = x
    hn = _rmsnorm(x, g_ref[...]) * (1.0 + mod_ref[4:5, :]) + mod_ref[3:4, :]
    hn_ref[...] = hn
    logits = jnp.dot(hn, wr_ref[...], preferred_element_type=F32,
                     precision=lax.Precision.HIGHEST) + br_ref[...]

    lane = lax.broadcasted_iota(jnp.int32, (tm, N_EXPERTS), 1).astype(F32)
    work = logits
    vals, ids, sels = [], [], []
    for _ in range(TOP_K):
        mx = jnp.max(work, axis=-1, keepdims=True)
        idx = jnp.min(jnp.where(work == mx, lane, float(N_EXPERTS)), axis=-1, keepdims=True)
        sel = lane == idx
        work = jnp.where(sel, -jnp.inf, work)
        vals.append(mx)
        ids.append(idx)
        sels.append(sel)
    exps = [jnp.exp(v - vals[0]) for v in vals]
    den = exps[0] + exps[1] + exps[2] + exps[3]

    chosen = jnp.where(sels[0] | sels[1] | sels[2] | sels[3], 1.0, 0.0)
    earlier = jnp.dot(tri_ref[...], chosen.astype(BF16), preferred_element_type=F32) + seen_s[...]
    seen_s[...] = earlier[tm - 1:tm, :] + chosen[tm - 1:tm, :]
    cnt_ref[...] = seen_s[...]

    out_lane = lax.broadcasted_iota(jnp.int32, (tm, ROUTE_LANES), 1)
    ei = jnp.zeros((tm, ROUTE_LANES), F32)
    pr = jnp.zeros((tm, ROUTE_LANES), F32)
    for k in range(TOP_K):
        rank = jnp.sum(jnp.where(sels[k], earlier, 0.0), axis=-1, keepdims=True)
        ei = jnp.where(out_lane == k, ids[k], ei)
        ei = jnp.where(out_lane == TOP_K + k, rank, ei)
        pr = jnp.where(out_lane == k, exps[k] / den, pr)
    ei_ref[...] = ei.astype(jnp.int32)
    pr_ref[...] = pr


def _out_proj(y, x, mod, g, wo_bf16, w_router, b_router, seen, seq):
    n = x.shape[0]
    tm = min(ROW_TILE, seq)
    row = pl.BlockSpec((tm, D_MODEL), lambda i: (i, 0))
    route = pl.BlockSpec((tm, ROUTE_LANES), lambda i: (i, 0))
    cnt = pl.BlockSpec((1, N_EXPERTS), lambda i: (0, 0))
    tri = jnp.tri(tm, k=-1, dtype=BF16)
    return pl.pallas_call(
        _outproj_kernel,
        grid=(n // tm,),
        in_specs=[row, row, _mod_spec(mod, seq, tm),
                  pl.BlockSpec((1, D_MODEL), lambda i: (0, 0)),
                  pl.BlockSpec((D_MODEL, D_MODEL), lambda i: (0, 0)),
                  pl.BlockSpec((D_MODEL, N_EXPERTS), lambda i: (0, 0)),
                  cnt, pl.BlockSpec((tm, tm), lambda i: (0, 0)), cnt],
        out_specs=(row, row, route, route, cnt),
        out_shape=(jax.ShapeDtypeStruct((n, D_MODEL), F32),
                   jax.ShapeDtypeStruct((n, D_MODEL), F32),
                   jax.ShapeDtypeStruct((n, ROUTE_LANES), jnp.int32),
                   jax.ShapeDtypeStruct((n, ROUTE_LANES), F32),
                   jax.ShapeDtypeStruct((1, N_EXPERTS), F32)),
        scratch_shapes=[pltpu.VMEM((1, N_EXPERTS), F32)],
        compiler_params=pltpu.CompilerParams(dimension_semantics=("arbitrary",)),
        name="out_proj",
    )(y, x, mod, g.reshape(1, D_MODEL), wo_bf16, w_router, b_router.reshape(1, N_EXPERTS), tri, seen)


def _dispatch_kernel(zero_blk, zero_on, dest_ref, hn_p, hn_s, xbuf, zeros_s, sem, zsem, *, tiles_p):
    i = pl.program_id(0)

    @pl.when(i == 0)
    def _():
        zeros_s[...] = jnp.zeros_like(zeros_s)

        def fill(j):
            return pltpu.make_async_copy(
                zeros_s, xbuf.at[pl.ds(pl.multiple_of(zero_blk[j] * MOE_TM, MOE_TM), MOE_TM)], zsem)

        for j in range(zero_blk.shape[0]):
            @pl.when(zero_on[j] == 1)
            def _():
                fill(j).start()

        for j in range(zero_blk.shape[0]):
            @pl.when(zero_on[j] == 1)
            def _():
                fill(j).wait()

    def issue(src):
        def body(j, carry):
            row = src.at[pl.ds(j, 1)]
            for k in range(TOP_K):
                pltpu.make_async_copy(row, xbuf.at[pl.ds(dest_ref[0, j * TOP_K + k], 1)], sem).start()
            return carry

        lax.fori_loop(0, DISPATCH_TM, body, 0, unroll=4)

    @pl.when(i < tiles_p)
    def _():
        issue(hn_p)

    @pl.when(i >= tiles_p)
    def _():
        issue(hn_s)

    for _ in range(TOP_K):
        pltpu.make_async_copy(hn_s, xbuf.at[pl.ds(0, DISPATCH_TM)], sem).wait()


def _dispatch(dest, zero_blk, zero_on, hn_p, hn_s, n_pad):
    n_p, n_s = hn_p.shape[0], hn_s.shape[0]
    tiles_p = n_p // DISPATCH_TM
    n_steps = (n_p + n_s) // DISPATCH_TM
    grid_spec = pltpu.PrefetchScalarGridSpec(
        num_scalar_prefetch=2,
        grid=(n_steps,),
        in_specs=[pl.BlockSpec((None, 1, DISPATCH_TM * TOP_K), lambda i, zb, zo: (i, 0, 0),
                               memory_space=pltpu.SMEM),
                  pl.BlockSpec((DISPATCH_TM, D_MODEL), lambda i, zb, zo: (jnp.minimum(i, tiles_p - 1), 0)),
                  pl.BlockSpec((DISPATCH_TM, D_MODEL), lambda i, zb, zo: (jnp.maximum(i - tiles_p, 0), 0))],
        out_specs=pl.BlockSpec(memory_space=pl.ANY),
        scratch_shapes=[pltpu.VMEM((MOE_TM, D_MODEL), F32), pltpu.SemaphoreType.DMA(()),
                        pltpu.SemaphoreType.DMA(())],
    )
    return pl.pallas_call(
        functools.partial(_dispatch_kernel, tiles_p=tiles_p),
        grid_spec=grid_spec,
        out_shape=jax.ShapeDtypeStruct((n_pad, D_MODEL), F32),
        compiler_params=pltpu.CompilerParams(dimension_semantics=("arbitrary",)),
        name="moe_dispatch",
    )(zero_blk, zero_on, dest.reshape(n_steps, 1, DISPATCH_TM * TOP_K), hn_p, hn_s)


def _moe_kernel(blk_e, blk_new, blk_on, x_ref, wgu_ref, bgu_ref, wdn_ref, bdn_ref, o_ref, wgu_s, wdn_s):
    i = pl.program_id(0)

    @pl.when(blk_on[i] == 0)
    def _():
        o_ref[...] = jnp.zeros_like(o_ref)

    @pl.when(blk_on[i] == 1)
    def _():
        @pl.when(blk_new[i] == 1)
        def _():
            wgu_s[...] = wgu_ref[...].astype(BF16)
            wdn_s[...] = wdn_ref[...].astype(BF16)

        gu = jnp.dot(x_ref[...].astype(BF16), wgu_s[...], preferred_element_type=F32) + bgu_ref[...]
        g = jnp.minimum(gu[:, :D_FF], SWIGLU_LIMIT)
        u = jnp.clip(gu[:, D_FF:], -SWIGLU_LIMIT, SWIGLU_LIMIT)
        act = (u + 1.0) * (g * jax.nn.sigmoid(SWIGLU_ALPHA * g))
        o_ref[...] = jnp.dot(act.astype(BF16), wdn_s[...], preferred_element_type=F32) + bdn_ref[...]


def _moe_blocks(x_buf, blk_e, blk_new, blk_on, layer, w_gu, b_gu, w_dn, b_dn):
    n_pad = x_buf.shape[0]
    n_blk = n_pad // MOE_TM
    grid_spec = pltpu.PrefetchScalarGridSpec(
        num_scalar_prefetch=3,
        grid=(n_blk,),
        in_specs=[
            pl.BlockSpec((MOE_TM, D_MODEL), lambda i, e, nw, on: (i, 0)),
            pl.BlockSpec((None, None, D_MODEL, 2 * D_FF), lambda i, e, nw, on: (layer, e[i], 0, 0)),
            pl.BlockSpec((None, 1, 2 * D_FF), lambda i, e, nw, on: (e[i], 0, 0)),
            pl.BlockSpec((None, None, D_FF, D_MODEL), lambda i, e, nw, on: (layer, e[i], 0, 0)),
            pl.BlockSpec((None, 1, D_MODEL), lambda i, e, nw, on: (e[i], 0, 0)),
        ],
        out_specs=pl.BlockSpec((MOE_TM, D_MODEL), lambda i, e, nw, on: (i, 0)),
        scratch_shapes=[pltpu.VMEM((D_MODEL, 2 * D_FF), BF16), pltpu.VMEM((D_FF, D_MODEL), BF16)],
    )
    return pl.pallas_call(
        _moe_kernel,
        grid_spec=grid_spec,
        out_shape=jax.ShapeDtypeStruct((n_pad, D_MODEL), F32),
        compiler_params=pltpu.CompilerParams(dimension_semantics=("arbitrary",),
                                             vmem_limit_bytes=VMEM_LIMIT),
        name="moe_experts",
    )(blk_e, blk_new, blk_on, x_buf, w_gu, b_gu.reshape(N_EXPERTS, 1, 2 * D_FF),
      w_dn, b_dn.reshape(N_EXPERTS, 1, D_MODEL))


def _moe(hn_p, hn_s, ei_p, ei_s, counts, layer, w_gu, b_gu, w_dn, b_dn):
    n_asg = (hn_p.shape[0] + hn_s.shape[0]) * TOP_K
    n_pad = (n_asg + N_EXPERTS * (MOE_TM - 1) + MOE_TM - 1) // MOE_TM * MOE_TM
    n_blk = n_pad // MOE_TM
    counts = counts.reshape(N_EXPERTS).astype(jnp.int32)
    padded = (counts + MOE_TM - 1) // MOE_TM * MOE_TM
    pad_end = jnp.cumsum(padded)
    pad_start = pad_end - padded
    blk_row = jnp.arange(n_blk, dtype=jnp.int32) * MOE_TM
    blk_e = jnp.minimum(jnp.sum((blk_row[:, None] >= pad_end[None, :]).astype(jnp.int32), axis=1),
                        N_EXPERTS - 1)
    blk_on = (blk_row < pad_end[-1]).astype(jnp.int32)
    blk_new = jnp.concatenate([jnp.ones((1,), jnp.int32), (blk_e[1:] != blk_e[:-1]).astype(jnp.int32)])

    def slots(ei):
        expert, rank = ei[:, 0:TOP_K], ei[:, TOP_K:2 * TOP_K]
        start = jnp.sum(jnp.where(expert[:, :, None] == jnp.arange(N_EXPERTS, dtype=jnp.int32),
                                  pad_start, 0), axis=-1)
        return (start + rank).reshape(-1)

    tail = n_blk - n_asg // MOE_TM
    last_on = (counts % MOE_TM != 0).astype(jnp.int32)
    tail_blk = pad_end[-1] // MOE_TM + jnp.arange(tail, dtype=jnp.int32)
    tail_on = (tail_blk < n_blk).astype(jnp.int32)
    zero_blk = jnp.concatenate([(pad_end // MOE_TM - 1) * last_on, tail_blk * tail_on])
    zero_on = jnp.concatenate([last_on, tail_on])

    dest_p, dest_s = slots(ei_p), slots(ei_s)
    x_buf = _dispatch(jnp.concatenate([dest_p, dest_s]), zero_blk, zero_on, hn_p, hn_s, n_pad)
    y_buf = _moe_blocks(x_buf, blk_e, blk_new, blk_on, layer, w_gu, b_gu, w_dn, b_dn)
    return dest_p, dest_s, y_buf


def _final_kernel(dest_cur, dest_nxt, x_ref, pr_ref, ybuf, mod_ref, g_ref, o_ref, rows, sems):
    x = x_ref[...] + mod_ref[5:6, :] * _combine_experts(dest_cur, dest_nxt, pr_ref, ybuf, rows, sems)
    o_ref[...] = _rmsnorm(x, g_ref[...])


def _final_norm(x, route, mod, g, seq):
    n = x.shape[0]
    tm = min(ROW_TILE, seq)
    row = pl.BlockSpec((tm, D_MODEL), lambda i: (i, 0))
    args, specs, scratch = _combine_operands(*route, n, tm)
    return pl.pallas_call(
        _final_kernel,
        grid=(n // tm,),
        in_specs=specs[:2] + [row] + specs[2:] + [_mod_spec(mod, seq, tm),
                                                  pl.BlockSpec((1, D_MODEL), lambda i: (0, 0))],
        out_specs=row,
        out_shape=jax.ShapeDtypeStruct((n, D_MODEL), F32),
        scratch_shapes=scratch,
        compiler_params=pltpu.CompilerParams(dimension_semantics=("arbitrary",), vmem_limit_bytes=VMEM_LIMIT),
        name="final_norm",
    )(*args[:2], x, *args[2:], mod, g.reshape(1, D_MODEL))


def _block_diag(w):
    eye = jnp.eye(HEADS, dtype=w.dtype)
    return (eye[:, None, :, None] * w[:, :, None, :]).reshape(W_GRP, W_GRP)


def _layer_weights(l, p):
    gate_w = jnp.concatenate([_block_diag(p["lru_wr"][l, 0]), _block_diag(p["lru_wi"][l, 0]),
                              _block_diag(p["lru_wr"][l, 1]), _block_diag(p["lru_wi"][l, 1])], axis=1)
    gate_b = jnp.concatenate([p["lru_br"][l, 0], p["lru_bi"][l, 0], p["lru_br"][l, 1], p["lru_bi"][l, 1]])
    head_of = jnp.arange(W_GRP) // HD
    row = lambda v: v.reshape(1, W_GRP)
    return dict(
        conv_a_w=p["conv_a_w"][l], conv_a_b=row(p["conv_a_b"][l]),
        norm_a_g=row(p["norm_a_g"][l]), norm_a_b=row(p["norm_a_b"][l]),
        conv_b_w=p["conv_b_w"][l], conv_b_b=row(p["conv_b_b"][l]),
        gate_w=gate_w.astype(BF16), gate_b=gate_b.reshape(1, 4 * W_GRP), lru_lam=p["lru_lam"][l],
        sgu_norm_g=row(p["sgu_norm_g"][l]), sgu_norm_b=row(p["sgu_norm_b"][l]),
        sgu_w=p["sgu_w"][l].astype(BF16), sgu_bias=jnp.repeat(p["sgu_b"][l].T, HD, axis=1),
        conv_d_w=p["conv_d_w"][l],
        m_head=((head_of[:, None] == head_of[None, :]).astype(F32) / HD).astype(BF16),
        m_all=jnp.full((W_GRP, W_GRP), 1.0 / W_GRP, BF16),
    )


def kernel(x_prompt, x_sample, state_rglru, c, c_ctx, w_ada, b_ada, norm1_g, norm2_g, w_in, conv_a_w,
           conv_a_b, norm_a_g, norm_a_b, conv_b_w, conv_b_b, lru_wr, lru_br, lru_wi, lru_bi, lru_lam,
           sgu_norm_g, sgu_norm_b, sgu_w, sgu_b, conv_d_w, w_out, w_router, b_router, w_gu, b_gu, w_dn,
           b_dn, final_g):
    p = dict(conv_a_w=conv_a_w, conv_a_b=conv_a_b, norm_a_g=norm_a_g, norm_a_b=norm_a_b,
             conv_b_w=conv_b_w, conv_b_b=conv_b_b, lru_wr=lru_wr, lru_br=lru_br, lru_wi=lru_wi,
             lru_bi=lru_bi, lru_lam=lru_lam, sgu_norm_g=sgu_norm_g, sgu_norm_b=sgu_norm_b,
             sgu_w=sgu_w, sgu_b=sgu_b, conv_d_w=conv_d_w)
    bp, tp, _ = x_prompt.shape
    bs, ts, _ = x_sample.shape
    n_p, n_s = bp * tp, bs * ts

    cond_rows = jnp.zeros((COND_ROWS, D_MODEL), F32).at[0].set(c_ctx).at[1:1 + bs].set(c)
    mod = _ada_mod(cond_rows, w_ada, b_ada).reshape(DEPTH, COND_ROWS, 6, D_MODEL)

    xp = x_prompt.reshape(n_p, D_MODEL)
    xs = x_sample.reshape(n_s, D_MODEL)
    h0_ctx = jnp.zeros((bp, 2, W_GRP), F32)
    route_p = route_s = mod_p_prev = mod_s_prev = None
    no_tokens_seen = jnp.zeros((1, N_EXPERTS), F32)
    states = []
    for l in range(DEPTH):
        lw = _layer_weights(l, p)
        mod_p, mod_s = mod[l, 0:1], mod[l, 1:1 + bs]
        w_in_l = w_in[l].astype(BF16)
        w_out_l = w_out[l].astype(BF16)
        xp, proj_p = _in_proj(xp, route_p, mod_p_prev, mod_p, norm1_g[l], w_in_l, tp)
        xs, proj_s = _in_proj(xs, route_s, mod_s_prev, mod_s, norm1_g[l], w_in_l, ts)
        y_p, st = _mixers(proj_p, h0_ctx, lw, bp, tp, False)
        y_s, _ = _mixers(proj_s, state_rglru[:, l], lw, bs, ts, True)
        states.append(st)
        xp, hn_p, ei_p, pr_p, seen = _out_proj(y_p, xp, mod_p, norm2_g[l], w_out_l, w_router[l],
                                               b_router[l], no_tokens_seen, tp)
        xs, hn_s, ei_s, pr_s, counts = _out_proj(y_s, xs, mod_s, norm2_g[l], w_out_l, w_router[l],
                                                 b_router[l], seen, ts)
        dest_p, dest_s, y_buf = _moe(hn_p, hn_s, ei_p, ei_s, counts, l, w_gu, b_gu[l], w_dn, b_dn[l])
        route_p, route_s = (dest_p, pr_p, y_buf), (dest_s, pr_s, y_buf)
        mod_p_prev, mod_s_prev = mod_p, mod_s
    y_prompt = _final_norm(xp, route_p, mod_p_prev, final_g, tp).reshape(bp, tp, D_MODEL)
    y_sample = _final_norm(xs, route_s, mod_s_prev, final_g, ts).reshape(bs, ts, D_MODEL)
    return y_prompt, y_sample, jnp.stack(states, axis=1)
```

```python
import functools

import jax
import jax.numpy as jnp
from jax import lax
from jax.experimental import pallas as pl
from jax.experimental.pallas import tpu as pltpu

F32 = jnp.float32
BF16 = jnp.bfloat16

D_MODEL = 1024
DEPTH = 2
GRID_W = 64
N_MIXERS = 4
W_GRP = D_MODEL // N_MIXERS
HEADS = 4
HD = W_GRP // HEADS
N_IN = 9 * W_GRP
CONV_A = 31
CONV_B = 4
CONV_D = 3
CHUNK = 128
LRU_C = 8.0
N_EXPERTS = 32
TOP_K = 4
D_FF = D_MODEL
SWIGLU_LIMIT = 7.0
SWIGLU_ALPHA = 1.702
EPS = 1e-6

V7X_SUBLANES = 8
V7X_LANES = 128
TOKEN_ROWS = D_MODEL // V7X_LANES
V7X_VMEM_BYTES = 64 * 1024 * 1024
VMEM_LIMIT = V7X_VMEM_BYTES * 7 // 8

COND_ROWS = 16
ADA_TN = 1536
ROW_TILE = 512
CONV_ROWS = GRID_W
CONV_PAD = 16
NORM_ROWS = 256
MOE_TM = 512
DISPATCH_TM = 512
ROUTE_LANES = 128


def _rmsnorm(x, g):
    return x * lax.rsqrt(jnp.mean(x * x, axis=-1, keepdims=True) + EPS) * g


def _group_mean(x, m_ref):
    hi = x.astype(BF16)
    lo = (x - hi.astype(F32)).astype(BF16)
    m = m_ref[...]
    return (jnp.dot(hi, m, preferred_element_type=F32) + jnp.dot(lo, m, preferred_element_type=F32))


def _layernorm(x, g, b, m_ref):
    xc = x - _group_mean(x, m_ref)
    var = _group_mean(xc * xc, m_ref)
    return xc * lax.rsqrt(var + EPS) * g + b


def _ada_kernel(c_ref, w_ref, b_ref, o_ref):
    c = c_ref[...]
    cond = (c * jax.nn.sigmoid(c)).astype(BF16)
    o_ref[...] = jnp.dot(cond, w_ref[...].astype(BF16), preferred_element_type=F32) + b_ref[...]


def _ada_mod(cond_rows, w_ada, b_ada):
    n_col = w_ada.shape[-1]
    return pl.pallas_call(
        _ada_kernel,
        grid=(DEPTH, n_col // ADA_TN),
        in_specs=[
            pl.BlockSpec((COND_ROWS, D_MODEL), lambda l, j: (0, 0)),
            pl.BlockSpec((None, D_MODEL, ADA_TN), lambda l, j: (l, 0, j)),
            pl.BlockSpec((None, 1, ADA_TN), lambda l, j: (l, 0, j)),
        ],
        out_specs=pl.BlockSpec((None, COND_ROWS, ADA_TN), lambda l, j: (l, 0, j)),
        out_shape=jax.ShapeDtypeStruct((DEPTH, COND_ROWS, n_col), F32),
        name="ada_mod",
    )(cond_rows, w_ada, b_ada.reshape(DEPTH, 1, n_col))


def _token_rows(t, count=1):
    return pl.ds(pl.multiple_of(t * TOKEN_ROWS, TOKEN_ROWS), count * TOKEN_ROWS)


def _store_token_tiles(ref, x):
    n = x.shape[0]
    for c in range(TOKEN_ROWS):
        ref[pl.ds(c, n, stride=TOKEN_ROWS), :] = x[:, c * V7X_LANES:(c + 1) * V7X_LANES]


def _load_token_tiles(ref, n):
    return [ref[pl.ds(c, n, stride=TOKEN_ROWS), :] for c in range(TOKEN_ROWS)]


def _combine_experts(dest_cur, dest_nxt, pr_ref, ybuf, rows, sems):
    i = pl.program_id(0)
    n_steps = pl.num_programs(0)
    tm = rows.shape[2] // TOKEN_ROWS
    slot = i % 2

    def row_copy(dref, j, k, slot_):
        return pltpu.make_async_copy(ybuf.at[_token_rows(dref[0, j * TOP_K + k])],
                                     rows.at[slot_, k, _token_rows(j)], sems.at[slot_])

    def issue(dref, slot_):
        def body(j, carry):
            for k in range(TOP_K):
                row_copy(dref, j, k, slot_).start()
            return carry

        lax.fori_loop(0, tm, body, 0, unroll=4)

    @pl.when(i == 0)
    def _():
        issue(dest_cur, 0)

    @pl.when(i + 1 < n_steps)
    def _():
        issue(dest_nxt, 1 - slot)

    for k in range(TOP_K):
        pltpu.make_async_copy(ybuf.at[_token_rows(0, tm)], rows.at[slot, k], sems.at[slot]).wait()
    experts = [_load_token_tiles(rows.at[slot, k], tm) for k in range(TOP_K)]
    chunks = []
    for c in range(TOKEN_ROWS):
        acc = pr_ref[:, 0:1] * experts[0][c]
        for k in range(1, TOP_K):
            acc = acc + pr_ref[:, k:k + 1] * experts[k][c]
        chunks.append(acc)
    return jnp.concatenate(chunks, axis=1)


def _inproj_kernel(*refs, has_res):
    if has_res:
        (dest_cur, dest_nxt, x_ref, pr_ref, ybuf, modp_ref, mod_ref, g_ref, w_ref,
         xo_ref, p_ref, rows, sems) = refs
        x = x_ref[...] + modp_ref[5:6, :] * _combine_experts(dest_cur, dest_nxt, pr_ref, ybuf, rows, sems)
        xo_ref[...] = x
    else:
        x_ref, mod_ref, g_ref, w_ref, p_ref = refs
        x = x_ref[...]
    hn = _rmsnorm(x, g_ref[...]) * (1.0 + mod_ref[1:2, :]) + mod_ref[0:1, :]
    p_ref[...] = jnp.dot(hn.astype(BF16), w_ref[...], preferred_element_type=F32)


def _mod_spec(mod, seq, tm):
    if mod.shape[0] == 1:
        return pl.BlockSpec((None, 6, D_MODEL), lambda i: (0, 0, 0))
    return pl.BlockSpec((None, 6, D_MODEL), lambda i: ((i * tm) // seq, 0, 0))


def _combine_operands(dest, probs, y_buf, n, tm):
    n_steps = n // tm
    dest3 = dest.reshape(n_steps, 1, tm * TOP_K)
    smem = functools.partial(pl.BlockSpec, (None, 1, tm * TOP_K), memory_space=pltpu.SMEM)
    args = [dest3, dest3, probs, y_buf]
    specs = [smem(lambda i: (i, 0, 0)),
             smem(lambda i: (jnp.minimum(i + 1, n_steps - 1), 0, 0)),
             pl.BlockSpec((tm, ROUTE_LANES), lambda i: (i, 0)),
             pl.BlockSpec(memory_space=pl.ANY)]
    scratch = [pltpu.VMEM((2, TOP_K, tm * TOKEN_ROWS, V7X_LANES), F32), pltpu.SemaphoreType.DMA((2,))]
    return args, specs, scratch


def _in_proj(x, route, mod_prev, mod, g, w_bf16, seq):
    n = x.shape[0]
    tm = min(ROW_TILE, seq)
    row = pl.BlockSpec((tm, D_MODEL), lambda i: (i, 0))
    has_res = route is not None
    args, specs, scratch = [], [], []
    if has_res:
        args, specs, scratch = _combine_operands(*route, n, tm)
        args = args[:2] + [x] + args[2:] + [mod_prev]
        specs = specs[:2] + [row] + specs[2:] + [_mod_spec(mod_prev, seq, tm)]
    else:
        args, specs = [x], [row]
    args += [mod, g.reshape(1, D_MODEL), w_bf16]
    specs += [_mod_spec(mod, seq, tm),
              pl.BlockSpec((1, D_MODEL), lambda i: (0, 0)),
              pl.BlockSpec((D_MODEL, N_IN), lambda i: (0, 0))]
    proj_shape = jax.ShapeDtypeStruct((n, N_IN), F32)
    proj_spec = pl.BlockSpec((tm, N_IN), lambda i: (i, 0))
    if has_res:
        out_shape = (jax.ShapeDtypeStruct((n, D_MODEL), F32), proj_shape)
        out_specs = (row, proj_spec)
    else:
        out_shape, out_specs = proj_shape, proj_spec
    out = pl.pallas_call(
        functools.partial(_inproj_kernel, has_res=has_res),
        grid=(n // tm,),
        in_specs=specs,
        out_specs=out_specs,
        out_shape=out_shape,
        scratch_shapes=scratch,
        compiler_params=pltpu.CompilerParams(dimension_semantics=("arbitrary",), vmem_limit_bytes=VMEM_LIMIT),
        name="in_proj",
    )(*args)
    return out if has_res else (x, out)


def _group_scan(a, b, reverse):
    shape = a.shape
    grouped = (shape[0] // V7X_SUBLANES, V7X_SUBLANES, shape[1])
    a, b = a.reshape(grouped), b.reshape(grouped)
    ri = lax.broadcasted_iota(jnp.int32, grouped, 1)
    for d in (1, 2, 4):
        shift = V7X_SUBLANES - d if reverse else d
        keep = ri < V7X_SUBLANES - d if reverse else ri >= d
        ra, rb = pltpu.roll(a, shift, 1), pltpu.roll(b, shift, 1)
        b = a * jnp.where(keep, rb, 0.0) + b
        a = a * jnp.where(keep, ra, 1.0)
    return a.reshape(shape), b.reshape(shape)


def _mixer_kernel(s0, s1, s2, caw, cab, nag, nab, cbw, cbb, wg, bg, lam, h0, sng, snb, sw, sbias, cdw,
                  m_head, m_all, y_ref, st_ref, pad_s, af_s, bf_s, ab_s, bb_s, *, seq, on_grid):
    mixer = pl.program_id(1)
    n_conv = seq // CONV_ROWS
    win_rows = CONV_ROWS + 2 * CONV_PAD

    def rows_at(c, size):
        return pl.ds(pl.multiple_of(c * size, size), size)

    def fill_padded(fn):
        zeros = jnp.zeros((CONV_PAD, W_GRP), F32)
        pad_s[0:CONV_PAD, :] = zeros
        pad_s[CONV_PAD + seq:2 * CONV_PAD + seq, :] = zeros

        def body(c, carry):
            dst = pl.ds(pl.multiple_of(c * CONV_ROWS + CONV_PAD, V7X_SUBLANES), CONV_ROWS)
            pad_s[dst, :] = fn(rows_at(c, CONV_ROWS))
            return carry

        lax.fori_loop(0, n_conv, body, 0)

    def conv_window(win, w_ref, taps, pad_l):
        acc = jnp.zeros((CONV_ROWS, W_GRP), F32)
        for mis in range(V7X_SUBLANES):
            starts = [(k, CONV_PAD - pad_l + k) for k in range(taps)
                      if (CONV_PAD - pad_l + k) % V7X_SUBLANES == mis]
            if not starts:
                continue
            shifted = pltpu.roll(win, win_rows - mis, 0) if mis else win
            for k, start in starts:
                acc = acc + w_ref[k:k + 1, :] * shifted[start - mis:start - mis + CONV_ROWS, :]
        return acc

    def conv_rows(c, w_ref, taps, pad_l):
        win = pad_s[pl.ds(pl.multiple_of(c * CONV_ROWS, CONV_ROWS), win_rows), :]
        return conv_window(win, w_ref, taps, pad_l)

    @pl.when(mixer == 0)
    def _conformer():
        def glu(rows):
            return s0[rows, :] * jax.nn.sigmoid(s1[rows, :])

        if not on_grid:
            fill_padded(glu)

        def body(c, carry):
            if on_grid:
                edge = jnp.zeros((CONV_PAD, W_GRP), F32)
                win = jnp.concatenate([edge, glu(rows_at(c, CONV_ROWS)), edge], axis=0)
                z = conv_window(win, caw, CONV_A, CONV_A // 2)
            else:
                z = conv_rows(c, caw, CONV_A, CONV_A // 2)
            af_s[rows_at(c, CONV_ROWS), :] = z + cab[...]
            return carry

        lax.fori_loop(0, n_conv, body, 0)

        def norm(c, carry):
            rows = rows_at(c, NORM_ROWS)
            z = _layernorm(af_s[rows, :], nag[...], nab[...], m_head)
            y_ref[rows, :] = (z * jax.nn.sigmoid(z)).astype(BF16)
            return carry

        lax.fori_loop(0, seq // NORM_ROWS, norm, 0, unroll=2)

    @pl.when(mixer == 1)
    def _rglru():
        fill_padded(lambda rows: s0[rows, :])
        lam_v = lam[...]
        softplus_neg = jnp.maximum(-lam_v, 0.0) + jnp.log1p(jnp.exp(-jnp.abs(lam_v)))

        def gates(c, carry):
            rows = rows_at(c, CONV_ROWS)
            xc = conv_rows(c, cbw, CONV_B, 2) + cbb[...]
            gt = jax.nn.sigmoid(jnp.dot(xc.astype(BF16), wg[...], preferred_element_type=F32) + bg[...])
            for d, (a_s, b_s) in enumerate(((af_s, bf_s), (ab_s, bb_s))):
                r_gate = gt[:, (2 * d) * W_GRP:(2 * d + 1) * W_GRP]
                i_gate = gt[:, (2 * d + 1) * W_GRP:(2 * d + 2) * W_GRP]
                log_a = -LRU_C * r_gate * softplus_neg[d:d + 1, :]
                a = jnp.exp(log_a)
                b = jnp.sqrt(jnp.maximum(-jnp.tanh(log_a) * (a * a + 1.0), 0.0)) * (i_gate * xc)
                a, b = _group_scan(a, b, reverse=(d == 1))
                a_s[rows, :] = a
                b_s[rows, :] = b
            return carry

        lax.fori_loop(0, n_conv, gates, 0)

        n_grp = seq // V7X_SUBLANES

        def chain(g, carry):
            cf, cb = carry
            rf = rows_at(g, V7X_SUBLANES)
            rb = rows_at(n_grp - 1 - g, V7X_SUBLANES)
            hf = af_s[rf, :] * cf + bf_s[rf, :]
            bf_s[rf, :] = hf
            hb = ab_s[rb, :] * cb + bb_s[rb, :]
            bb_s[rb, :] = hb
            cf = jnp.broadcast_to(hf[V7X_SUBLANES - 1:V7X_SUBLANES, :], (V7X_SUBLANES, W_GRP))
            cb = jnp.broadcast_to(hb[0:1, :], (V7X_SUBLANES, W_GRP))
            return cf, cb

        init = (jnp.broadcast_to(h0[0:1, :], (V7X_SUBLANES, W_GRP)),
                jnp.broadcast_to(h0[1:2, :], (V7X_SUBLANES, W_GRP)))
        cf, cb = lax.fori_loop(0, n_grp, chain, init, unroll=4)
        st_ref[0:1, :] = cf[0:1, :]
        st_ref[1:2, :] = cb[0:1, :]

        def out(c, carry):
            rows = rows_at(c, CONV_ROWS)
            y_ref[rows, :] = (jax.nn.gelu(s1[rows, :]) * (bf_s[rows, :] + bb_s[rows, :])).astype(BF16)
            return carry

        lax.fori_loop(0, n_conv, out, 0)

    @pl.when(mixer == 2)
    def _sgu():
        lane = lax.broadcasted_iota(jnp.int32, (CHUNK, W_GRP), 1)

        def body(n, carry):
            rows = rows_at(n, CHUNK)
            v = _layernorm(s1[rows, :], sng[...], snb[...], m_all).astype(BF16)
            s = sbias[...]
            for h in range(HEADS):
                sh = jnp.dot(sw[h], v, preferred_element_type=F32)
                s = s + jnp.where((lane >= h * HD) & (lane < (h + 1) * HD), sh, 0.0)
            y_ref[rows, :] = (s0[rows, :] * s).astype(BF16)
            return carry

        lax.fori_loop(0, seq // CHUNK, body, 0, unroll=4)

    @pl.when(mixer == 3)
    def _gated_conv():
        fill_padded(lambda rows: s1[rows, :] * s2[rows, :])

        def body(c, carry):
            rows = rows_at(c, CONV_ROWS)
            y_ref[rows, :] = (s0[rows, :] * conv_rows(c, cdw, CONV_D, CONV_D // 2)).astype(BF16)
            return carry

        lax.fori_loop(0, n_conv, body, 0)


def _mixers(proj, h0, lw, batch, seq, on_grid):
    proj3 = proj.reshape(batch, seq, N_IN)
    slab = (None, seq, W_GRP)

    def const(shape):
        return pl.BlockSpec(shape, lambda b, m: (0,) * len(shape))

    in_specs = [
        pl.BlockSpec(slab, lambda b, m: (b, 0, 2 * m)),
        pl.BlockSpec(slab, lambda b, m: (b, 0, 2 * m + 1)),
        pl.BlockSpec(slab, lambda b, m: (b, 0, 8)),
        const((CONV_A, W_GRP)), const((1, W_GRP)), const((1, W_GRP)), const((1, W_GRP)),
        const((CONV_B, W_GRP)), const((1, W_GRP)),
        const((W_GRP, 4 * W_GRP)), const((1, 4 * W_GRP)), const((2, W_GRP)),
        pl.BlockSpec((None, 2, W_GRP), lambda b, m: (b, 0, 0)),
        const((1, W_GRP)), const((1, W_GRP)), const((HEADS, CHUNK, CHUNK)), const((CHUNK, W_GRP)),
        const((CONV_D, W_GRP)), const((W_GRP, W_GRP)), const((W_GRP, W_GRP)),
    ]
    y, st = pl.pallas_call(
        functools.partial(_mixer_kernel, seq=seq, on_grid=on_grid),
        grid=(batch, N_MIXERS),
        in_specs=in_specs,
        out_specs=(pl.BlockSpec(slab, lambda b, m: (b, 0, m)),
                   pl.BlockSpec((None, 2, W_GRP), lambda b, m: (b, 0, 0))),
        out_shape=(jax.ShapeDtypeStruct((batch, seq, D_MODEL), BF16),
                   jax.ShapeDtypeStruct((batch, 2, W_GRP), F32)),
        scratch_shapes=[pltpu.VMEM((seq + 2 * CONV_PAD, W_GRP), F32)] + [pltpu.VMEM((seq, W_GRP), F32)] * 4,
        compiler_params=pltpu.CompilerParams(dimension_semantics=("arbitrary", "arbitrary"),
                                             vmem_limit_bytes=VMEM_LIMIT),
        name="mixers",
    )(proj3, proj3, proj3, lw["conv_a_w"], lw["conv_a_b"], lw["norm_a_g"], lw["norm_a_b"],
      lw["conv_b_w"], lw["conv_b_b"], lw["gate_w"], lw["gate_b"], lw["lru_lam"], h0,
      lw["sgu_norm_g"], lw["sgu_norm_b"], lw["sgu_w"], lw["sgu_bias"], lw["conv_d_w"],
      lw["m_head"], lw["m_all"])
    return y.reshape(batch * seq, D_MODEL), st


def _outproj_kernel(y_ref, x_ref, mod_ref, g_ref, wo_ref, wr_ref, br_ref, tri_ref, cin_ref,
                    xm_ref, hn_ref, ei_ref, pr_ref, cnt_ref, seen_s):
    tm = x_ref.shape[0]

    @pl.when(pl.program_id(0) == 0)
    def _():
        seen_s[...] = cin_ref[...]

    y = jnp.dot(y_ref[...], wo_ref[...], preferred_element_type=F32)
    x = x_ref[...] + mod_ref[2:3, :] * y
    xm_ref[...] = x
    hn = _rmsnorm(x, g_ref[...]) * (1.0 + mod_ref[4:5, :]) + mod_ref[3:4, :]
    _store_token_tiles(hn_ref, hn)

    h1 = hn.astype(BF16)
    r1 = hn - h1.astype(F32)
    h2 = r1.astype(BF16)
    h3 = (r1 - h2.astype(F32)).astype(BF16)
    w_cat = wr_ref[...]
    p1 = jnp.dot(h1, w_cat, preferred_element_type=F32)
    p2 = jnp.dot(h2, w_cat, preferred_element_type=F32)
    p3 = jnp.dot(h3, w_cat, preferred_element_type=F32)
    e = N_EXPERTS
    logits = (p1[:, 2 * e:3 * e] + p2[:, e:2 * e] + p3[:, 0:e] + p1[:, e:2 * e] + p2[:, 0:e]
              + p1[:, 0:e]) + br_ref[...]

    lane = lax.broadcasted_iota(jnp.int32, (tm, N_EXPERTS), 1).astype(F32)
    work = logits
    vals, ids, sels = [], [], []
    for _ in range(TOP_K):
        mx = jnp.max(work, axis=-1, keepdims=True)
        idx = jnp.min(jnp.where(work == mx, lane, float(N_EXPERTS)), axis=-1, keepdims=True)
        sel = lane == idx
        work = jnp.where(sel, -jnp.inf, work)
        vals.append(mx)
        ids.append(idx)
        sels.append(sel)
    exps = [jnp.exp(v - vals[0]) for v in vals]
    den = exps[0] + exps[1] + exps[2] + exps[3]

    chosen = jnp.where(sels[0] | sels[1] | sels[2] | sels[3], 1.0, 0.0)
    earlier = jnp.dot(tri_ref[...], chosen.astype(BF16), preferred_element_type=F32) + seen_s[...]
    seen_s[...] = earlier[tm - 1:tm, :] + chosen[tm - 1:tm, :]
    cnt_ref[...] = seen_s[...]

    out_lane = lax.broadcasted_iota(jnp.int32, (tm, ROUTE_LANES), 1)
    ei = jnp.zeros((tm, ROUTE_LANES), F32)
    pr = jnp.zeros((tm, ROUTE_LANES), F32)
    for k in range(TOP_K):
        rank = jnp.sum(jnp.where(sels[k], earlier, 0.0), axis=-1, keepdims=True)
        ei = jnp.where(out_lane == k, ids[k], ei)
        ei = jnp.where(out_lane == TOP_K + k, rank, ei)
        pr = jnp.where(out_lane == k, exps[k] / den, pr)
    ei_ref[...] = ei.astype(jnp.int32)
    pr_ref[...] = pr


def _out_proj(y, x, mod, g, wo_bf16, w_router, b_router, seen, seq):
    n = x.shape[0]
    tm = min(ROW_TILE, seq)
    row = pl.BlockSpec((tm, D_MODEL), lambda i: (i, 0))
    route = pl.BlockSpec((tm, ROUTE_LANES), lambda i: (i, 0))
    cnt = pl.BlockSpec((1, N_EXPERTS), lambda i: (0, 0))
    tri = jnp.tri(tm, k=-1, dtype=BF16)
    w1 = w_router.astype(BF16)
    w2 = (w_router - w1.astype(F32)).astype(BF16)
    w3 = (w_router - w1.astype(F32) - w2.astype(F32)).astype(BF16)
    w_cat = jnp.concatenate([w1, w2, w3, jnp.zeros_like(w1)], axis=1)
    return pl.pallas_call(
        _outproj_kernel,
        grid=(n // tm,),
        in_specs=[row, row, _mod_spec(mod, seq, tm),
                  pl.BlockSpec((1, D_MODEL), lambda i: (0, 0)),
                  pl.BlockSpec((D_MODEL, D_MODEL), lambda i: (0, 0)),
                  pl.BlockSpec((D_MODEL, 4 * N_EXPERTS), lambda i: (0, 0)),
                  cnt, pl.BlockSpec((tm, tm), lambda i: (0, 0)), cnt],
        out_specs=(row, pl.BlockSpec((tm * TOKEN_ROWS, V7X_LANES), lambda i: (i, 0)), route, route, cnt),
        out_shape=(jax.ShapeDtypeStruct((n, D_MODEL), F32),
                   jax.ShapeDtypeStruct((n * TOKEN_ROWS, V7X_LANES), F32),
                   jax.ShapeDtypeStruct((n, ROUTE_LANES), jnp.int32),
                   jax.ShapeDtypeStruct((n, ROUTE_LANES), F32),
                   jax.ShapeDtypeStruct((1, N_EXPERTS), F32)),
        scratch_shapes=[pltpu.VMEM((1, N_EXPERTS), F32)],
        compiler_params=pltpu.CompilerParams(dimension_semantics=("arbitrary",)),
        name="out_proj",
    )(y, x, mod, g.reshape(1, D_MODEL), wo_bf16, w_cat, b_router.reshape(1, N_EXPERTS), tri, seen)


def _dispatch_kernel(zero_blk, zero_on, dest_ref, hn_p, hn_s, xbuf, zeros_s, sem, zsem, *, tiles_p):
    i = pl.program_id(0)

    @pl.when(i == 0)
    def _():
        zeros_s[...] = jnp.zeros_like(zeros_s)

        def fill(j):
            return pltpu.make_async_copy(zeros_s, xbuf.at[_token_rows(zero_blk[j] * MOE_TM, MOE_TM)], zsem)

        for j in range(zero_blk.shape[0]):
            @pl.when(zero_on[j] == 1)
            def _():
                fill(j).start()

        for j in range(zero_blk.shape[0]):
            @pl.when(zero_on[j] == 1)
            def _():
                fill(j).wait()

    def issue(src):
        def body(j, carry):
            row = src.at[_token_rows(j)]
            for k in range(TOP_K):
                pltpu.make_async_copy(row, xbuf.at[_token_rows(dest_ref[0, j * TOP_K + k])], sem).start()
            return carry

        lax.fori_loop(0, DISPATCH_TM, body, 0, unroll=4)

    @pl.when(i < tiles_p)
    def _():
        issue(hn_p)

    @pl.when(i >= tiles_p)
    def _():
        issue(hn_s)

    for _ in range(TOP_K):
        pltpu.make_async_copy(hn_s, xbuf.at[_token_rows(0, DISPATCH_TM)], sem).wait()


def _dispatch(dest, zero_blk, zero_on, hn_p, hn_s, n_pad):
    n_p, n_s = hn_p.shape[0] // TOKEN_ROWS, hn_s.shape[0] // TOKEN_ROWS
    tiles_p = n_p // DISPATCH_TM
    n_steps = (n_p + n_s) // DISPATCH_TM
    tile = (DISPATCH_TM * TOKEN_ROWS, V7X_LANES)
    grid_spec = pltpu.PrefetchScalarGridSpec(
        num_scalar_prefetch=2,
        grid=(n_steps,),
        in_specs=[pl.BlockSpec((None, 1, DISPATCH_TM * TOP_K), lambda i, zb, zo: (i, 0, 0),
                               memory_space=pltpu.SMEM),
                  pl.BlockSpec(tile, lambda i, zb, zo: (jnp.minimum(i, tiles_p - 1), 0)),
                  pl.BlockSpec(tile, lambda i, zb, zo: (jnp.maximum(i - tiles_p, 0), 0))],
        out_specs=pl.BlockSpec(memory_space=pl.ANY),
        scratch_shapes=[pltpu.VMEM((MOE_TM * TOKEN_ROWS, V7X_LANES), F32), pltpu.SemaphoreType.DMA(()),
                        pltpu.SemaphoreType.DMA(())],
    )
    return pl.pallas_call(
        functools.partial(_dispatch_kernel, tiles_p=tiles_p),
        grid_spec=grid_spec,
        out_shape=jax.ShapeDtypeStruct((n_pad * TOKEN_ROWS, V7X_LANES), F32),
        compiler_params=pltpu.CompilerParams(dimension_semantics=("arbitrary",)),
        name="moe_dispatch",
    )(zero_blk, zero_on, dest.reshape(n_steps, 1, DISPATCH_TM * TOP_K), hn_p, hn_s)


def _moe_kernel(blk_e, blk_new, blk_on, x_ref, wgu_ref, bgu_ref, wdn_ref, bdn_ref, o_ref, wgu_s, wdn_s):
    i = pl.program_id(0)

    @pl.when(blk_on[i] == 0)
    def _():
        o_ref[...] = jnp.zeros_like(o_ref)

    @pl.when(blk_on[i] == 1)
    def _():
        @pl.when(blk_new[i] == 1)
        def _():
            wgu_s[...] = wgu_ref[...].astype(BF16)
            wdn_s[...] = wdn_ref[...].astype(BF16)

        x = jnp.concatenate([c.astype(BF16) for c in _load_token_tiles(x_ref, MOE_TM)], axis=1)
        gu = jnp.dot(x, wgu_s[...], preferred_element_type=F32) + bgu_ref[...]
        g = jnp.minimum(gu[:, :D_FF], SWIGLU_LIMIT)
        u = jnp.clip(gu[:, D_FF:], -SWIGLU_LIMIT, SWIGLU_LIMIT)
        act = (u + 1.0) * (g * jax.nn.sigmoid(SWIGLU_ALPHA * g))
        _store_token_tiles(o_ref, jnp.dot(act.astype(BF16), wdn_s[...], preferred_element_type=F32)
                           + bdn_ref[...])


def _moe_blocks(x_buf, blk_e, blk_new, blk_on, layer, w_gu, b_gu, w_dn, b_dn):
    n_pad = x_buf.shape[0] // TOKEN_ROWS
    n_blk = n_pad // MOE_TM
    tile = pl.BlockSpec((MOE_TM * TOKEN_ROWS, V7X_LANES), lambda i, e, nw, on: (i, 0))
    grid_spec = pltpu.PrefetchScalarGridSpec(
        num_scalar_prefetch=3,
        grid=(n_blk,),
        in_specs=[
            tile,
            pl.BlockSpec((None, None, D_MODEL, 2 * D_FF), lambda i, e, nw, on: (layer, e[i], 0, 0)),
            pl.BlockSpec((None, 1, 2 * D_FF), lambda i, e, nw, on: (e[i], 0, 0)),
            pl.BlockSpec((None, None, D_FF, D_MODEL), lambda i, e, nw, on: (layer, e[i], 0, 0)),
            pl.BlockSpec((None, 1, D_MODEL), lambda i, e, nw, on: (e[i], 0, 0)),
        ],
        out_specs=tile,
        scratch_shapes=[pltpu.VMEM((D_MODEL, 2 * D_FF), BF16), pltpu.VMEM((D_FF, D_MODEL), BF16)],
    )
    return pl.pallas_call(
        _moe_kernel,
        grid_spec=grid_spec,
        out_shape=jax.ShapeDtypeStruct((n_pad * TOKEN_ROWS, V7X_LANES), F32),
        compiler_params=pltpu.CompilerParams(dimension_semantics=("arbitrary",),
                                             vmem_limit_bytes=VMEM_LIMIT),
        name="moe_experts",
    )(blk_e, blk_new, blk_on, x_buf, w_gu, b_gu.reshape(N_EXPERTS, 1, 2 * D_FF),
      w_dn, b_dn.reshape(N_EXPERTS, 1, D_MODEL))


def _moe(hn_p, hn_s, ei_p, ei_s, counts, layer, w_gu, b_gu, w_dn, b_dn):
    n_asg = (ei_p.shape[0] + ei_s.shape[0]) * TOP_K
    n_pad = (n_asg + N_EXPERTS * (MOE_TM - 1) + MOE_TM - 1) // MOE_TM * MOE_TM
    n_blk = n_pad // MOE_TM
    counts = counts.reshape(N_EXPERTS).astype(jnp.int32)
    padded = (counts + MOE_TM - 1) // MOE_TM * MOE_TM
    pad_end = jnp.cumsum(padded)
    pad_start = pad_end - padded
    blk_row = jnp.arange(n_blk, dtype=jnp.int32) * MOE_TM
    blk_e = jnp.minimum(jnp.sum((blk_row[:, None] >= pad_end[None, :]).astype(jnp.int32), axis=1),
                        N_EXPERTS - 1)
    blk_on = (blk_row < pad_end[-1]).astype(jnp.int32)
    blk_new = jnp.concatenate([jnp.ones((1,), jnp.int32), (blk_e[1:] != blk_e[:-1]).astype(jnp.int32)])

    def slots(ei):
        expert, rank = ei[:, 0:TOP_K], ei[:, TOP_K:2 * TOP_K]
        start = jnp.sum(jnp.where(expert[:, :, None] == jnp.arange(N_EXPERTS, dtype=jnp.int32),
                                  pad_start, 0), axis=-1)
        return (start + rank).reshape(-1)

    tail = n_blk - n_asg // MOE_TM
    last_on = (counts % MOE_TM != 0).astype(jnp.int32)
    tail_blk = pad_end[-1] // MOE_TM + jnp.arange(tail, dtype=jnp.int32)
    tail_on = (tail_blk < n_blk).astype(jnp.int32)
    zero_blk = jnp.concatenate([(pad_end // MOE_TM - 1) * last_on, tail_blk * tail_on])
    zero_on = jnp.concatenate([last_on, tail_on])

    dest_p, dest_s = slots(ei_p), slots(ei_s)
    x_buf = _dispatch(jnp.concatenate([dest_p, dest_s]), zero_blk, zero_on, hn_p, hn_s, n_pad)
    y_buf = _moe_blocks(x_buf, blk_e, blk_new, blk_on, layer, w_gu, b_gu, w_dn, b_dn)
    return dest_p, dest_s, y_buf


def _final_kernel(dest_cur, dest_nxt, x_ref, pr_ref, ybuf, mod_ref, g_ref, o_ref, rows, sems):
    x = x_ref[...] + mod_ref[5:6, :] * _combine_experts(dest_cur, dest_nxt, pr_ref, ybuf, rows, sems)
    o_ref[...] = _rmsnorm(x, g_ref[...])


def _final_norm(x, route, mod, g, seq):
    n = x.shape[0]
    tm = min(ROW_TILE, seq)
    row = pl.BlockSpec((tm, D_MODEL), lambda i: (i, 0))
    args, specs, scratch = _combine_operands(*route, n, tm)
    return pl.pallas_call(
        _final_kernel,
        grid=(n // tm,),
        in_specs=specs[:2] + [row] + specs[2:] + [_mod_spec(mod, seq, tm),
                                                  pl.BlockSpec((1, D_MODEL), lambda i: (0, 0))],
        out_specs=row,
        out_shape=jax.ShapeDtypeStruct((n, D_MODEL), F32),
        scratch_shapes=scratch,
        compiler_params=pltpu.CompilerParams(dimension_semantics=("arbitrary",), vmem_limit_bytes=VMEM_LIMIT),
        name="final_norm",
    )(*args[:2], x, *args[2:], mod, g.reshape(1, D_MODEL))


def _block_diag(w):
    eye = jnp.eye(HEADS, dtype=w.dtype)
    return (eye[:, None, :, None] * w[:, :, None, :]).reshape(W_GRP, W_GRP)


def _layer_weights(l, p):
    gate_w = jnp.concatenate([_block_diag(p["lru_wr"][l, 0]), _block_diag(p["lru_wi"][l, 0]),
                              _block_diag(p["lru_wr"][l, 1]), _block_diag(p["lru_wi"][l, 1])], axis=1)
    gate_b = jnp.concatenate([p["lru_br"][l, 0], p["lru_bi"][l, 0], p["lru_br"][l, 1], p["lru_bi"][l, 1]])
    head_of = jnp.arange(W_GRP) // HD
    row = lambda v: v.reshape(1, W_GRP)
    return dict(
        conv_a_w=p["conv_a_w"][l], conv_a_b=row(p["conv_a_b"][l]),
        norm_a_g=row(p["norm_a_g"][l]), norm_a_b=row(p["norm_a_b"][l]),
        conv_b_w=p["conv_b_w"][l], conv_b_b=row(p["conv_b_b"][l]),
        gate_w=gate_w.astype(BF16), gate_b=gate_b.reshape(1, 4 * W_GRP), lru_lam=p["lru_lam"][l],
        sgu_norm_g=row(p["sgu_norm_g"][l]), sgu_norm_b=row(p["sgu_norm_b"][l]),
        sgu_w=p["sgu_w"][l].astype(BF16), sgu_bias=jnp.repeat(p["sgu_b"][l].T, HD, axis=1),
        conv_d_w=p["conv_d_w"][l],
        m_head=((head_of[:, None] == head_of[None, :]).astype(F32) / HD).astype(BF16),
        m_all=jnp.full((W_GRP, W_GRP), 1.0 / W_GRP, BF16),
    )


def kernel(x_prompt, x_sample, state_rglru, c, c_ctx, w_ada, b_ada, norm1_g, norm2_g, w_in, conv_a_w,
           conv_a_b, norm_a_g, norm_a_b, conv_b_w, conv_b_b, lru_wr, lru_br, lru_wi, lru_bi, lru_lam,
           sgu_norm_g, sgu_norm_b, sgu_w, sgu_b, conv_d_w, w_out, w_router, b_router, w_gu, b_gu, w_dn,
           b_dn, final_g):
    p = dict(conv_a_w=conv_a_w, conv_a_b=conv_a_b, norm_a_g=norm_a_g, norm_a_b=norm_a_b,
             conv_b_w=conv_b_w, conv_b_b=conv_b_b, lru_wr=lru_wr, lru_br=lru_br, lru_wi=lru_wi,
             lru_bi=lru_bi, lru_lam=lru_lam, sgu_norm_g=sgu_norm_g, sgu_norm_b=sgu_norm_b,
             sgu_w=sgu_w, sgu_b=sgu_b, conv_d_w=conv_d_w)
    bp, tp, _ = x_prompt.shape
    bs, ts, _ = x_sample.shape
    n_p, n_s = bp * tp, bs * ts

    cond_rows = jnp.zeros((COND_ROWS, D_MODEL), F32).at[0].set(c_ctx).at[1:1 + bs].set(c)
    mod = _ada_mod(cond_rows, w_ada, b_ada).reshape(DEPTH, COND_ROWS, 6, D_MODEL)

    xp = x_prompt.reshape(n_p, D_MODEL)
    xs = x_sample.reshape(n_s, D_MODEL)
    h0_ctx = jnp.zeros((bp, 2, W_GRP), F32)
    route_p = route_s = mod_p_prev = mod_s_prev = None
    no_tokens_seen = jnp.zeros((1, N_EXPERTS), F32)
    states = []
    for l in range(DEPTH):
        lw = _layer_weights(l, p)
        mod_p, mod_s = mod[l, 0:1], mod[l, 1:1 + bs]
        w_in_l = w_in[l].astype(BF16)
        w_out_l = w_out[l].astype(BF16)
        xp, proj_p = _in_proj(xp, route_p, mod_p_prev, mod_p, norm1_g[l], w_in_l, tp)
        xs, proj_s = _in_proj(xs, route_s, mod_s_prev, mod_s, norm1_g[l], w_in_l, ts)
        y_p, st = _mixers(proj_p, h0_ctx, lw, bp, tp, False)
        y_s, _ = _mixers(proj_s, state_rglru[:, l], lw, bs, ts, True)
        states.append(st)
        xp, hn_p, ei_p, pr_p, seen = _out_proj(y_p, xp, mod_p, norm2_g[l], w_out_l, w_router[l],
                                               b_router[l], no_tokens_seen, tp)
        xs, hn_s, ei_s, pr_s, counts = _out_proj(y_s, xs, mod_s, norm2_g[l], w_out_l, w_router[l],
                                                 b_router[l], seen, ts)
        dest_p, dest_s, y_buf = _moe(hn_p, hn_s, ei_p, ei_s, counts, l, w_gu, b_gu[l], w_dn, b_dn[l])
        route_p, route_s = (dest_p, pr_p, y_buf), (dest_s, pr_s, y_buf)
        mod_p_prev, mod_s_prev = mod_p, mod_s
    y_prompt = _final_norm(xp, route_p, mod_p_prev, final_g, tp).reshape(bp, tp, D_MODEL)
    y_sample = _final_norm(xs, route_s, mod_s_prev, final_g, ts).reshape(bs, ts, D_MODEL)
    return y_prompt, y_sample, jnp.stack(states, axis=1)
```

```python
import functools

import jax
import jax.numpy as jnp
from jax import lax
from jax.experimental import pallas as pl
from jax.experimental.pallas import tpu as pltpu

F32 = jnp.float32
BF16 = jnp.bfloat16

D_MODEL = 1024
DEPTH = 2
GRID_W = 64
N_MIXERS = 4
W_GRP = D_MODEL // N_MIXERS
HEADS = 4
HD = W_GRP // HEADS
N_IN = 9 * W_GRP
CONV_A = 31
CONV_B = 4
CONV_D = 3
CHUNK = 128
LRU_C = 8.0
N_EXPERTS = 32
TOP_K = 4
D_FF = D_MODEL
SWIGLU_LIMIT = 7.0
SWIGLU_ALPHA = 1.702
EPS = 1e-6

V7X_SUBLANES = 8
V7X_LANES = 128
TOKEN_ROWS = D_MODEL // V7X_LANES
DMA_PRIORITIES = 2
V7X_VMEM_BYTES = 64 * 1024 * 1024
VMEM_LIMIT = V7X_VMEM_BYTES * 7 // 8

COND_ROWS = 16
ADA_TN = 1536
ROW_TILE = 512
CONV_ROWS = GRID_W
CONV_PAD = 16
NORM_ROWS = 256
MOE_TM = 512
DISPATCH_TM = 512
ROUTE_LANES = 128


def _rmsnorm(x, g):
    return x * lax.rsqrt(jnp.mean(x * x, axis=-1, keepdims=True) + EPS) * g


def _group_mean(x, m_ref):
    hi = x.astype(BF16)
    lo = (x - hi.astype(F32)).astype(BF16)
    m = m_ref[...]
    return (jnp.dot(hi, m, preferred_element_type=F32) + jnp.dot(lo, m, preferred_element_type=F32))


def _layernorm(x, g, b, m_ref):
    xc = x - _group_mean(x, m_ref)
    var = _group_mean(xc * xc, m_ref)
    return xc * lax.rsqrt(var + EPS) * g + b


def _ada_kernel(c_ref, w_ref, b_ref, o_ref):
    c = c_ref[...]
    cond = (c * jax.nn.sigmoid(c)).astype(BF16)
    o_ref[...] = jnp.dot(cond, w_ref[...].astype(BF16), preferred_element_type=F32) + b_ref[...]


def _ada_mod(cond_rows, w_ada, b_ada):
    n_col = w_ada.shape[-1]
    return pl.pallas_call(
        _ada_kernel,
        grid=(DEPTH, n_col // ADA_TN),
        in_specs=[
            pl.BlockSpec((COND_ROWS, D_MODEL), lambda l, j: (0, 0)),
            pl.BlockSpec((None, D_MODEL, ADA_TN), lambda l, j: (l, 0, j)),
            pl.BlockSpec((None, 1, ADA_TN), lambda l, j: (l, 0, j)),
        ],
        out_specs=pl.BlockSpec((None, COND_ROWS, ADA_TN), lambda l, j: (l, 0, j)),
        out_shape=jax.ShapeDtypeStruct((DEPTH, COND_ROWS, n_col), F32),
        name="ada_mod",
    )(cond_rows, w_ada, b_ada.reshape(DEPTH, 1, n_col))


def _token_rows(t, count=1):
    return pl.ds(pl.multiple_of(t * TOKEN_ROWS, TOKEN_ROWS), count * TOKEN_ROWS)


def _store_token_tiles(ref, x):
    n = x.shape[0]
    for c in range(TOKEN_ROWS):
        ref[pl.ds(c, n, stride=TOKEN_ROWS), :] = x[:, c * V7X_LANES:(c + 1) * V7X_LANES]


def _load_token_tiles(ref, n):
    return [ref[pl.ds(c, n, stride=TOKEN_ROWS), :] for c in range(TOKEN_ROWS)]


def _combine_experts(dest_cur, dest_nxt, pr_ref, ybuf, rows, sems):
    i = pl.program_id(0)
    n_steps = pl.num_programs(0)
    tm = rows.shape[2] // TOKEN_ROWS
    slot = i % 2

    def row_copy(dref, j, k, slot_):
        return pltpu.make_async_copy(ybuf.at[_token_rows(dref[0, j * TOP_K + k])],
                                     rows.at[slot_, k, _token_rows(j)], sems.at[slot_])

    def issue(dref, slot_):
        def body(j, carry):
            for k in range(TOP_K):
                row_copy(dref, j, k, slot_).start(priority=k % DMA_PRIORITIES)
            return carry

        lax.fori_loop(0, tm, body, 0, unroll=4)

    @pl.when(i == 0)
    def _():
        issue(dest_cur, 0)

    @pl.when(i + 1 < n_steps)
    def _():
        issue(dest_nxt, 1 - slot)

    for k in range(TOP_K):
        pltpu.make_async_copy(ybuf.at[_token_rows(0, tm)], rows.at[slot, k], sems.at[slot]).wait()
    experts = [_load_token_tiles(rows.at[slot, k], tm) for k in range(TOP_K)]
    chunks = []
    for c in range(TOKEN_ROWS):
        acc = pr_ref[:, 0:1] * experts[0][c]
        for k in range(1, TOP_K):
            acc = acc + pr_ref[:, k:k + 1] * experts[k][c]
        chunks.append(acc)
    return jnp.concatenate(chunks, axis=1)


def _inproj_kernel(*refs, has_res):
    if has_res:
        (dest_cur, dest_nxt, x_ref, pr_ref, ybuf, modp_ref, mod_ref, g_ref, w_ref,
         xo_ref, p_ref, rows, sems) = refs
        x = x_ref[...] + modp_ref[5:6, :] * _combine_experts(dest_cur, dest_nxt, pr_ref, ybuf, rows, sems)
        xo_ref[...] = x
    else:
        x_ref, mod_ref, g_ref, w_ref, p_ref = refs
        x = x_ref[...]
    hn = _rmsnorm(x, g_ref[...]) * (1.0 + mod_ref[1:2, :]) + mod_ref[0:1, :]
    p_ref[...] = jnp.dot(hn.astype(BF16), w_ref[...], preferred_element_type=F32)


def _mod_spec(mod, seq, tm):
    if mod.shape[0] == 1:
        return pl.BlockSpec((None, 6, D_MODEL), lambda i: (0, 0, 0))
    return pl.BlockSpec((None, 6, D_MODEL), lambda i: ((i * tm) // seq, 0, 0))


def _combine_operands(dest, probs, y_buf, n, tm):
    n_steps = n // tm
    dest3 = dest.reshape(n_steps, 1, tm * TOP_K)
    smem = functools.partial(pl.BlockSpec, (None, 1, tm * TOP_K), memory_space=pltpu.SMEM)
    args = [dest3, dest3, probs, y_buf]
    specs = [smem(lambda i: (i, 0, 0)),
             smem(lambda i: (jnp.minimum(i + 1, n_steps - 1), 0, 0)),
             pl.BlockSpec((tm, ROUTE_LANES), lambda i: (i, 0)),
             pl.BlockSpec(memory_space=pl.ANY)]
    scratch = [pltpu.VMEM((2, TOP_K, tm * TOKEN_ROWS, V7X_LANES), F32), pltpu.SemaphoreType.DMA((2,))]
    return args, specs, scratch


def _in_proj(x, route, mod_prev, mod, g, w_bf16, seq):
    n = x.shape[0]
    tm = min(ROW_TILE, seq)
    row = pl.BlockSpec((tm, D_MODEL), lambda i: (i, 0))
    has_res = route is not None
    args, specs, scratch = [], [], []
    if has_res:
        args, specs, scratch = _combine_operands(*route, n, tm)
        args = args[:2] + [x] + args[2:] + [mod_prev]
        specs = specs[:2] + [row] + specs[2:] + [_mod_spec(mod_prev, seq, tm)]
    else:
        args, specs = [x], [row]
    args += [mod, g.reshape(1, D_MODEL), w_bf16]
    specs += [_mod_spec(mod, seq, tm),
              pl.BlockSpec((1, D_MODEL), lambda i: (0, 0)),
              pl.BlockSpec((D_MODEL, N_IN), lambda i: (0, 0))]
    proj_shape = jax.ShapeDtypeStruct((n, N_IN), F32)
    proj_spec = pl.BlockSpec((tm, N_IN), lambda i: (i, 0))
    if has_res:
        out_shape = (jax.ShapeDtypeStruct((n, D_MODEL), F32), proj_shape)
        out_specs = (row, proj_spec)
    else:
        out_shape, out_specs = proj_shape, proj_spec
    out = pl.pallas_call(
        functools.partial(_inproj_kernel, has_res=has_res),
        grid=(n // tm,),
        in_specs=specs,
        out_specs=out_specs,
        out_shape=out_shape,
        scratch_shapes=scratch,
        compiler_params=pltpu.CompilerParams(dimension_semantics=("arbitrary",), vmem_limit_bytes=VMEM_LIMIT),
        name="in_proj",
    )(*args)
    return out if has_res else (x, out)


def _group_scan(a, b, reverse):
    shape = a.shape
    grouped = (shape[0] // V7X_SUBLANES, V7X_SUBLANES, shape[1])
    a, b = a.reshape(grouped), b.reshape(grouped)
    ri = lax.broadcasted_iota(jnp.int32, grouped, 1)
    for d in (1, 2, 4):
        shift = V7X_SUBLANES - d if reverse else d
        keep = ri < V7X_SUBLANES - d if reverse else ri >= d
        ra, rb = pltpu.roll(a, shift, 1), pltpu.roll(b, shift, 1)
        b = a * jnp.where(keep, rb, 0.0) + b
        a = a * jnp.where(keep, ra, 1.0)
    return a.reshape(shape), b.reshape(shape)


def _mixer_kernel(s0, s1, s2, caw, cab, nag, nab, cbw, cbb, wg, bg, lam, h0, sng, snb, sw, sbias, cdw,
                  m_head, y_ref, st_ref, pad_s, af_s, bf_s, ab_s, bb_s, *, seq, on_grid):
    mixer = pl.program_id(1)
    n_conv = seq // CONV_ROWS
    win_rows = CONV_ROWS + 2 * CONV_PAD

    def rows_at(c, size):
        return pl.ds(pl.multiple_of(c * size, size), size)

    def fill_padded(fn):
        zeros = jnp.zeros((CONV_PAD, W_GRP), F32)
        pad_s[0:CONV_PAD, :] = zeros
        pad_s[CONV_PAD + seq:2 * CONV_PAD + seq, :] = zeros

        def body(c, carry):
            dst = pl.ds(pl.multiple_of(c * CONV_ROWS + CONV_PAD, V7X_SUBLANES), CONV_ROWS)
            pad_s[dst, :] = fn(rows_at(c, CONV_ROWS))
            return carry

        lax.fori_loop(0, n_conv, body, 0)

    def conv_window(win, w_ref, taps, pad_l):
        acc = jnp.zeros((CONV_ROWS, W_GRP), F32)
        for mis in range(V7X_SUBLANES):
            starts = [(k, CONV_PAD - pad_l + k) for k in range(taps)
                      if (CONV_PAD - pad_l + k) % V7X_SUBLANES == mis]
            if not starts:
                continue
            shifted = pltpu.roll(win, win_rows - mis, 0) if mis else win
            for k, start in starts:
                acc = acc + w_ref[k:k + 1, :] * shifted[start - mis:start - mis + CONV_ROWS, :]
        return acc

    def conv_rows(c, w_ref, taps, pad_l):
        win = pad_s[pl.ds(pl.multiple_of(c * CONV_ROWS, CONV_ROWS), win_rows), :]
        return conv_window(win, w_ref, taps, pad_l)

    @pl.when(mixer == 0)
    def _conformer():
        def glu(rows):
            return s0[rows, :] * jax.nn.sigmoid(s1[rows, :])

        if not on_grid:
            fill_padded(glu)

        def body(c, carry):
            if on_grid:
                edge = jnp.zeros((CONV_PAD, W_GRP), F32)
                win = jnp.concatenate([edge, glu(rows_at(c, CONV_ROWS)), edge], axis=0)
                z = conv_window(win, caw, CONV_A, CONV_A // 2)
            else:
                z = conv_rows(c, caw, CONV_A, CONV_A // 2)
            af_s[rows_at(c, CONV_ROWS), :] = z + cab[...]
            return carry

        lax.fori_loop(0, n_conv, body, 0)

        def norm(c, carry):
            rows = rows_at(c, NORM_ROWS)
            z = _layernorm(af_s[rows, :], nag[...], nab[...], m_head)
            y_ref[rows, :] = (z * jax.nn.sigmoid(z)).astype(BF16)
            return carry

        lax.fori_loop(0, seq // NORM_ROWS, norm, 0, unroll=2)

    @pl.when(mixer == 1)
    def _rglru():
        fill_padded(lambda rows: s0[rows, :])
        lam_v = lam[...]
        softplus_neg = jnp.maximum(-lam_v, 0.0) + jnp.log1p(jnp.exp(-jnp.abs(lam_v)))

        def gates(c, carry):
            rows = rows_at(c, CONV_ROWS)
            xc = conv_rows(c, cbw, CONV_B, 2) + cbb[...]
            gt = jax.nn.sigmoid(jnp.dot(xc.astype(BF16), wg[...], preferred_element_type=F32) + bg[...])
            for d, (a_s, b_s) in enumerate(((af_s, bf_s), (ab_s, bb_s))):
                r_gate = gt[:, (2 * d) * W_GRP:(2 * d + 1) * W_GRP]
                i_gate = gt[:, (2 * d + 1) * W_GRP:(2 * d + 2) * W_GRP]
                log_a = -LRU_C * r_gate * softplus_neg[d:d + 1, :]
                a = jnp.exp(log_a)
                b = jnp.sqrt(jnp.maximum(-jnp.tanh(log_a) * (a * a + 1.0), 0.0)) * (i_gate * xc)
                a, b = _group_scan(a, b, reverse=(d == 1))
                a_s[rows, :] = a
                b_s[rows, :] = b
            return carry

        lax.fori_loop(0, n_conv, gates, 0, unroll=2)

        n_grp = seq // V7X_SUBLANES

        def chain(g, carry):
            cf, cb = carry
            rf = rows_at(g, V7X_SUBLANES)
            rb = rows_at(n_grp - 1 - g, V7X_SUBLANES)
            hf = af_s[rf, :] * cf + bf_s[rf, :]
            bf_s[rf, :] = hf
            hb = ab_s[rb, :] * cb + bb_s[rb, :]
            bb_s[rb, :] = hb
            cf = jnp.broadcast_to(hf[V7X_SUBLANES - 1:V7X_SUBLANES, :], (V7X_SUBLANES, W_GRP))
            cb = jnp.broadcast_to(hb[0:1, :], (V7X_SUBLANES, W_GRP))
            return cf, cb

        init = (jnp.broadcast_to(h0[0:1, :], (V7X_SUBLANES, W_GRP)),
                jnp.broadcast_to(h0[1:2, :], (V7X_SUBLANES, W_GRP)))
        cf, cb = lax.fori_loop(0, n_grp, chain, init, unroll=4)
        st_ref[0:1, :] = cf[0:1, :]
        st_ref[1:2, :] = cb[0:1, :]

        def out(c, carry):
            rows = rows_at(c, CONV_ROWS)
            y_ref[rows, :] = (jax.nn.gelu(s1[rows, :]) * (bf_s[rows, :] + bb_s[rows, :])).astype(BF16)
            return carry

        lax.fori_loop(0, n_conv, out, 0)

    @pl.when(mixer == 2)
    def _sgu():
        lane = lax.broadcasted_iota(jnp.int32, (CHUNK, W_GRP), 1)

        def body(n, carry):
            rows = rows_at(n, CHUNK)
            v = s1[rows, :]
            vc = v - jnp.mean(v, axis=-1, keepdims=True)
            var = jnp.mean(vc * vc, axis=-1, keepdims=True)
            v = (vc * lax.rsqrt(var + EPS) * sng[...] + snb[...]).astype(BF16)
            s = sbias[...]
            for h in range(HEADS):
                sh = jnp.dot(sw[h], v, preferred_element_type=F32)
                s = s + jnp.where((lane >= h * HD) & (lane < (h + 1) * HD), sh, 0.0)
            y_ref[rows, :] = (s0[rows, :] * s).astype(BF16)
            return carry

        lax.fori_loop(0, seq // CHUNK, body, 0, unroll=4)

    @pl.when(mixer == 3)
    def _gated_conv():
        fill_padded(lambda rows: s1[rows, :] * s2[rows, :])

        def body(c, carry):
            rows = rows_at(c, CONV_ROWS)
            y_ref[rows, :] = (s0[rows, :] * conv_rows(c, cdw, CONV_D, CONV_D // 2)).astype(BF16)
            return carry

        lax.fori_loop(0, n_conv, body, 0)


def _mixers(proj, h0, lw, batch, seq, on_grid):
    proj3 = proj.reshape(batch, seq, N_IN)
    slab = (None, seq, W_GRP)

    def const(shape):
        return pl.BlockSpec(shape, lambda b, m: (0,) * len(shape))

    in_specs = [
        pl.BlockSpec(slab, lambda b, m: (b, 0, 2 * m)),
        pl.BlockSpec(slab, lambda b, m: (b, 0, 2 * m + 1)),
        pl.BlockSpec(slab, lambda b, m: (b, 0, 8)),
        const((CONV_A, W_GRP)), const((1, W_GRP)), const((1, W_GRP)), const((1, W_GRP)),
        const((CONV_B, W_GRP)), const((1, W_GRP)),
        const((W_GRP, 4 * W_GRP)), const((1, 4 * W_GRP)), const((2, W_GRP)),
        pl.BlockSpec((None, 2, W_GRP), lambda b, m: (b, 0, 0)),
        const((1, W_GRP)), const((1, W_GRP)), const((HEADS, CHUNK, CHUNK)), const((CHUNK, W_GRP)),
        const((CONV_D, W_GRP)), const((W_GRP, W_GRP)),
    ]
    y, st = pl.pallas_call(
        functools.partial(_mixer_kernel, seq=seq, on_grid=on_grid),
        grid=(batch, N_MIXERS),
        in_specs=in_specs,
        out_specs=(pl.BlockSpec(slab, lambda b, m: (b, 0, m)),
                   pl.BlockSpec((None, 2, W_GRP), lambda b, m: (b, 0, 0))),
        out_shape=(jax.ShapeDtypeStruct((batch, seq, D_MODEL), BF16),
                   jax.ShapeDtypeStruct((batch, 2, W_GRP), F32)),
        scratch_shapes=[pltpu.VMEM((seq + 2 * CONV_PAD, W_GRP), F32)] + [pltpu.VMEM((seq, W_GRP), F32)] * 4,
        compiler_params=pltpu.CompilerParams(dimension_semantics=("arbitrary", "arbitrary"),
                                             vmem_limit_bytes=VMEM_LIMIT),
        name="mixers",
    )(proj3, proj3, proj3, lw["conv_a_w"], lw["conv_a_b"], lw["norm_a_g"], lw["norm_a_b"],
      lw["conv_b_w"], lw["conv_b_b"], lw["gate_w"], lw["gate_b"], lw["lru_lam"], h0,
      lw["sgu_norm_g"], lw["sgu_norm_b"], lw["sgu_w"], lw["sgu_bias"], lw["conv_d_w"],
      lw["m_head"])
    return y.reshape(batch * seq, D_MODEL), st


def _outproj_kernel(y_ref, x_ref, mod_ref, g_ref, wo_ref, wr_ref, br_ref, tri_ref, cin_ref,
                    xm_ref, hn_ref, ei_ref, pr_ref, cnt_ref, seen_s):
    tm = x_ref.shape[0]

    @pl.when(pl.program_id(0) == 0)
    def _():
        seen_s[...] = cin_ref[...]

    y = jnp.dot(y_ref[...], wo_ref[...], preferred_element_type=F32)
    x = x_ref[...] + mod_ref[2:3, :] * y
    xm_ref[...] = x
    hn = _rmsnorm(x, g_ref[...]) * (1.0 + mod_ref[4:5, :]) + mod_ref[3:4, :]
    _store_token_tiles(hn_ref, hn)

    h1 = hn.astype(BF16)
    r1 = hn - h1.astype(F32)
    h2 = r1.astype(BF16)
    h3 = (r1 - h2.astype(F32)).astype(BF16)
    w_cat = wr_ref[...]
    p1 = jnp.dot(h1, w_cat, preferred_element_type=F32)
    p2 = jnp.dot(h2, w_cat, preferred_element_type=F32)
    p3 = jnp.dot(h3, w_cat, preferred_element_type=F32)
    e = N_EXPERTS
    logits = (p1[:, 2 * e:3 * e] + p2[:, e:2 * e] + p3[:, 0:e] + p1[:, e:2 * e] + p2[:, 0:e]
              + p1[:, 0:e]) + br_ref[...]

    lane = lax.broadcasted_iota(jnp.int32, (tm, N_EXPERTS), 1).astype(F32)
    work = logits
    vals, ids, sels = [], [], []
    for _ in range(TOP_K):
        mx = jnp.max(work, axis=-1, keepdims=True)
        idx = jnp.min(jnp.where(work == mx, lane, float(N_EXPERTS)), axis=-1, keepdims=True)
        sel = lane == idx
        work = jnp.where(sel, -jnp.inf, work)
        vals.append(mx)
        ids.append(idx)
        sels.append(sel)
    exps = [jnp.exp(v - vals[0]) for v in vals]
    den = exps[0] + exps[1] + exps[2] + exps[3]

    chosen = jnp.where(sels[0] | sels[1] | sels[2] | sels[3], 1.0, 0.0)
    earlier = jnp.dot(tri_ref[...], chosen.astype(BF16), preferred_element_type=F32) + seen_s[...]
    seen_s[...] = earlier[tm - 1:tm, :] + chosen[tm - 1:tm, :]
    cnt_ref[...] = seen_s[...]

    out_lane = lax.broadcasted_iota(jnp.int32, (tm, ROUTE_LANES), 1)
    ei = jnp.zeros((tm, ROUTE_LANES), F32)
    pr = jnp.zeros((tm, ROUTE_LANES), F32)
    for k in range(TOP_K):
        rank = jnp.sum(jnp.where(sels[k], earlier, 0.0), axis=-1, keepdims=True)
        ei = jnp.where(out_lane == k, ids[k], ei)
        ei = jnp.where(out_lane == TOP_K + k, rank, ei)
        pr = jnp.where(out_lane == k, exps[k] / den, pr)
    ei_ref[...] = ei.astype(jnp.int32)
    pr_ref[...] = pr


def _out_proj(y, x, mod, g, wo_bf16, w_router, b_router, seen, seq):
    n = x.shape[0]
    tm = min(ROW_TILE, seq)
    row = pl.BlockSpec((tm, D_MODEL), lambda i: (i, 0))
    route = pl.BlockSpec((tm, ROUTE_LANES), lambda i: (i, 0))
    cnt = pl.BlockSpec((1, N_EXPERTS), lambda i: (0, 0))
    tri = jnp.tri(tm, k=-1, dtype=BF16)
    w1 = w_router.astype(BF16)
    w2 = (w_router - w1.astype(F32)).astype(BF16)
    w3 = (w_router - w1.astype(F32) - w2.astype(F32)).astype(BF16)
    w_cat = jnp.concatenate([w1, w2, w3, jnp.zeros_like(w1)], axis=1)
    return pl.pallas_call(
        _outproj_kernel,
        grid=(n // tm,),
        in_specs=[row, row, _mod_spec(mod, seq, tm),
                  pl.BlockSpec((1, D_MODEL), lambda i: (0, 0)),
                  pl.BlockSpec((D_MODEL, D_MODEL), lambda i: (0, 0)),
                  pl.BlockSpec((D_MODEL, 4 * N_EXPERTS), lambda i: (0, 0)),
                  cnt, pl.BlockSpec((tm, tm), lambda i: (0, 0)), cnt],
        out_specs=(row, pl.BlockSpec((tm * TOKEN_ROWS, V7X_LANES), lambda i: (i, 0)), route, route, cnt),
        out_shape=(jax.ShapeDtypeStruct((n, D_MODEL), F32),
                   jax.ShapeDtypeStruct((n * TOKEN_ROWS, V7X_LANES), F32),
                   jax.ShapeDtypeStruct((n, ROUTE_LANES), jnp.int32),
                   jax.ShapeDtypeStruct((n, ROUTE_LANES), F32),
                   jax.ShapeDtypeStruct((1, N_EXPERTS), F32)),
        scratch_shapes=[pltpu.VMEM((1, N_EXPERTS), F32)],
        compiler_params=pltpu.CompilerParams(dimension_semantics=("arbitrary",)),
        name="out_proj",
    )(y, x, mod, g.reshape(1, D_MODEL), wo_bf16, w_cat, b_router.reshape(1, N_EXPERTS), tri, seen)


def _dispatch_kernel(zero_blk, zero_on, dest_ref, hn_p, hn_s, xbuf, zeros_s, sem, zsem, *, tiles_p):
    i = pl.program_id(0)

    @pl.when(i == 0)
    def _():
        zeros_s[...] = jnp.zeros_like(zeros_s)

        def fill(j):
            return pltpu.make_async_copy(zeros_s, xbuf.at[_token_rows(zero_blk[j] * MOE_TM, MOE_TM)], zsem)

        for j in range(zero_blk.shape[0]):
            @pl.when(zero_on[j] == 1)
            def _():
                fill(j).start()

        for j in range(zero_blk.shape[0]):
            @pl.when(zero_on[j] == 1)
            def _():
                fill(j).wait()

    def issue(src):
        def body(j, carry):
            row = src.at[_token_rows(j)]
            for k in range(TOP_K):
                pltpu.make_async_copy(row, xbuf.at[_token_rows(dest_ref[0, j * TOP_K + k])],
                                      sem).start(priority=k % DMA_PRIORITIES)
            return carry

        lax.fori_loop(0, DISPATCH_TM, body, 0, unroll=4)

    @pl.when(i < tiles_p)
    def _():
        issue(hn_p)

    @pl.when(i >= tiles_p)
    def _():
        issue(hn_s)

    for _ in range(TOP_K):
        pltpu.make_async_copy(hn_s, xbuf.at[_token_rows(0, DISPATCH_TM)], sem).wait()


def _dispatch(dest, zero_blk, zero_on, hn_p, hn_s, n_pad):
    n_p, n_s = hn_p.shape[0] // TOKEN_ROWS, hn_s.shape[0] // TOKEN_ROWS
    tiles_p = n_p // DISPATCH_TM
    n_steps = (n_p + n_s) // DISPATCH_TM
    tile = (DISPATCH_TM * TOKEN_ROWS, V7X_LANES)
    grid_spec = pltpu.PrefetchScalarGridSpec(
        num_scalar_prefetch=2,
        grid=(n_steps,),
        in_specs=[pl.BlockSpec((None, 1, DISPATCH_TM * TOP_K), lambda i, zb, zo: (i, 0, 0),
                               memory_space=pltpu.SMEM),
                  pl.BlockSpec(tile, lambda i, zb, zo: (jnp.minimum(i, tiles_p - 1), 0)),
                  pl.BlockSpec(tile, lambda i, zb, zo: (jnp.maximum(i - tiles_p, 0), 0))],
        out_specs=pl.BlockSpec(memory_space=pl.ANY),
        scratch_shapes=[pltpu.VMEM((MOE_TM * TOKEN_ROWS, V7X_LANES), F32), pltpu.SemaphoreType.DMA(()),
                        pltpu.SemaphoreType.DMA(())],
    )
    return pl.pallas_call(
        functools.partial(_dispatch_kernel, tiles_p=tiles_p),
        grid_spec=grid_spec,
        out_shape=jax.ShapeDtypeStruct((n_pad * TOKEN_ROWS, V7X_LANES), F32),
        compiler_params=pltpu.CompilerParams(dimension_semantics=("arbitrary",)),
        name="moe_dispatch",
    )(zero_blk, zero_on, dest.reshape(n_steps, 1, DISPATCH_TM * TOP_K), hn_p, hn_s)


def _moe_kernel(blk_e, blk_new, blk_on, x_ref, wgu_ref, bgu_ref, wdn_ref, bdn_ref, o_ref, wgu_s, wdn_s):
    i = pl.program_id(0)

    @pl.when(blk_on[i] == 0)
    def _():
        o_ref[...] = jnp.zeros_like(o_ref)

    @pl.when(blk_on[i] == 1)
    def _():
        @pl.when(blk_new[i] == 1)
        def _():
            wgu_s[...] = wgu_ref[...].astype(BF16)
            wdn_s[...] = wdn_ref[...].astype(BF16)

        x = jnp.concatenate([c.astype(BF16) for c in _load_token_tiles(x_ref, MOE_TM)], axis=1)
        gu = jnp.dot(x, wgu_s[...], preferred_element_type=F32) + bgu_ref[...]
        g = jnp.minimum(gu[:, :D_FF], SWIGLU_LIMIT)
        u = jnp.clip(gu[:, D_FF:], -SWIGLU_LIMIT, SWIGLU_LIMIT)
        act = (u + 1.0) * (g * jax.nn.sigmoid(SWIGLU_ALPHA * g))
        _store_token_tiles(o_ref, jnp.dot(act.astype(BF16), wdn_s[...], preferred_element_type=F32)
                           + bdn_ref[...])


def _moe_blocks(x_buf, blk_e, blk_new, blk_on, layer, w_gu, b_gu, w_dn, b_dn):
    n_pad = x_buf.shape[0] // TOKEN_ROWS
    n_blk = n_pad // MOE_TM
    tile = pl.BlockSpec((MOE_TM * TOKEN_ROWS, V7X_LANES), lambda i, e, nw, on: (i, 0))
    grid_spec = pltpu.PrefetchScalarGridSpec(
        num_scalar_prefetch=3,
        grid=(n_blk,),
        in_specs=[
            tile,
            pl.BlockSpec((None, None, D_MODEL, 2 * D_FF), lambda i, e, nw, on: (layer, e[i], 0, 0)),
            pl.BlockSpec((None, 1, 2 * D_FF), lambda i, e, nw, on: (e[i], 0, 0)),
            pl.BlockSpec((None, None, D_FF, D_MODEL), lambda i, e, nw, on: (layer, e[i], 0, 0)),
            pl.BlockSpec((None, 1, D_MODEL), lambda i, e, nw, on: (e[i], 0, 0)),
        ],
        out_specs=tile,
        scratch_shapes=[pltpu.VMEM((D_MODEL, 2 * D_FF), BF16), pltpu.VMEM((D_FF, D_MODEL), BF16)],
    )
    return pl.pallas_call(
        _moe_kernel,
        grid_spec=grid_spec,
        out_shape=jax.ShapeDtypeStruct((n_pad * TOKEN_ROWS, V7X_LANES), F32),
        compiler_params=pltpu.CompilerParams(dimension_semantics=("arbitrary",),
                                             vmem_limit_bytes=VMEM_LIMIT),
        name="moe_experts",
    )(blk_e, blk_new, blk_on, x_buf, w_gu, b_gu.reshape(N_EXPERTS, 1, 2 * D_FF),
      w_dn, b_dn.reshape(N_EXPERTS, 1, D_MODEL))


def _moe(hn_p, hn_s, ei_p, ei_s, counts, layer, w_gu, b_gu, w_dn, b_dn):
    n_asg = (ei_p.shape[0] + ei_s.shape[0]) * TOP_K
    n_pad = (n_asg + N_EXPERTS * (MOE_TM - 1) + MOE_TM - 1) // MOE_TM * MOE_TM
    n_blk = n_pad // MOE_TM
    counts = counts.reshape(N_EXPERTS).astype(jnp.int32)
    padded = (counts + MOE_TM - 1) // MOE_TM * MOE_TM
    pad_end = jnp.cumsum(padded)
    pad_start = pad_end - padded
    blk_row = jnp.arange(n_blk, dtype=jnp.int32) * MOE_TM
    blk_e = jnp.minimum(jnp.sum((blk_row[:, None] >= pad_end[None, :]).astype(jnp.int32), axis=1),
                        N_EXPERTS - 1)
    blk_on = (blk_row < pad_end[-1]).astype(jnp.int32)
    blk_new = jnp.concatenate([jnp.ones((1,), jnp.int32), (blk_e[1:] != blk_e[:-1]).astype(jnp.int32)])

    def slots(ei):
        expert, rank = ei[:, 0:TOP_K], ei[:, TOP_K:2 * TOP_K]
        start = jnp.sum(jnp.where(expert[:, :, None] == jnp.arange(N_EXPERTS, dtype=jnp.int32),
                                  pad_start, 0), axis=-1)
        return (start + rank).reshape(-1)

    tail = n_blk - n_asg // MOE_TM
    last_on = (counts % MOE_TM != 0).astype(jnp.int32)
    tail_blk = pad_end[-1] // MOE_TM + jnp.arange(tail, dtype=jnp.int32)
    tail_on = (tail_blk < n_blk).astype(jnp.int32)
    zero_blk = jnp.concatenate([(pad_end // MOE_TM - 1) * last_on, tail_blk * tail_on])
    zero_on = jnp.concatenate([last_on, tail_on])

    dest_p, dest_s = slots(ei_p), slots(ei_s)
    x_buf = _dispatch(jnp.concatenate([dest_p, dest_s]), zero_blk, zero_on, hn_p, hn_s, n_pad)
    y_buf = _moe_blocks(x_buf, blk_e, blk_new, blk_on, layer, w_gu, b_gu, w_dn, b_dn)
    return dest_p, dest_s, y_buf


def _final_kernel(dest_cur, dest_nxt, x_ref, pr_ref, ybuf, mod_ref, g_ref, o_ref, rows, sems):
    x = x_ref[...] + mod_ref[5:6, :] * _combine_experts(dest_cur, dest_nxt, pr_ref, ybuf, rows, sems)
    o_ref[...] = _rmsnorm(x, g_ref[...])


def _final_norm(x, route, mod, g, seq):
    n = x.shape[0]
    tm = min(ROW_TILE, seq)
    row = pl.BlockSpec((tm, D_MODEL), lambda i: (i, 0))
    args, specs, scratch = _combine_operands(*route, n, tm)
    return pl.pallas_call(
        _final_kernel,
        grid=(n // tm,),
        in_specs=specs[:2] + [row] + specs[2:] + [_mod_spec(mod, seq, tm),
                                                  pl.BlockSpec((1, D_MODEL), lambda i: (0, 0))],
        out_specs=row,
        out_shape=jax.ShapeDtypeStruct((n, D_MODEL), F32),
        scratch_shapes=scratch,
        compiler_params=pltpu.CompilerParams(dimension_semantics=("arbitrary",), vmem_limit_bytes=VMEM_LIMIT),
        name="final_norm",
    )(*args[:2], x, *args[2:], mod, g.reshape(1, D_MODEL))


def _block_diag(w):
    eye = jnp.eye(HEADS, dtype=w.dtype)
    return (eye[:, None, :, None] * w[:, :, None, :]).reshape(W_GRP, W_GRP)


def _layer_weights(l, p):
    gate_w = jnp.concatenate([_block_diag(p["lru_wr"][l, 0]), _block_diag(p["lru_wi"][l, 0]),
                              _block_diag(p["lru_wr"][l, 1]), _block_diag(p["lru_wi"][l, 1])], axis=1)
    gate_b = jnp.concatenate([p["lru_br"][l, 0], p["lru_bi"][l, 0], p["lru_br"][l, 1], p["lru_bi"][l, 1]])
    head_of = jnp.arange(W_GRP) // HD
    row = lambda v: v.reshape(1, W_GRP)
    return dict(
        conv_a_w=p["conv_a_w"][l], conv_a_b=row(p["conv_a_b"][l]),
        norm_a_g=row(p["norm_a_g"][l]), norm_a_b=row(p["norm_a_b"][l]),
        conv_b_w=p["conv_b_w"][l], conv_b_b=row(p["conv_b_b"][l]),
        gate_w=gate_w.astype(BF16), gate_b=gate_b.reshape(1, 4 * W_GRP), lru_lam=p["lru_lam"][l],
        sgu_norm_g=row(p["sgu_norm_g"][l]), sgu_norm_b=row(p["sgu_norm_b"][l]),
        sgu_w=p["sgu_w"][l].astype(BF16), sgu_bias=jnp.repeat(p["sgu_b"][l].T, HD, axis=1),
        conv_d_w=p["conv_d_w"][l],
        m_head=((head_of[:, None] == head_of[None, :]).astype(F32) / HD).astype(BF16),
    )


def kernel(x_prompt, x_sample, state_rglru, c, c_ctx, w_ada, b_ada, norm1_g, norm2_g, w_in, conv_a_w,
           conv_a_b, norm_a_g, norm_a_b, conv_b_w, conv_b_b, lru_wr, lru_br, lru_wi, lru_bi, lru_lam,
           sgu_norm_g, sgu_norm_b, sgu_w, sgu_b, conv_d_w, w_out, w_router, b_router, w_gu, b_gu, w_dn,
           b_dn, final_g):
    p = dict(conv_a_w=conv_a_w, conv_a_b=conv_a_b, norm_a_g=norm_a_g, norm_a_b=norm_a_b,
             conv_b_w=conv_b_w, conv_b_b=conv_b_b, lru_wr=lru_wr, lru_br=lru_br, lru_wi=lru_wi,
             lru_bi=lru_bi, lru_lam=lru_lam, sgu_norm_g=sgu_norm_g, sgu_norm_b=sgu_norm_b,
             sgu_w=sgu_w, sgu_b=sgu_b, conv_d_w=conv_d_w)
    bp, tp, _ = x_prompt.shape
    bs, ts, _ = x_sample.shape
    n_p, n_s = bp * tp, bs * ts

    cond_rows = jnp.zeros((COND_ROWS, D_MODEL), F32).at[0].set(c_ctx).at[1:1 + bs].set(c)
    mod = _ada_mod(cond_rows, w_ada, b_ada).reshape(DEPTH, COND_ROWS, 6, D_MODEL)

    xp = x_prompt.reshape(n_p, D_MODEL)
    xs = x_sample.reshape(n_s, D_MODEL)
    h0_ctx = jnp.zeros((bp, 2, W_GRP), F32)
    route_p = route_s = mod_p_prev = mod_s_prev = None
    no_tokens_seen = jnp.zeros((1, N_EXPERTS), F32)
    states = []
    for l in range(DEPTH):
        lw = _layer_weights(l, p)
        mod_p, mod_s = mod[l, 0:1], mod[l, 1:1 + bs]
        w_in_l = w_in[l].astype(BF16)
        w_out_l = w_out[l].astype(BF16)
        xp, proj_p = _in_proj(xp, route_p, mod_p_prev, mod_p, norm1_g[l], w_in_l, tp)
        xs, proj_s = _in_proj(xs, route_s, mod_s_prev, mod_s, norm1_g[l], w_in_l, ts)
        y_p, st = _mixers(proj_p, h0_ctx, lw, bp, tp, False)
        y_s, _ = _mixers(proj_s, state_rglru[:, l], lw, bs, ts, True)
        states.append(st)
        xp, hn_p, ei_p, pr_p, seen = _out_proj(y_p, xp, mod_p, norm2_g[l], w_out_l, w_router[l],
                                               b_router[l], no_tokens_seen, tp)
        xs, hn_s, ei_s, pr_s, counts = _out_proj(y_s, xs, mod_s, norm2_g[l], w_out_l, w_router[l],
                                                 b_router[l], seen, ts)
        dest_p, dest_s, y_buf = _moe(hn_p, hn_s, ei_p, ei_s, counts, l, w_gu, b_gu[l], w_dn, b_dn[l])
        route_p, route_s = (dest_p, pr_p, y_buf), (dest_s, pr_s, y_buf)
        mod_p_prev, mod_s_prev = mod_p, mod_s
    y_prompt = _final_norm(xp, route_p, mod_p_prev, final_g, tp).reshape(bp, tp, D_MODEL)
    y_sample = _final_norm(xs, route_s, mod_s_prev, final_g, ts).reshape(bs, ts, D_MODEL)
    return y_prompt, y_sample, jnp.stack(states, axis=1)
```

```python
import functools

import jax
import jax.numpy as jnp
from jax import lax
from jax.experimental import pallas as pl
from jax.experimental.pallas import tpu as pltpu

F32 = jnp.float32
BF16 = jnp.bfloat16

D_MODEL = 1024
DEPTH = 2
GRID_W = 64
N_MIXERS = 4
W_GRP = D_MODEL // N_MIXERS
HEADS = 4
HD = W_GRP // HEADS
N_IN = 9 * W_GRP
CONV_A = 31
CONV_B = 4
CONV_D = 3
CHUNK = 128
LRU_C = 8.0
N_EXPERTS = 32
TOP_K = 4
D_FF = D_MODEL
SWIGLU_LIMIT = 7.0
SWIGLU_ALPHA = 1.702
EPS = 1e-6

V7X_SUBLANES = 8
V7X_LANES = 128
TOKEN_ROWS = D_MODEL // V7X_LANES
DMA_PRIORITIES = 2
V7X_VMEM_BYTES = 64 * 1024 * 1024
VMEM_LIMIT = V7X_VMEM_BYTES * 7 // 8

COND_ROWS = 16
ADA_TN = 1536
ROW_TILE = 512
CONV_ROWS = GRID_W
CONV_PAD = 16
NORM_ROWS = 256
MOE_TM = 512
DISPATCH_TM = 512
ROUTE_LANES = 128


def _rmsnorm(x, g):
    return x * lax.rsqrt(jnp.mean(x * x, axis=-1, keepdims=True) + EPS) * g


def _group_mean(x, m_ref):
    hi = x.astype(BF16)
    lo = (x - hi.astype(F32)).astype(BF16)
    m = m_ref[...]
    return (jnp.dot(hi, m, preferred_element_type=F32) + jnp.dot(lo, m, preferred_element_type=F32))


def _layernorm(x, g, b, m_ref):
    xc = x - _group_mean(x, m_ref)
    var = _group_mean(xc * xc, m_ref)
    return xc * lax.rsqrt(var + EPS) * g + b


def _ada_kernel(c_ref, w_ref, b_ref, o_ref):
    c = c_ref[...]
    cond = (c * jax.nn.sigmoid(c)).astype(BF16)
    o_ref[...] = jnp.dot(cond, w_ref[...].astype(BF16), preferred_element_type=F32) + b_ref[...]


def _ada_mod(cond_rows, w_ada, b_ada):
    n_col = w_ada.shape[-1]
    return pl.pallas_call(
        _ada_kernel,
        grid=(DEPTH, n_col // ADA_TN),
        in_specs=[
            pl.BlockSpec((COND_ROWS, D_MODEL), lambda l, j: (0, 0)),
            pl.BlockSpec((None, D_MODEL, ADA_TN), lambda l, j: (l, 0, j)),
            pl.BlockSpec((None, 1, ADA_TN), lambda l, j: (l, 0, j)),
        ],
        out_specs=pl.BlockSpec((None, COND_ROWS, ADA_TN), lambda l, j: (l, 0, j)),
        out_shape=jax.ShapeDtypeStruct((DEPTH, COND_ROWS, n_col), F32),
        name="ada_mod",
    )(cond_rows, w_ada, b_ada.reshape(DEPTH, 1, n_col))


def _token_rows(t, count=1):
    return pl.ds(pl.multiple_of(t * TOKEN_ROWS, TOKEN_ROWS), count * TOKEN_ROWS)


def _store_token_tiles(ref, x):
    n = x.shape[0]
    for c in range(TOKEN_ROWS):
        ref[pl.ds(c, n, stride=TOKEN_ROWS), :] = x[:, c * V7X_LANES:(c + 1) * V7X_LANES]


def _load_token_tiles(ref, n):
    return [ref[pl.ds(c, n, stride=TOKEN_ROWS), :] for c in range(TOKEN_ROWS)]


def _combine_experts(dest_cur, dest_nxt, pr_ref, ybuf, rows, sems):
    i = pl.program_id(0)
    n_steps = pl.num_programs(0)
    tm = rows.shape[2] // TOKEN_ROWS
    slot = i % 2

    def row_copy(dref, j, k, slot_):
        return pltpu.make_async_copy(ybuf.at[_token_rows(dref[0, j * TOP_K + k])],
                                     rows.at[slot_, k, _token_rows(j)], sems.at[slot_])

    def issue(dref, slot_):
        def body(j, carry):
            for k in range(TOP_K):
                row_copy(dref, j, k, slot_).start(priority=k % DMA_PRIORITIES)
            return carry

        lax.fori_loop(0, tm, body, 0, unroll=4)

    @pl.when(i == 0)
    def _():
        issue(dest_cur, 0)

    @pl.when(i + 1 < n_steps)
    def _():
        issue(dest_nxt, 1 - slot)

    for k in range(TOP_K):
        pltpu.make_async_copy(ybuf.at[_token_rows(0, tm)], rows.at[slot, k], sems.at[slot]).wait()
    experts = [_load_token_tiles(rows.at[slot, k], tm) for k in range(TOP_K)]
    chunks = []
    for c in range(TOKEN_ROWS):
        acc = pr_ref[:, 0:1] * experts[0][c]
        for k in range(1, TOP_K):
            acc = acc + pr_ref[:, k:k + 1] * experts[k][c]
        chunks.append(acc)
    return jnp.concatenate(chunks, axis=1)


def _inproj_kernel(*refs, has_res):
    if has_res:
        (dest_cur, dest_nxt, x_ref, pr_ref, ybuf, modp_ref, mod_ref, g_ref, w_ref,
         xo_ref, p_ref, rows, sems) = refs
        x = x_ref[...] + modp_ref[5:6, :] * _combine_experts(dest_cur, dest_nxt, pr_ref, ybuf, rows, sems)
        xo_ref[...] = x
    else:
        x_ref, mod_ref, g_ref, w_ref, p_ref = refs
        x = x_ref[...]
    hn = _rmsnorm(x, g_ref[...]) * (1.0 + mod_ref[1:2, :]) + mod_ref[0:1, :]
    p_ref[...] = jnp.dot(hn.astype(BF16), w_ref[...], preferred_element_type=F32)


def _mod_spec(mod, seq, tm):
    if mod.shape[0] == 1:
        return pl.BlockSpec((None, 6, D_MODEL), lambda i: (0, 0, 0))
    return pl.BlockSpec((None, 6, D_MODEL), lambda i: ((i * tm) // seq, 0, 0))


def _combine_operands(dest, probs, y_buf, n, tm):
    n_steps = n // tm
    dest3 = dest.reshape(n_steps, 1, tm * TOP_K)
    smem = functools.partial(pl.BlockSpec, (None, 1, tm * TOP_K), memory_space=pltpu.SMEM)
    args = [dest3, dest3, probs, y_buf]
    specs = [smem(lambda i: (i, 0, 0)),
             smem(lambda i: (jnp.minimum(i + 1, n_steps - 1), 0, 0)),
             pl.BlockSpec((tm, ROUTE_LANES), lambda i: (i, 0)),
             pl.BlockSpec(memory_space=pl.ANY)]
    scratch = [pltpu.VMEM((2, TOP_K, tm * TOKEN_ROWS, V7X_LANES), F32), pltpu.SemaphoreType.DMA((2,))]
    return args, specs, scratch


def _in_proj(x, route, mod_prev, mod, g, w_bf16, seq):
    n = x.shape[0]
    tm = min(ROW_TILE, seq)
    row = pl.BlockSpec((tm, D_MODEL), lambda i: (i, 0))
    has_res = route is not None
    args, specs, scratch = [], [], []
    if has_res:
        args, specs, scratch = _combine_operands(*route, n, tm)
        args = args[:2] + [x] + args[2:] + [mod_prev]
        specs = specs[:2] + [row] + specs[2:] + [_mod_spec(mod_prev, seq, tm)]
    else:
        args, specs = [x], [row]
    args += [mod, g.reshape(1, D_MODEL), w_bf16]
    specs += [_mod_spec(mod, seq, tm),
              pl.BlockSpec((1, D_MODEL), lambda i: (0, 0)),
              pl.BlockSpec((D_MODEL, N_IN), lambda i: (0, 0))]
    proj_shape = jax.ShapeDtypeStruct((n, N_IN), F32)
    proj_spec = pl.BlockSpec((tm, N_IN), lambda i: (i, 0))
    if has_res:
        out_shape = (jax.ShapeDtypeStruct((n, D_MODEL), F32), proj_shape)
        out_specs = (row, proj_spec)
    else:
        out_shape, out_specs = proj_shape, proj_spec
    out = pl.pallas_call(
        functools.partial(_inproj_kernel, has_res=has_res),
        grid=(n // tm,),
        in_specs=specs,
        out_specs=out_specs,
        out_shape=out_shape,
        scratch_shapes=scratch,
        compiler_params=pltpu.CompilerParams(dimension_semantics=("arbitrary",), vmem_limit_bytes=VMEM_LIMIT),
        name="in_proj",
    )(*args)
    return out if has_res else (x, out)


def _group_scan(a, b, reverse):
    shape = a.shape
    grouped = (shape[0] // V7X_SUBLANES, V7X_SUBLANES, shape[1])
    a, b = a.reshape(grouped), b.reshape(grouped)
    ri = lax.broadcasted_iota(jnp.int32, grouped, 1)
    for d in (1, 2, 4):
        shift = V7X_SUBLANES - d if reverse else d
        keep = ri < V7X_SUBLANES - d if reverse else ri >= d
        ra, rb = pltpu.roll(a, shift, 1), pltpu.roll(b, shift, 1)
        b = a * jnp.where(keep, rb, 0.0) + b
        a = a * jnp.where(keep, ra, 1.0)
    return a.reshape(shape), b.reshape(shape)


def _mixer_kernel(s0, s1, s2, caw, cab, nag, nab, cbw, cbb, wg, bg, lam, h0, sng, snb, sw, sbias, cdw,
                  m_head, y_ref, st_ref, pad_s, af_s, bf_s, ab_s, bb_s, *, seq, on_grid):
    mixer = pl.program_id(1)
    n_conv = seq // CONV_ROWS
    win_rows = CONV_ROWS + 2 * CONV_PAD

    def rows_at(c, size):
        return pl.ds(pl.multiple_of(c * size, size), size)

    def fill_padded(fn):
        zeros = jnp.zeros((CONV_PAD, W_GRP), F32)
        pad_s[0:CONV_PAD, :] = zeros
        pad_s[CONV_PAD + seq:2 * CONV_PAD + seq, :] = zeros

        def body(c, carry):
            dst = pl.ds(pl.multiple_of(c * CONV_ROWS + CONV_PAD, V7X_SUBLANES), CONV_ROWS)
            pad_s[dst, :] = fn(rows_at(c, CONV_ROWS))
            return carry

        lax.fori_loop(0, n_conv, body, 0)

    def conv_window(win, w_ref, taps, pad_l):
        acc = jnp.zeros((CONV_ROWS, W_GRP), F32)
        for mis in range(V7X_SUBLANES):
            starts = [(k, CONV_PAD - pad_l + k) for k in range(taps)
                      if (CONV_PAD - pad_l + k) % V7X_SUBLANES == mis]
            if not starts:
                continue
            shifted = pltpu.roll(win, win_rows - mis, 0) if mis else win
            for k, start in starts:
                acc = acc + w_ref[k:k + 1, :] * shifted[start - mis:start - mis + CONV_ROWS, :]
        return acc

    def conv_rows(c, w_ref, taps, pad_l):
        win = pad_s[pl.ds(pl.multiple_of(c * CONV_ROWS, CONV_ROWS), win_rows), :]
        return conv_window(win, w_ref, taps, pad_l)

    @pl.when(mixer == 0)
    def _conformer():
        def glu(rows):
            return s0[rows, :] * jax.nn.sigmoid(s1[rows, :])

        if not on_grid:
            fill_padded(glu)

        def body(c, carry):
            if on_grid:
                edge = jnp.zeros((CONV_PAD, W_GRP), F32)
                win = jnp.concatenate([edge, glu(rows_at(c, CONV_ROWS)), edge], axis=0)
                z = conv_window(win, caw, CONV_A, CONV_A // 2)
            else:
                z = conv_rows(c, caw, CONV_A, CONV_A // 2)
            af_s[rows_at(c, CONV_ROWS), :] = z + cab[...]
            return carry

        lax.fori_loop(0, n_conv, body, 0)

        def norm(c, carry):
            rows = rows_at(c, NORM_ROWS)
            z = _layernorm(af_s[rows, :], nag[...], nab[...], m_head)
            y_ref[rows, :] = (z * jax.nn.sigmoid(z)).astype(BF16)
            return carry

        lax.fori_loop(0, seq // NORM_ROWS, norm, 0, unroll=2)

    @pl.when(mixer == 1)
    def _rglru():
        fill_padded(lambda rows: s0[rows, :])
        lam_v = lam[...]
        softplus_neg = jnp.maximum(-lam_v, 0.0) + jnp.log1p(jnp.exp(-jnp.abs(lam_v)))

        def gates(c, carry):
            rows = rows_at(c, CONV_ROWS)
            xc = conv_rows(c, cbw, CONV_B, 2) + cbb[...]
            gt = jax.nn.sigmoid(jnp.dot(xc.astype(BF16), wg[...], preferred_element_type=F32) + bg[...])
            for d, (a_s, b_s) in enumerate(((af_s, bf_s), (ab_s, bb_s))):
                r_gate = gt[:, (2 * d) * W_GRP:(2 * d + 1) * W_GRP]
                i_gate = gt[:, (2 * d + 1) * W_GRP:(2 * d + 2) * W_GRP]
                log_a = -LRU_C * r_gate * softplus_neg[d:d + 1, :]
                a = jnp.exp(log_a)
                b = jnp.sqrt(jnp.maximum(-jnp.tanh(log_a) * (a * a + 1.0), 0.0)) * (i_gate * xc)
                a, b = _group_scan(a, b, reverse=(d == 1))
                a_s[rows, :] = a
                b_s[rows, :] = b
            return carry

        lax.fori_loop(0, n_conv, gates, 0, unroll=2)

        n_grp = seq // V7X_SUBLANES

        def chain(g, carry):
            cf, cb = carry
            rf = rows_at(g, V7X_SUBLANES)
            rb = rows_at(n_grp - 1 - g, V7X_SUBLANES)
            hf = af_s[rf, :] * cf + bf_s[rf, :]
            bf_s[rf, :] = hf
            hb = ab_s[rb, :] * cb + bb_s[rb, :]
            bb_s[rb, :] = hb
            cf = jnp.broadcast_to(hf[V7X_SUBLANES - 1:V7X_SUBLANES, :], (V7X_SUBLANES, W_GRP))
            cb = jnp.broadcast_to(hb[0:1, :], (V7X_SUBLANES, W_GRP))
            return cf, cb

        init = (jnp.broadcast_to(h0[0:1, :], (V7X_SUBLANES, W_GRP)),
                jnp.broadcast_to(h0[1:2, :], (V7X_SUBLANES, W_GRP)))
        cf, cb = lax.fori_loop(0, n_grp, chain, init, unroll=4)
        st_ref[0:1, :] = cf[0:1, :]
        st_ref[1:2, :] = cb[0:1, :]

        def out(c, carry):
            rows = rows_at(c, CONV_ROWS)
            y_ref[rows, :] = (jax.nn.gelu(s1[rows, :]) * (bf_s[rows, :] + bb_s[rows, :])).astype(BF16)
            return carry

        lax.fori_loop(0, n_conv, out, 0)

    @pl.when(mixer == 2)
    def _sgu():
        lane = lax.broadcasted_iota(jnp.int32, (CHUNK, W_GRP), 1)

        def body(n, carry):
            rows = rows_at(n, CHUNK)
            v = s1[rows, :]
            vc = v - jnp.mean(v, axis=-1, keepdims=True)
            var = jnp.mean(vc * vc, axis=-1, keepdims=True)
            v = (vc * lax.rsqrt(var + EPS) * sng[...] + snb[...]).astype(BF16)
            s = sbias[...]
            for h in range(HEADS):
                sh = jnp.dot(sw[h], v, preferred_element_type=F32)
                s = s + jnp.where((lane >= h * HD) & (lane < (h + 1) * HD), sh, 0.0)
            y_ref[rows, :] = (s0[rows, :] * s).astype(BF16)
            return carry

        lax.fori_loop(0, seq // CHUNK, body, 0, unroll=4)

    @pl.when(mixer == 3)
    def _gated_conv():
        fill_padded(lambda rows: s1[rows, :] * s2[rows, :])

        def body(c, carry):
            rows = rows_at(c, CONV_ROWS)
            y_ref[rows, :] = (s0[rows, :] * conv_rows(c, cdw, CONV_D, CONV_D // 2)).astype(BF16)
            return carry

        lax.fori_loop(0, n_conv, body, 0)


def _mixers(proj, h0, lw, batch, seq, on_grid):
    proj3 = proj.reshape(batch, seq, N_IN)
    slab = (None, seq, W_GRP)

    def const(shape):
        return pl.BlockSpec(shape, lambda b, m: (0,) * len(shape))

    in_specs = [
        pl.BlockSpec(slab, lambda b, m: (b, 0, 2 * m)),
        pl.BlockSpec(slab, lambda b, m: (b, 0, 2 * m + 1)),
        pl.BlockSpec(slab, lambda b, m: (b, 0, 8)),
        const((CONV_A, W_GRP)), const((1, W_GRP)), const((1, W_GRP)), const((1, W_GRP)),
        const((CONV_B, W_GRP)), const((1, W_GRP)),
        const((W_GRP, 4 * W_GRP)), const((1, 4 * W_GRP)), const((2, W_GRP)),
        pl.BlockSpec((None, 2, W_GRP), lambda b, m: (b, 0, 0)),
        const((1, W_GRP)), const((1, W_GRP)), const((HEADS, CHUNK, CHUNK)), const((CHUNK, W_GRP)),
        const((CONV_D, W_GRP)), const((W_GRP, W_GRP)),
    ]
    y, st = pl.pallas_call(
        functools.partial(_mixer_kernel, seq=seq, on_grid=on_grid),
        grid=(batch, N_MIXERS),
        in_specs=in_specs,
        out_specs=(pl.BlockSpec(slab, lambda b, m: (b, 0, m)),
                   pl.BlockSpec((None, 2, W_GRP), lambda b, m: (b, 0, 0))),
        out_shape=(jax.ShapeDtypeStruct((batch, seq, D_MODEL), BF16),
                   jax.ShapeDtypeStruct((batch, 2, W_GRP), F32)),
        scratch_shapes=[pltpu.VMEM((seq + 2 * CONV_PAD, W_GRP), F32)] + [pltpu.VMEM((seq, W_GRP), F32)] * 4,
        compiler_params=pltpu.CompilerParams(dimension_semantics=("arbitrary", "arbitrary"),
                                             vmem_limit_bytes=VMEM_LIMIT),
        name="mixers",
    )(proj3, proj3, proj3, lw["conv_a_w"], lw["conv_a_b"], lw["norm_a_g"], lw["norm_a_b"],
      lw["conv_b_w"], lw["conv_b_b"], lw["gate_w"], lw["gate_b"], lw["lru_lam"], h0,
      lw["sgu_norm_g"], lw["sgu_norm_b"], lw["sgu_w"], lw["sgu_bias"], lw["conv_d_w"],
      lw["m_head"])
    return y.reshape(batch * seq, D_MODEL), st


def _outproj_kernel(y_ref, x_ref, mod_ref, g_ref, wo_ref, wr_ref, br_ref, tri_ref, cin_ref,
                    xm_ref, hn_ref, ei_ref, pr_ref, cnt_ref, seen_s):
    tm = x_ref.shape[0]

    @pl.when(pl.program_id(0) == 0)
    def _():
        seen_s[...] = cin_ref[...]

    y = jnp.dot(y_ref[...], wo_ref[...], preferred_element_type=F32)
    x = x_ref[...] + mod_ref[2:3, :] * y
    xm_ref[...] = x
    hn = _rmsnorm(x, g_ref[...]) * (1.0 + mod_ref[4:5, :]) + mod_ref[3:4, :]
    _store_token_tiles(hn_ref, hn)

    h1 = hn.astype(BF16)
    r1 = hn - h1.astype(F32)
    h2 = r1.astype(BF16)
    h3 = (r1 - h2.astype(F32)).astype(BF16)
    w_cat = wr_ref[...]
    p1 = jnp.dot(h1, w_cat, preferred_element_type=F32)
    p2 = jnp.dot(h2, w_cat, preferred_element_type=F32)
    p3 = jnp.dot(h3, w_cat, preferred_element_type=F32)
    e = N_EXPERTS
    logits = (p1[:, 2 * e:3 * e] + p2[:, e:2 * e] + p3[:, 0:e] + p1[:, e:2 * e] + p2[:, 0:e]
              + p1[:, 0:e]) + br_ref[...]

    lane = lax.broadcasted_iota(jnp.int32, (tm, N_EXPERTS), 1).astype(F32)
    work = logits
    vals, ids, sels = [], [], []
    for _ in range(TOP_K):
        mx = jnp.max(work, axis=-1, keepdims=True)
        idx = jnp.min(jnp.where(work == mx, lane, float(N_EXPERTS)), axis=-1, keepdims=True)
        sel = lane == idx
        work = jnp.where(sel, -jnp.inf, work)
        vals.append(mx)
        ids.append(idx)
        sels.append(sel)
    exps = [jnp.exp(v - vals[0]) for v in vals]
    den = exps[0] + exps[1] + exps[2] + exps[3]

    chosen = jnp.where(sels[0] | sels[1] | sels[2] | sels[3], 1.0, 0.0)
    earlier = jnp.dot(tri_ref[...], chosen.astype(BF16), preferred_element_type=F32) + seen_s[...]
    seen_s[...] = earlier[tm - 1:tm, :] + chosen[tm - 1:tm, :]
    cnt_ref[...] = seen_s[...]

    out_lane = lax.broadcasted_iota(jnp.int32, (tm, ROUTE_LANES), 1)
    ei = jnp.zeros((tm, ROUTE_LANES), F32)
    pr = jnp.zeros((tm, ROUTE_LANES), F32)
    for k in range(TOP_K):
        rank = jnp.sum(jnp.where(sels[k], earlier, 0.0), axis=-1, keepdims=True)
        ei = jnp.where(out_lane == k, ids[k], ei)
        ei = jnp.where(out_lane == TOP_K + k, rank, ei)
        pr = jnp.where(out_lane == k, exps[k] / den, pr)
    ei_ref[...] = ei.astype(jnp.int32)
    pr_ref[...] = pr


def _out_proj(y, x, mod, g, wo_bf16, w_router, b_router, seen, seq):
    n = x.shape[0]
    tm = min(ROW_TILE, seq)
    row = pl.BlockSpec((tm, D_MODEL), lambda i: (i, 0))
    route = pl.BlockSpec((tm, ROUTE_LANES), lambda i: (i, 0))
    cnt = pl.BlockSpec((1, N_EXPERTS), lambda i: (0, 0))
    tri = jnp.tri(tm, k=-1, dtype=BF16)
    w1 = w_router.astype(BF16)
    w2 = (w_router - w1.astype(F32)).astype(BF16)
    w3 = (w_router - w1.astype(F32) - w2.astype(F32)).astype(BF16)
    w_cat = jnp.concatenate([w1, w2, w3, jnp.zeros_like(w1)], axis=1)
    return pl.pallas_call(
        _outproj_kernel,
        grid=(n // tm,),
        in_specs=[row, row, _mod_spec(mod, seq, tm),
                  pl.BlockSpec((1, D_MODEL), lambda i: (0, 0)),
                  pl.BlockSpec((D_MODEL, D_MODEL), lambda i: (0, 0)),
                  pl.BlockSpec((D_MODEL, 4 * N_EXPERTS), lambda i: (0, 0)),
                  cnt, pl.BlockSpec((tm, tm), lambda i: (0, 0)), cnt],
        out_specs=(row, pl.BlockSpec((tm * TOKEN_ROWS, V7X_LANES), lambda i: (i, 0)), route, route, cnt),
        out_shape=(jax.ShapeDtypeStruct((n, D_MODEL), F32),
                   jax.ShapeDtypeStruct((n * TOKEN_ROWS, V7X_LANES), F32),
                   jax.ShapeDtypeStruct((n, ROUTE_LANES), jnp.int32),
                   jax.ShapeDtypeStruct((n, ROUTE_LANES), F32),
                   jax.ShapeDtypeStruct((1, N_EXPERTS), F32)),
        scratch_shapes=[pltpu.VMEM((1, N_EXPERTS), F32)],
        compiler_params=pltpu.CompilerParams(dimension_semantics=("arbitrary",)),
        name="out_proj",
    )(y, x, mod, g.reshape(1, D_MODEL), wo_bf16, w_cat, b_router.reshape(1, N_EXPERTS), tri, seen)


def _dispatch_kernel(zero_blk, zero_on, dest_ref, hn_p, hn_s, xbuf, zeros_s, sem, zsem, *, tiles_p):
    i = pl.program_id(0)

    @pl.when(i == 0)
    def _():
        zeros_s[...] = jnp.zeros_like(zeros_s)

        def fill(j):
            return pltpu.make_async_copy(zeros_s, xbuf.at[_token_rows(zero_blk[j] * MOE_TM, MOE_TM)], zsem)

        for j in range(zero_blk.shape[0]):
            @pl.when(zero_on[j] == 1)
            def _():
                fill(j).start()

        for j in range(zero_blk.shape[0]):
            @pl.when(zero_on[j] == 1)
            def _():
                fill(j).wait()

    def issue(src):
        def body(j, carry):
            row = src.at[_token_rows(j)]
            for k in range(TOP_K):
                pltpu.make_async_copy(row, xbuf.at[_token_rows(dest_ref[0, j * TOP_K + k])],
                                      sem).start(priority=k % DMA_PRIORITIES)
            return carry

        lax.fori_loop(0, DISPATCH_TM, body, 0, unroll=4)

    @pl.when(i < tiles_p)
    def _():
        issue(hn_p)

    @pl.when(i >= tiles_p)
    def _():
        issue(hn_s)

    for _ in range(TOP_K):
        pltpu.make_async_copy(hn_s, xbuf.at[_token_rows(0, DISPATCH_TM)], sem).wait()


def _dispatch(dest, zero_blk, zero_on, hn_p, hn_s, n_pad):
    n_p, n_s = hn_p.shape[0] // TOKEN_ROWS, hn_s.shape[0] // TOKEN_ROWS
    tiles_p = n_p // DISPATCH_TM
    n_steps = (n_p + n_s) // DISPATCH_TM
    tile = (DISPATCH_TM * TOKEN_ROWS, V7X_LANES)
    grid_spec = pltpu.PrefetchScalarGridSpec(
        num_scalar_prefetch=2,
        grid=(n_steps,),
        in_specs=[pl.BlockSpec((None, 1, DISPATCH_TM * TOP_K), lambda i, zb, zo: (i, 0, 0),
                               memory_space=pltpu.SMEM),
                  pl.BlockSpec(tile, lambda i, zb, zo: (jnp.minimum(i, tiles_p - 1), 0)),
                  pl.BlockSpec(tile, lambda i, zb, zo: (jnp.maximum(i - tiles_p, 0), 0))],
        out_specs=pl.BlockSpec(memory_space=pl.ANY),
        scratch_shapes=[pltpu.VMEM((MOE_TM * TOKEN_ROWS, V7X_LANES), F32), pltpu.SemaphoreType.DMA(()),
                        pltpu.SemaphoreType.DMA(())],
    )
    return pl.pallas_call(
        functools.partial(_dispatch_kernel, tiles_p=tiles_p),
        grid_spec=grid_spec,
        out_shape=jax.ShapeDtypeStruct((n_pad * TOKEN_ROWS, V7X_LANES), F32),
        compiler_params=pltpu.CompilerParams(dimension_semantics=("arbitrary",)),
        name="moe_dispatch",
    )(zero_blk, zero_on, dest.reshape(n_steps, 1, DISPATCH_TM * TOP_K), hn_p, hn_s)


def _moe_kernel(blk_e, blk_new, blk_on, blk_slot, blk_next, x_ref, wgu_hbm, bgu_ref, wdn_hbm, bdn_ref, o_ref,
                wgu_f, wdn_f, wgu_s, wdn_s, sems, *, layer):
    i = pl.program_id(0)

    def fetch(expert, slot):
        return (pltpu.make_async_copy(wgu_hbm.at[layer, expert], wgu_f.at[slot], sems.at[0, slot]),
                pltpu.make_async_copy(wdn_hbm.at[layer, expert], wdn_f.at[slot], sems.at[1, slot]))

    @pl.when(i == 0)
    def _():
        for copy in fetch(blk_e[0], 0):
            copy.start()

    @pl.when(blk_on[i] == 0)
    def _():
        o_ref[...] = jnp.zeros_like(o_ref)

    @pl.when(blk_on[i] == 1)
    def _():
        @pl.when(blk_new[i] == 1)
        def _():
            slot = blk_slot[i]
            for copy in fetch(blk_e[i], slot):
                copy.wait()

            @pl.when(blk_next[i] >= 0)
            def _():
                for copy in fetch(blk_next[i], 1 - slot):
                    copy.start()

            wgu_s[...] = wgu_f[slot].astype(BF16)
            wdn_s[...] = wdn_f[slot].astype(BF16)

        x = jnp.concatenate([c.astype(BF16) for c in _load_token_tiles(x_ref, MOE_TM)], axis=1)
        gu = jnp.dot(x, wgu_s[...], preferred_element_type=F32) + bgu_ref[...]
        g = jnp.minimum(gu[:, :D_FF], SWIGLU_LIMIT)
        u = jnp.clip(gu[:, D_FF:], -SWIGLU_LIMIT, SWIGLU_LIMIT)
        act = (u + 1.0) * (g * jax.nn.sigmoid(SWIGLU_ALPHA * g))
        _store_token_tiles(o_ref, jnp.dot(act.astype(BF16), wdn_s[...], preferred_element_type=F32)
                           + bdn_ref[...])


def _moe_blocks(x_buf, blk, layer, w_gu, b_gu, w_dn, b_dn):
    n_pad = x_buf.shape[0] // TOKEN_ROWS
    n_blk = n_pad // MOE_TM
    tile = pl.BlockSpec((MOE_TM * TOKEN_ROWS, V7X_LANES), lambda i, e, *_: (i, 0))
    grid_spec = pltpu.PrefetchScalarGridSpec(
        num_scalar_prefetch=len(blk),
        grid=(n_blk,),
        in_specs=[
            tile,
            pl.BlockSpec(memory_space=pl.ANY),
            pl.BlockSpec((None, 1, 2 * D_FF), lambda i, e, *_: (e[i], 0, 0)),
            pl.BlockSpec(memory_space=pl.ANY),
            pl.BlockSpec((None, 1, D_MODEL), lambda i, e, *_: (e[i], 0, 0)),
        ],
        out_specs=tile,
        scratch_shapes=[pltpu.VMEM((2, D_MODEL, 2 * D_FF), F32), pltpu.VMEM((2, D_FF, D_MODEL), F32),
                        pltpu.VMEM((D_MODEL, 2 * D_FF), BF16), pltpu.VMEM((D_FF, D_MODEL), BF16),
                        pltpu.SemaphoreType.DMA((2, 2))],
    )
    return pl.pallas_call(
        functools.partial(_moe_kernel, layer=layer),
        grid_spec=grid_spec,
        out_shape=jax.ShapeDtypeStruct((n_pad * TOKEN_ROWS, V7X_LANES), F32),
        compiler_params=pltpu.CompilerParams(dimension_semantics=("arbitrary",),
                                             vmem_limit_bytes=VMEM_LIMIT),
        name="moe_experts",
    )(*blk, x_buf, w_gu, b_gu.reshape(N_EXPERTS, 1, 2 * D_FF), w_dn, b_dn.reshape(N_EXPERTS, 1, D_MODEL))


def _moe(hn_p, hn_s, ei_p, ei_s, counts, layer, w_gu, b_gu, w_dn, b_dn):
    n_asg = (ei_p.shape[0] + ei_s.shape[0]) * TOP_K
    n_pad = (n_asg + N_EXPERTS * (MOE_TM - 1) + MOE_TM - 1) // MOE_TM * MOE_TM
    n_blk = n_pad // MOE_TM
    counts = counts.reshape(N_EXPERTS).astype(jnp.int32)
    padded = (counts + MOE_TM - 1) // MOE_TM * MOE_TM
    pad_end = jnp.cumsum(padded)
    pad_start = pad_end - padded
    blk_row = jnp.arange(n_blk, dtype=jnp.int32) * MOE_TM
    blk_e = jnp.minimum(jnp.sum((blk_row[:, None] >= pad_end[None, :]).astype(jnp.int32), axis=1),
                        N_EXPERTS - 1)
    blk_on = (blk_row < pad_end[-1]).astype(jnp.int32)
    blk_new = jnp.concatenate([jnp.ones((1,), jnp.int32), (blk_e[1:] != blk_e[:-1]).astype(jnp.int32)])
    blk_slot = (jnp.cumsum(blk_new) - 1) % 2
    experts = jnp.arange(N_EXPERTS, dtype=jnp.int32)
    in_use = jnp.where(counts > 0, experts, N_EXPERTS)
    after = jnp.concatenate([lax.cummin(in_use[::-1])[::-1][1:], jnp.full((1,), N_EXPERTS, jnp.int32)])
    blk_next = jnp.where(after[blk_e] < N_EXPERTS, after[blk_e], -1)

    def slots(ei):
        expert, rank = ei[:, 0:TOP_K], ei[:, TOP_K:2 * TOP_K]
        start = jnp.sum(jnp.where(expert[:, :, None] == jnp.arange(N_EXPERTS, dtype=jnp.int32),
                                  pad_start, 0), axis=-1)
        return (start + rank).reshape(-1)

    tail = n_blk - n_asg // MOE_TM
    last_on = (counts % MOE_TM != 0).astype(jnp.int32)
    tail_blk = pad_end[-1] // MOE_TM + jnp.arange(tail, dtype=jnp.int32)
    tail_on = (tail_blk < n_blk).astype(jnp.int32)
    zero_blk = jnp.concatenate([(pad_end // MOE_TM - 1) * last_on, tail_blk * tail_on])
    zero_on = jnp.concatenate([last_on, tail_on])

    dest_p, dest_s = slots(ei_p), slots(ei_s)
    x_buf = _dispatch(jnp.concatenate([dest_p, dest_s]), zero_blk, zero_on, hn_p, hn_s, n_pad)
    y_buf = _moe_blocks(x_buf, (blk_e, blk_new, blk_on, blk_slot.astype(jnp.int32), blk_next.astype(jnp.int32)),
                        layer, w_gu, b_gu, w_dn, b_dn)
    return dest_p, dest_s, y_buf


def _final_kernel(dest_cur, dest_nxt, x_ref, pr_ref, ybuf, mod_ref, g_ref, o_ref, rows, sems):
    x = x_ref[...] + mod_ref[5:6, :] * _combine_experts(dest_cur, dest_nxt, pr_ref, ybuf, rows, sems)
    o_ref[...] = _rmsnorm(x, g_ref[...])


def _final_norm(x, route, mod, g, seq):
    n = x.shape[0]
    tm = min(ROW_TILE, seq)
    row = pl.BlockSpec((tm, D_MODEL), lambda i: (i, 0))
    args, specs, scratch = _combine_operands(*route, n, tm)
    return pl.pallas_call(
        _final_kernel,
        grid=(n // tm,),
        in_specs=specs[:2] + [row] + specs[2:] + [_mod_spec(mod, seq, tm),
                                                  pl.BlockSpec((1, D_MODEL), lambda i: (0, 0))],
        out_specs=row,
        out_shape=jax.ShapeDtypeStruct((n, D_MODEL), F32),
        scratch_shapes=scratch,
        compiler_params=pltpu.CompilerParams(dimension_semantics=("arbitrary",), vmem_limit_bytes=VMEM_LIMIT),
        name="final_norm",
    )(*args[:2], x, *args[2:], mod, g.reshape(1, D_MODEL))


def _block_diag(w):
    eye = jnp.eye(HEADS, dtype=w.dtype)
    return (eye[:, None, :, None] * w[:, :, None, :]).reshape(W_GRP, W_GRP)


def _layer_weights(l, p):
    gate_w = jnp.concatenate([_block_diag(p["lru_wr"][l, 0]), _block_diag(p["lru_wi"][l, 0]),
                              _block_diag(p["lru_wr"][l, 1]), _block_diag(p["lru_wi"][l, 1])], axis=1)
    gate_b = jnp.concatenate([p["lru_br"][l, 0], p["lru_bi"][l, 0], p["lru_br"][l, 1], p["lru_bi"][l, 1]])
    head_of = jnp.arange(W_GRP) // HD
    row = lambda v: v.reshape(1, W_GRP)
    return dict(
        conv_a_w=p["conv_a_w"][l], conv_a_b=row(p["conv_a_b"][l]),
        norm_a_g=row(p["norm_a_g"][l]), norm_a_b=row(p["norm_a_b"][l]),
        conv_b_w=p["conv_b_w"][l], conv_b_b=row(p["conv_b_b"][l]),
        gate_w=gate_w.astype(BF16), gate_b=gate_b.reshape(1, 4 * W_GRP), lru_lam=p["lru_lam"][l],
        sgu_norm_g=row(p["sgu_norm_g"][l]), sgu_norm_b=row(p["sgu_norm_b"][l]),
        sgu_w=p["sgu_w"][l].astype(BF16), sgu_bias=jnp.repeat(p["sgu_b"][l].T, HD, axis=1),
        conv_d_w=p["conv_d_w"][l],
        m_head=((head_of[:, None] == head_of[None, :]).astype(F32) / HD).astype(BF16),
    )


def kernel(x_prompt, x_sample, state_rglru, c, c_ctx, w_ada, b_ada, norm1_g, norm2_g, w_in, conv_a_w,
           conv_a_b, norm_a_g, norm_a_b, conv_b_w, conv_b_b, lru_wr, lru_br, lru_wi, lru_bi, lru_lam,
           sgu_norm_g, sgu_norm_b, sgu_w, sgu_b, conv_d_w, w_out, w_router, b_router, w_gu, b_gu, w_dn,
           b_dn, final_g):
    p = dict(conv_a_w=conv_a_w, conv_a_b=conv_a_b, norm_a_g=norm_a_g, norm_a_b=norm_a_b,
             conv_b_w=conv_b_w, conv_b_b=conv_b_b, lru_wr=lru_wr, lru_br=lru_br, lru_wi=lru_wi,
             lru_bi=lru_bi, lru_lam=lru_lam, sgu_norm_g=sgu_norm_g, sgu_norm_b=sgu_norm_b,
             sgu_w=sgu_w, sgu_b=sgu_b, conv_d_w=conv_d_w)
    bp, tp, _ = x_prompt.shape
    bs, ts, _ = x_sample.shape
    n_p, n_s = bp * tp, bs * ts

    cond_rows = jnp.zeros((COND_ROWS, D_MODEL), F32).at[0].set(c_ctx).at[1:1 + bs].set(c)
    mod = _ada_mod(cond_rows, w_ada, b_ada).reshape(DEPTH, COND_ROWS, 6, D_MODEL)

    xp = x_prompt.reshape(n_p, D_MODEL)
    xs = x_sample.reshape(n_s, D_MODEL)
    h0_ctx = jnp.zeros((bp, 2, W_GRP), F32)
    route_p = route_s = mod_p_prev = mod_s_prev = None
    no_tokens_seen = jnp.zeros((1, N_EXPERTS), F32)
    states = []
    for l in range(DEPTH):
        lw = _layer_weights(l, p)
        mod_p, mod_s = mod[l, 0:1], mod[l, 1:1 + bs]
        w_in_l = w_in[l].astype(BF16)
        w_out_l = w_out[l].astype(BF16)
        xp, proj_p = _in_proj(xp, route_p, mod_p_prev, mod_p, norm1_g[l], w_in_l, tp)
        xs, proj_s = _in_proj(xs, route_s, mod_s_prev, mod_s, norm1_g[l], w_in_l, ts)
        y_p, st = _mixers(proj_p, h0_ctx, lw, bp, tp, False)
        y_s, _ = _mixers(proj_s, state_rglru[:, l], lw, bs, ts, True)
        states.append(st)
        xp, hn_p, ei_p, pr_p, seen = _out_proj(y_p, xp, mod_p, norm2_g[l], w_out_l, w_router[l],
                                               b_router[l], no_tokens_seen, tp)
        xs, hn_s, ei_s, pr_s, counts = _out_proj(y_s, xs, mod_s, norm2_g[l], w_out_l, w_router[l],
                                                 b_router[l], seen, ts)
        dest_p, dest_s, y_buf = _moe(hn_p, hn_s, ei_p, ei_s, counts, l, w_gu, b_gu[l], w_dn, b_dn[l])
        route_p, route_s = (dest_p, pr_p, y_buf), (dest_s, pr_s, y_buf)
        mod_p_prev, mod_s_prev = mod_p, mod_s
    y_prompt = _final_norm(xp, route_p, mod_p_prev, final_g, tp).reshape(bp, tp, D_MODEL)
    y_sample = _final_norm(xs, route_s, mod_s_prev, final_g, ts).reshape(bs, ts, D_MODEL)
    return y_prompt, y_sample, jnp.stack(states, axis=1)
```

```python
import functools

import jax
import jax.numpy as jnp
from jax import lax
from jax.experimental import pallas as pl
from jax.experimental.pallas import tpu as pltpu

F32 = jnp.float32
BF16 = jnp.bfloat16

D_MODEL = 1024
DEPTH = 2
GRID_W = 64
N_MIXERS = 4
W_GRP = D_MODEL // N_MIXERS
HEADS = 4
HD = W_GRP // HEADS
N_IN = 9 * W_GRP
CONV_A = 31
CONV_B = 4
CONV_D = 3
CHUNK = 128
LRU_C = 8.0
N_EXPERTS = 32
TOP_K = 4
D_FF = D_MODEL
SWIGLU_LIMIT = 7.0
SWIGLU_ALPHA = 1.702
EPS = 1e-6

V7X_SUBLANES = 8
V7X_LANES = 128
TOKEN_ROWS = D_MODEL // V7X_LANES
DMA_PRIORITIES = 2
V7X_VMEM_BYTES = 64 * 1024 * 1024
VMEM_LIMIT = V7X_VMEM_BYTES * 7 // 8

COND_ROWS = 16
ADA_TN = 1536
ROW_TILE = 512
CONV_ROWS = GRID_W
CONV_PAD = 16
NORM_ROWS = 256
MOE_TM = 512
DISPATCH_TM = 512


def _rmsnorm(x, g):
    return x * lax.rsqrt(jnp.mean(x * x, axis=-1, keepdims=True) + EPS) * g


def _group_mean(x, m_ref):
    hi = x.astype(BF16)
    lo = (x - hi.astype(F32)).astype(BF16)
    m = m_ref[...]
    return (jnp.dot(hi, m, preferred_element_type=F32) + jnp.dot(lo, m, preferred_element_type=F32))


def _layernorm(x, g, b, m_ref):
    xc = x - _group_mean(x, m_ref)
    var = _group_mean(xc * xc, m_ref)
    return xc * lax.rsqrt(var + EPS) * g + b


def _ada_kernel(c_ref, w_ref, b_ref, o_ref):
    c = c_ref[...]
    cond = (c * jax.nn.sigmoid(c)).astype(BF16)
    o_ref[...] = jnp.dot(cond, w_ref[...].astype(BF16), preferred_element_type=F32) + b_ref[...]


def _ada_mod(cond_rows, w_ada, b_ada):
    n_col = w_ada.shape[-1]
    return pl.pallas_call(
        _ada_kernel,
        grid=(DEPTH, n_col // ADA_TN),
        in_specs=[
            pl.BlockSpec((COND_ROWS, D_MODEL), lambda l, j: (0, 0)),
            pl.BlockSpec((None, D_MODEL, ADA_TN), lambda l, j: (l, 0, j)),
            pl.BlockSpec((None, 1, ADA_TN), lambda l, j: (l, 0, j)),
        ],
        out_specs=pl.BlockSpec((None, COND_ROWS, ADA_TN), lambda l, j: (l, 0, j)),
        out_shape=jax.ShapeDtypeStruct((DEPTH, COND_ROWS, n_col), F32),
        name="ada_mod",
    )(cond_rows, w_ada, b_ada.reshape(DEPTH, 1, n_col))


def _token_rows(t, count=1):
    return pl.ds(pl.multiple_of(t * TOKEN_ROWS, TOKEN_ROWS), count * TOKEN_ROWS)


def _store_token_tiles(ref, x):
    n = x.shape[0]
    for c in range(TOKEN_ROWS):
        ref[pl.ds(c, n, stride=TOKEN_ROWS), :] = x[:, c * V7X_LANES:(c + 1) * V7X_LANES]


def _load_token_tiles(ref, n):
    return [ref[pl.ds(c, n, stride=TOKEN_ROWS), :] for c in range(TOKEN_ROWS)]


def _combine_experts(dest_cur, dest_nxt, pr_ref, ybuf, rows, sems):
    i = pl.program_id(0)
    n_steps = pl.num_programs(0)
    tm = rows.shape[2] // TOKEN_ROWS
    slot = i % 2

    def row_copy(dref, j, k, slot_):
        return pltpu.make_async_copy(ybuf.at[_token_rows(dref[k, j])],
                                     rows.at[slot_, k, _token_rows(j)], sems.at[slot_])

    def issue(dref, slot_):
        def body(j, carry):
            for k in range(TOP_K):
                row_copy(dref, j, k, slot_).start(priority=k % DMA_PRIORITIES)
            return carry

        lax.fori_loop(0, tm, body, 0, unroll=4)

    @pl.when(i == 0)
    def _():
        issue(dest_cur, 0)

    @pl.when(i + 1 < n_steps)
    def _():
        issue(dest_nxt, 1 - slot)

    for k in range(TOP_K):
        pltpu.make_async_copy(ybuf.at[_token_rows(0, tm)], rows.at[slot, k], sems.at[slot]).wait()
    experts = [_load_token_tiles(rows.at[slot, k], tm) for k in range(TOP_K)]
    pad = jnp.zeros((V7X_LANES - pr_ref.shape[0], tm), F32)
    probs = jnp.concatenate([pr_ref[...], pad], axis=0).T
    chunks = []
    for c in range(TOKEN_ROWS):
        acc = probs[:, 0:1] * experts[0][c]
        for k in range(1, TOP_K):
            acc = acc + probs[:, k:k + 1] * experts[k][c]
        chunks.append(acc)
    return jnp.concatenate(chunks, axis=1)


def _inproj_kernel(*refs, has_res):
    if has_res:
        (dest_cur, dest_nxt, x_ref, pr_ref, ybuf, modp_ref, mod_ref, g_ref, w_ref,
         xo_ref, p_ref, rows, sems) = refs
        x = x_ref[...] + modp_ref[5:6, :] * _combine_experts(dest_cur, dest_nxt, pr_ref, ybuf, rows, sems)
        xo_ref[...] = x
    else:
        x_ref, mod_ref, g_ref, w_ref, p_ref = refs
        x = x_ref[...]
    hn = _rmsnorm(x, g_ref[...]) * (1.0 + mod_ref[1:2, :]) + mod_ref[0:1, :]
    p_ref[...] = jnp.dot(hn.astype(BF16), w_ref[...], preferred_element_type=F32)


def _mod_spec(mod, seq, tm):
    if mod.shape[0] == 1:
        return pl.BlockSpec((None, 6, D_MODEL), lambda i: (0, 0, 0))
    return pl.BlockSpec((None, 6, D_MODEL), lambda i: ((i * tm) // seq, 0, 0))


def _per_tile(dest, tm):
    return dest.reshape(TOP_K, dest.shape[1] // tm, tm).transpose(1, 0, 2)


def _combine_operands(dest, probs, y_buf, n, tm):
    n_steps = n // tm
    dest3 = _per_tile(dest, tm)
    smem = functools.partial(pl.BlockSpec, (None, TOP_K, tm), memory_space=pltpu.SMEM)
    args = [dest3, dest3, probs, y_buf]
    specs = [smem(lambda i: (i, 0, 0)),
             smem(lambda i: (jnp.minimum(i + 1, n_steps - 1), 0, 0)),
             pl.BlockSpec((2 * TOP_K, tm), lambda i: (0, i)),
             pl.BlockSpec(memory_space=pl.ANY)]
    scratch = [pltpu.VMEM((2, TOP_K, tm * TOKEN_ROWS, V7X_LANES), F32), pltpu.SemaphoreType.DMA((2,))]
    return args, specs, scratch


def _in_proj(x, route, mod_prev, mod, g, w_bf16, seq):
    n = x.shape[0]
    tm = min(ROW_TILE, seq)
    row = pl.BlockSpec((tm, D_MODEL), lambda i: (i, 0))
    has_res = route is not None
    args, specs, scratch = [], [], []
    if has_res:
        args, specs, scratch = _combine_operands(*route, n, tm)
        args = args[:2] + [x] + args[2:] + [mod_prev]
        specs = specs[:2] + [row] + specs[2:] + [_mod_spec(mod_prev, seq, tm)]
    else:
        args, specs = [x], [row]
    args += [mod, g.reshape(1, D_MODEL), w_bf16]
    specs += [_mod_spec(mod, seq, tm),
              pl.BlockSpec((1, D_MODEL), lambda i: (0, 0)),
              pl.BlockSpec((D_MODEL, N_IN), lambda i: (0, 0))]
    proj_shape = jax.ShapeDtypeStruct((n, N_IN), F32)
    proj_spec = pl.BlockSpec((tm, N_IN), lambda i: (i, 0))
    if has_res:
        out_shape = (jax.ShapeDtypeStruct((n, D_MODEL), F32), proj_shape)
        out_specs = (row, proj_spec)
    else:
        out_shape, out_specs = proj_shape, proj_spec
    out = pl.pallas_call(
        functools.partial(_inproj_kernel, has_res=has_res),
        grid=(n // tm,),
        in_specs=specs,
        out_specs=out_specs,
        out_shape=out_shape,
        scratch_shapes=scratch,
        compiler_params=pltpu.CompilerParams(dimension_semantics=("arbitrary",), vmem_limit_bytes=VMEM_LIMIT),
        name="in_proj",
    )(*args)
    return out if has_res else (x, out)


def _group_scan(a, b, reverse):
    shape = a.shape
    grouped = (shape[0] // V7X_SUBLANES, V7X_SUBLANES, shape[1])
    a, b = a.reshape(grouped), b.reshape(grouped)
    ri = lax.broadcasted_iota(jnp.int32, grouped, 1)
    for d in (1, 2, 4):
        shift = V7X_SUBLANES - d if reverse else d
        keep = ri < V7X_SUBLANES - d if reverse else ri >= d
        ra, rb = pltpu.roll(a, shift, 1), pltpu.roll(b, shift, 1)
        b = a * jnp.where(keep, rb, 0.0) + b
        a = a * jnp.where(keep, ra, 1.0)
    return a.reshape(shape), b.reshape(shape)


def _mixer_kernel(s0, s1, s2, caw, cab, nag, nab, cbw, cbb, wg, bg, lam, h0, sng, snb, sw, sbias, cdw,
                  m_head, y_ref, st_ref, pad_s, af_s, bf_s, ab_s, bb_s, *, seq, on_grid):
    mixer = pl.program_id(1)
    n_conv = seq // CONV_ROWS
    win_rows = CONV_ROWS + 2 * CONV_PAD

    def rows_at(c, size):
        return pl.ds(pl.multiple_of(c * size, size), size)

    def fill_padded(fn):
        zeros = jnp.zeros((CONV_PAD, W_GRP), F32)
        pad_s[0:CONV_PAD, :] = zeros
        pad_s[CONV_PAD + seq:2 * CONV_PAD + seq, :] = zeros

        def body(c, carry):
            dst = pl.ds(pl.multiple_of(c * CONV_ROWS + CONV_PAD, V7X_SUBLANES), CONV_ROWS)
            pad_s[dst, :] = fn(rows_at(c, CONV_ROWS))
            return carry

        lax.fori_loop(0, n_conv, body, 0)

    def conv_window(win, w_ref, taps, pad_l):
        acc = jnp.zeros((CONV_ROWS, W_GRP), F32)
        for mis in range(V7X_SUBLANES):
            starts = [(k, CONV_PAD - pad_l + k) for k in range(taps)
                      if (CONV_PAD - pad_l + k) % V7X_SUBLANES == mis]
            if not starts:
                continue
            shifted = pltpu.roll(win, win_rows - mis, 0) if mis else win
            for k, start in starts:
                acc = acc + w_ref[k:k + 1, :] * shifted[start - mis:start - mis + CONV_ROWS, :]
        return acc

    def conv_rows(c, w_ref, taps, pad_l):
        win = pad_s[pl.ds(pl.multiple_of(c * CONV_ROWS, CONV_ROWS), win_rows), :]
        return conv_window(win, w_ref, taps, pad_l)

    @pl.when(mixer == 0)
    def _conformer():
        def glu(rows):
            return s0[rows, :] * jax.nn.sigmoid(s1[rows, :])

        if not on_grid:
            fill_padded(glu)

        def body(c, carry):
            if on_grid:
                edge = jnp.zeros((CONV_PAD, W_GRP), F32)
                win = jnp.concatenate([edge, glu(rows_at(c, CONV_ROWS)), edge], axis=0)
                z = conv_window(win, caw, CONV_A, CONV_A // 2)
            else:
                z = conv_rows(c, caw, CONV_A, CONV_A // 2)
            af_s[rows_at(c, CONV_ROWS), :] = z + cab[...]
            return carry

        lax.fori_loop(0, n_conv, body, 0)

        def norm(c, carry):
            rows = rows_at(c, NORM_ROWS)
            z = _layernorm(af_s[rows, :], nag[...], nab[...], m_head)
            y_ref[rows, :] = (z * jax.nn.sigmoid(z)).astype(BF16)
            return carry

        lax.fori_loop(0, seq // NORM_ROWS, norm, 0, unroll=2)

    @pl.when(mixer == 1)
    def _rglru():
        fill_padded(lambda rows: s0[rows, :])
        lam_v = lam[...]
        softplus_neg = jnp.maximum(-lam_v, 0.0) + jnp.log1p(jnp.exp(-jnp.abs(lam_v)))

        def gates(c, carry):
            rows = rows_at(c, CONV_ROWS)
            xc = conv_rows(c, cbw, CONV_B, 2) + cbb[...]
            gt = jax.nn.sigmoid(jnp.dot(xc.astype(BF16), wg[...], preferred_element_type=F32) + bg[...])
            for d, (a_s, b_s) in enumerate(((af_s, bf_s), (ab_s, bb_s))):
                r_gate = gt[:, (2 * d) * W_GRP:(2 * d + 1) * W_GRP]
                i_gate = gt[:, (2 * d + 1) * W_GRP:(2 * d + 2) * W_GRP]
                log_a = -LRU_C * r_gate * softplus_neg[d:d + 1, :]
                a = jnp.exp(log_a)
                b = jnp.sqrt(jnp.maximum(-jnp.tanh(log_a) * (a * a + 1.0), 0.0)) * (i_gate * xc)
                a, b = _group_scan(a, b, reverse=(d == 1))
                a_s[rows, :] = a
                b_s[rows, :] = b
            return carry

        lax.fori_loop(0, n_conv, gates, 0, unroll=2)

        n_grp = seq // V7X_SUBLANES

        def chain(g, carry):
            cf, cb = carry
            rf = rows_at(g, V7X_SUBLANES)
            rb = rows_at(n_grp - 1 - g, V7X_SUBLANES)
            hf = af_s[rf, :] * cf + bf_s[rf, :]
            bf_s[rf, :] = hf
            hb = ab_s[rb, :] * cb + bb_s[rb, :]
            bb_s[rb, :] = hb
            cf = jnp.broadcast_to(hf[V7X_SUBLANES - 1:V7X_SUBLANES, :], (V7X_SUBLANES, W_GRP))
            cb = jnp.broadcast_to(hb[0:1, :], (V7X_SUBLANES, W_GRP))
            return cf, cb

        init = (jnp.broadcast_to(h0[0:1, :], (V7X_SUBLANES, W_GRP)),
                jnp.broadcast_to(h0[1:2, :], (V7X_SUBLANES, W_GRP)))
        cf, cb = lax.fori_loop(0, n_grp, chain, init, unroll=4)
        st_ref[0:1, :] = cf[0:1, :]
        st_ref[1:2, :] = cb[0:1, :]

        def out(c, carry):
            rows = rows_at(c, CONV_ROWS)
            y_ref[rows, :] = (jax.nn.gelu(s1[rows, :]) * (bf_s[rows, :] + bb_s[rows, :])).astype(BF16)
            return carry

        lax.fori_loop(0, n_conv, out, 0)

    @pl.when(mixer == 2)
    def _sgu():
        lane = lax.broadcasted_iota(jnp.int32, (CHUNK, W_GRP), 1)

        def body(n, carry):
            rows = rows_at(n, CHUNK)
            v = s1[rows, :]
            vc = v - jnp.mean(v, axis=-1, keepdims=True)
            var = jnp.mean(vc * vc, axis=-1, keepdims=True)
            v = (vc * lax.rsqrt(var + EPS) * sng[...] + snb[...]).astype(BF16)
            s = sbias[...]
            for h in range(HEADS):
                sh = jnp.dot(sw[h], v, preferred_element_type=F32)
                s = s + jnp.where((lane >= h * HD) & (lane < (h + 1) * HD), sh, 0.0)
            y_ref[rows, :] = (s0[rows, :] * s).astype(BF16)
            return carry

        lax.fori_loop(0, seq // CHUNK, body, 0, unroll=4)

    @pl.when(mixer == 3)
    def _gated_conv():
        fill_padded(lambda rows: s1[rows, :] * s2[rows, :])

        def body(c, carry):
            rows = rows_at(c, CONV_ROWS)
            y_ref[rows, :] = (s0[rows, :] * conv_rows(c, cdw, CONV_D, CONV_D // 2)).astype(BF16)
            return carry

        lax.fori_loop(0, n_conv, body, 0)


def _mixers(proj, h0, lw, batch, seq, on_grid):
    proj3 = proj.reshape(batch, seq, N_IN)
    slab = (None, seq, W_GRP)

    def const(shape):
        return pl.BlockSpec(shape, lambda b, m: (0,) * len(shape))

    in_specs = [
        pl.BlockSpec(slab, lambda b, m: (b, 0, 2 * m)),
        pl.BlockSpec(slab, lambda b, m: (b, 0, 2 * m + 1)),
        pl.BlockSpec(slab, lambda b, m: (b, 0, 8)),
        const((CONV_A, W_GRP)), const((1, W_GRP)), const((1, W_GRP)), const((1, W_GRP)),
        const((CONV_B, W_GRP)), const((1, W_GRP)),
        const((W_GRP, 4 * W_GRP)), const((1, 4 * W_GRP)), const((2, W_GRP)),
        pl.BlockSpec((None, 2, W_GRP), lambda b, m: (b, 0, 0)),
        const((1, W_GRP)), const((1, W_GRP)), const((HEADS, CHUNK, CHUNK)), const((CHUNK, W_GRP)),
        const((CONV_D, W_GRP)), const((W_GRP, W_GRP)),
    ]
    y, st = pl.pallas_call(
        functools.partial(_mixer_kernel, seq=seq, on_grid=on_grid),
        grid=(batch, N_MIXERS),
        in_specs=in_specs,
        out_specs=(pl.BlockSpec(slab, lambda b, m: (b, 0, m)),
                   pl.BlockSpec((None, 2, W_GRP), lambda b, m: (b, 0, 0))),
        out_shape=(jax.ShapeDtypeStruct((batch, seq, D_MODEL), BF16),
                   jax.ShapeDtypeStruct((batch, 2, W_GRP), F32)),
        scratch_shapes=[pltpu.VMEM((seq + 2 * CONV_PAD, W_GRP), F32)] + [pltpu.VMEM((seq, W_GRP), F32)] * 4,
        compiler_params=pltpu.CompilerParams(dimension_semantics=("arbitrary", "arbitrary"),
                                             vmem_limit_bytes=VMEM_LIMIT),
        name="mixers",
    )(proj3, proj3, proj3, lw["conv_a_w"], lw["conv_a_b"], lw["norm_a_g"], lw["norm_a_b"],
      lw["conv_b_w"], lw["conv_b_b"], lw["gate_w"], lw["gate_b"], lw["lru_lam"], h0,
      lw["sgu_norm_g"], lw["sgu_norm_b"], lw["sgu_w"], lw["sgu_bias"], lw["conv_d_w"],
      lw["m_head"])
    return y.reshape(batch * seq, D_MODEL), st


def _outproj_kernel(y_ref, x_ref, mod_ref, g_ref, wo_ref, wr_ref, br_ref, tri_ref, cin_ref,
                    xm_ref, hn_ref, ei_ref, pr_ref, cnt_ref, seen_s):
    tm = x_ref.shape[0]
    e = N_EXPERTS

    @pl.when(pl.program_id(0) == 0)
    def _():
        seen_s[...] = cin_ref[...]

    y = jnp.dot(y_ref[...], wo_ref[...], preferred_element_type=F32)
    x = x_ref[...] + mod_ref[2:3, :] * y
    xm_ref[...] = x
    hn = _rmsnorm(x, g_ref[...]) * (1.0 + mod_ref[4:5, :]) + mod_ref[3:4, :]
    _store_token_tiles(hn_ref, hn)

    h1 = hn.astype(BF16)
    r1 = hn - h1.astype(F32)
    h2 = r1.astype(BF16)
    h3 = (r1 - h2.astype(F32)).astype(BF16)
    w_t = wr_ref[...]
    contract_features = (((1,), (1,)), ((), ()))
    p1 = lax.dot_general(w_t, h1, contract_features, preferred_element_type=F32)
    p2 = lax.dot_general(w_t, h2, contract_features, preferred_element_type=F32)
    p3 = lax.dot_general(w_t, h3, contract_features, preferred_element_type=F32)
    logits = (p1[2 * e:3 * e] + p2[e:2 * e] + p3[0:e] + p1[e:2 * e] + p2[0:e] + p1[0:e]) + br_ref[...]

    expert = lax.broadcasted_iota(jnp.int32, (e, tm), 0)
    beats = jnp.zeros((e, tm), F32)
    for other in range(e):
        lo = logits[other:other + 1, :]
        ahead = (lo > logits) | ((lo == logits) & (expert > other))
        beats = beats + jnp.where(ahead, 1.0, 0.0)

    chosen = jnp.where(beats < float(TOP_K), 1.0, 0.0)
    earlier = jnp.dot(chosen.astype(BF16), tri_ref[...], preferred_element_type=F32) + seen_s[...]
    seen_s[...] = earlier[:, tm - 1:tm] + chosen[:, tm - 1:tm]
    cnt_ref[...] = seen_s[...]

    def pick(k, values):
        return jnp.sum(jnp.where(beats == float(k), values, 0.0), axis=0, keepdims=True)

    expert_f = expert.astype(F32)
    vals = [pick(k, logits) for k in range(TOP_K)]
    exps = [jnp.exp(v - vals[0]) for v in vals]
    den = exps[0] + exps[1] + exps[2] + exps[3]
    out_row = lax.broadcasted_iota(jnp.int32, (2 * TOP_K, tm), 0)
    ei = jnp.zeros((2 * TOP_K, tm), F32)
    pr = jnp.zeros((2 * TOP_K, tm), F32)
    for k in range(TOP_K):
        ei = jnp.where(out_row == k, pick(k, expert_f), ei)
        ei = jnp.where(out_row == TOP_K + k, pick(k, earlier), ei)
        pr = jnp.where(out_row == k, exps[k] / den, pr)
    ei_ref[...] = ei.astype(jnp.int32)
    pr_ref[...] = pr


def _out_proj(y, x, mod, g, wo_bf16, w_router, b_router, seen, seq):
    n = x.shape[0]
    tm = min(ROW_TILE, seq)
    row = pl.BlockSpec((tm, D_MODEL), lambda i: (i, 0))
    route = pl.BlockSpec((2 * TOP_K, tm), lambda i: (0, i))
    cnt = pl.BlockSpec((N_EXPERTS, 1), lambda i: (0, 0))
    before = jnp.tri(tm, k=-1, dtype=BF16).T
    w1 = w_router.astype(BF16)
    w2 = (w_router - w1.astype(F32)).astype(BF16)
    w3 = (w_router - w1.astype(F32) - w2.astype(F32)).astype(BF16)
    w_t = jnp.concatenate([w1.T, w2.T, w3.T, jnp.zeros_like(w1.T)], axis=0)
    return pl.pallas_call(
        _outproj_kernel,
        grid=(n // tm,),
        in_specs=[row, row, _mod_spec(mod, seq, tm),
                  pl.BlockSpec((1, D_MODEL), lambda i: (0, 0)),
                  pl.BlockSpec((D_MODEL, D_MODEL), lambda i: (0, 0)),
                  pl.BlockSpec((4 * N_EXPERTS, D_MODEL), lambda i: (0, 0)),
                  cnt, pl.BlockSpec((tm, tm), lambda i: (0, 0)), cnt],
        out_specs=(row, pl.BlockSpec((tm * TOKEN_ROWS, V7X_LANES), lambda i: (i, 0)), route, route, cnt),
        out_shape=(jax.ShapeDtypeStruct((n, D_MODEL), F32),
                   jax.ShapeDtypeStruct((n * TOKEN_ROWS, V7X_LANES), F32),
                   jax.ShapeDtypeStruct((2 * TOP_K, n), jnp.int32),
                   jax.ShapeDtypeStruct((2 * TOP_K, n), F32),
                   jax.ShapeDtypeStruct((N_EXPERTS, 1), F32)),
        scratch_shapes=[pltpu.VMEM((N_EXPERTS, 1), F32)],
        compiler_params=pltpu.CompilerParams(dimension_semantics=("arbitrary",)),
        name="out_proj",
    )(y, x, mod, g.reshape(1, D_MODEL), wo_bf16, w_t, b_router.reshape(N_EXPERTS, 1), before, seen)


def _dispatch_kernel(zero_blk, zero_on, dest_ref, hn_p, hn_s, xbuf, zeros_s, sem, zsem, *, tiles_p):
    i = pl.program_id(0)

    @pl.when(i == 0)
    def _():
        zeros_s[...] = jnp.zeros_like(zeros_s)

        def fill(j):
            return pltpu.make_async_copy(zeros_s, xbuf.at[_token_rows(zero_blk[j] * MOE_TM, MOE_TM)], zsem)

        for j in range(zero_blk.shape[0]):
            @pl.when(zero_on[j] == 1)
            def _():
                fill(j).start()

        for j in range(zero_blk.shape[0]):
            @pl.when(zero_on[j] == 1)
            def _():
                fill(j).wait()

    def issue(src):
        def body(j, carry):
            row = src.at[_token_rows(j)]
            for k in range(TOP_K):
                pltpu.make_async_copy(row, xbuf.at[_token_rows(dest_ref[k, j])],
                                      sem).start(priority=k % DMA_PRIORITIES)
            return carry

        lax.fori_loop(0, DISPATCH_TM, body, 0, unroll=4)

    @pl.when(i < tiles_p)
    def _():
        issue(hn_p)

    @pl.when(i >= tiles_p)
    def _():
        issue(hn_s)

    for _ in range(TOP_K):
        pltpu.make_async_copy(hn_s, xbuf.at[_token_rows(0, DISPATCH_TM)], sem).wait()


def _dispatch(dest, zero_blk, zero_on, hn_p, hn_s, n_pad):
    n_p, n_s = hn_p.shape[0] // TOKEN_ROWS, hn_s.shape[0] // TOKEN_ROWS
    tiles_p = n_p // DISPATCH_TM
    n_steps = (n_p + n_s) // DISPATCH_TM
    tile = (DISPATCH_TM * TOKEN_ROWS, V7X_LANES)
    grid_spec = pltpu.PrefetchScalarGridSpec(
        num_scalar_prefetch=2,
        grid=(n_steps,),
        in_specs=[pl.BlockSpec((None, TOP_K, DISPATCH_TM), lambda i, zb, zo: (i, 0, 0),
                               memory_space=pltpu.SMEM),
                  pl.BlockSpec(tile, lambda i, zb, zo: (jnp.minimum(i, tiles_p - 1), 0)),
                  pl.BlockSpec(tile, lambda i, zb, zo: (jnp.maximum(i - tiles_p, 0), 0))],
        out_specs=pl.BlockSpec(memory_space=pl.ANY),
        scratch_shapes=[pltpu.VMEM((MOE_TM * TOKEN_ROWS, V7X_LANES), F32), pltpu.SemaphoreType.DMA(()),
                        pltpu.SemaphoreType.DMA(())],
    )
    return pl.pallas_call(
        functools.partial(_dispatch_kernel, tiles_p=tiles_p),
        grid_spec=grid_spec,
        out_shape=jax.ShapeDtypeStruct((n_pad * TOKEN_ROWS, V7X_LANES), F32),
        compiler_params=pltpu.CompilerParams(dimension_semantics=("arbitrary",)),
        name="moe_dispatch",
    )(zero_blk, zero_on, _per_tile(dest, DISPATCH_TM), hn_p, hn_s)


def _moe_kernel(blk_e, blk_new, blk_on, blk_slot, blk_next, x_ref, wgu_hbm, bgu_ref, wdn_hbm, bdn_ref, o_ref,
                wgu_f, wdn_f, wgu_s, wdn_s, sems, *, layer):
    i = pl.program_id(0)

    def fetch(expert, slot):
        return (pltpu.make_async_copy(wgu_hbm.at[layer, expert], wgu_f.at[slot], sems.at[0, slot]),
                pltpu.make_async_copy(wdn_hbm.at[layer, expert], wdn_f.at[slot], sems.at[1, slot]))

    @pl.when(i == 0)
    def _():
        for copy in fetch(blk_e[0], 0):
            copy.start()

    @pl.when(blk_on[i] == 0)
    def _():
        o_ref[...] = jnp.zeros_like(o_ref)

    @pl.when(blk_on[i] == 1)
    def _():
        @pl.when(blk_new[i] == 1)
        def _():
            slot = blk_slot[i]
            for copy in fetch(blk_e[i], slot):
                copy.wait()

            @pl.when(blk_next[i] >= 0)
            def _():
                for copy in fetch(blk_next[i], 1 - slot):
                    copy.start()

            wgu_s[...] = wgu_f[slot].astype(BF16)
            wdn_s[...] = wdn_f[slot].astype(BF16)

        x = jnp.concatenate([c.astype(BF16) for c in _load_token_tiles(x_ref, MOE_TM)], axis=1)
        gu = jnp.dot(x, wgu_s[...], preferred_element_type=F32) + bgu_ref[...]
        g = jnp.minimum(gu[:, :D_FF], SWIGLU_LIMIT)
        u = jnp.clip(gu[:, D_FF:], -SWIGLU_LIMIT, SWIGLU_LIMIT)
        act = (u + 1.0) * (g * jax.nn.sigmoid(SWIGLU_ALPHA * g))
        _store_token_tiles(o_ref, jnp.dot(act.astype(BF16), wdn_s[...], preferred_element_type=F32)
                           + bdn_ref[...])


def _moe_blocks(x_buf, blk, layer, w_gu, b_gu, w_dn, b_dn):
    n_pad = x_buf.shape[0] // TOKEN_ROWS
    n_blk = n_pad // MOE_TM
    tile = pl.BlockSpec((MOE_TM * TOKEN_ROWS, V7X_LANES), lambda i, e, *_: (i, 0))
    grid_spec = pltpu.PrefetchScalarGridSpec(
        num_scalar_prefetch=len(blk),
        grid=(n_blk,),
        in_specs=[
            tile,
            pl.BlockSpec(memory_space=pl.ANY),
            pl.BlockSpec((None, 1, 2 * D_FF), lambda i, e, *_: (e[i], 0, 0)),
            pl.BlockSpec(memory_space=pl.ANY),
            pl.BlockSpec((None, 1, D_MODEL), lambda i, e, *_: (e[i], 0, 0)),
        ],
        out_specs=tile,
        scratch_shapes=[pltpu.VMEM((2, D_MODEL, 2 * D_FF), F32), pltpu.VMEM((2, D_FF, D_MODEL), F32),
                        pltpu.VMEM((D_MODEL, 2 * D_FF), BF16), pltpu.VMEM((D_FF, D_MODEL), BF16),
                        pltpu.SemaphoreType.DMA((2, 2))],
    )
    return pl.pallas_call(
        functools.partial(_moe_kernel, layer=layer),
        grid_spec=grid_spec,
        out_shape=jax.ShapeDtypeStruct((n_pad * TOKEN_ROWS, V7X_LANES), F32),
        compiler_params=pltpu.CompilerParams(dimension_semantics=("arbitrary",),
                                             vmem_limit_bytes=VMEM_LIMIT),
        name="moe_experts",
    )(*blk, x_buf, w_gu, b_gu.reshape(N_EXPERTS, 1, 2 * D_FF), w_dn, b_dn.reshape(N_EXPERTS, 1, D_MODEL))


def _moe(hn_p, hn_s, ei_p, ei_s, counts, layer, w_gu, b_gu, w_dn, b_dn):
    n_asg = (ei_p.shape[1] + ei_s.shape[1]) * TOP_K
    n_pad = (n_asg + N_EXPERTS * (MOE_TM - 1) + MOE_TM - 1) // MOE_TM * MOE_TM
    n_blk = n_pad // MOE_TM
    counts = counts.reshape(N_EXPERTS).astype(jnp.int32)
    padded = (counts + MOE_TM - 1) // MOE_TM * MOE_TM
    pad_end = jnp.cumsum(padded)
    pad_start = pad_end - padded
    blk_row = jnp.arange(n_blk, dtype=jnp.int32) * MOE_TM
    blk_e = jnp.minimum(jnp.sum((blk_row[:, None] >= pad_end[None, :]).astype(jnp.int32), axis=1),
                        N_EXPERTS - 1)
    blk_on = (blk_row < pad_end[-1]).astype(jnp.int32)
    blk_new = jnp.concatenate([jnp.ones((1,), jnp.int32), (blk_e[1:] != blk_e[:-1]).astype(jnp.int32)])
    blk_slot = (jnp.cumsum(blk_new) - 1) % 2
    experts = jnp.arange(N_EXPERTS, dtype=jnp.int32)
    in_use = jnp.where(counts > 0, experts, N_EXPERTS)
    after = jnp.concatenate([lax.cummin(in_use[::-1])[::-1][1:], jnp.full((1,), N_EXPERTS, jnp.int32)])
    blk_next = jnp.where(after[blk_e] < N_EXPERTS, after[blk_e], -1)

    def slots(ei):
        expert, rank = ei[0:TOP_K], ei[TOP_K:2 * TOP_K]
        start = jnp.sum(jnp.where(expert[None] == jnp.arange(N_EXPERTS, dtype=jnp.int32)[:, None, None],
                                  pad_start[:, None, None], 0), axis=0)
        return start + rank

    tail = n_blk - n_asg // MOE_TM
    last_on = (counts % MOE_TM != 0).astype(jnp.int32)
    tail_blk = pad_end[-1] // MOE_TM + jnp.arange(tail, dtype=jnp.int32)
    tail_on = (tail_blk < n_blk).astype(jnp.int32)
    zero_blk = jnp.concatenate([(pad_end // MOE_TM - 1) * last_on, tail_blk * tail_on])
    zero_on = jnp.concatenate([last_on, tail_on])

    dest_p, dest_s = slots(ei_p), slots(ei_s)
    x_buf = _dispatch(jnp.concatenate([dest_p, dest_s], axis=1), zero_blk, zero_on, hn_p, hn_s, n_pad)
    y_buf = _moe_blocks(x_buf, (blk_e, blk_new, blk_on, blk_slot.astype(jnp.int32), blk_next.astype(jnp.int32)),
                        layer, w_gu, b_gu, w_dn, b_dn)
    return dest_p, dest_s, y_buf


def _final_kernel(dest_cur, dest_nxt, x_ref, pr_ref, ybuf, mod_ref, g_ref, o_ref, rows, sems):
    x = x_ref[...] + mod_ref[5:6, :] * _combine_experts(dest_cur, dest_nxt, pr_ref, ybuf, rows, sems)
    o_ref[...] = _rmsnorm(x, g_ref[...])


def _final_norm(x, route, mod, g, seq):
    n = x.shape[0]
    tm = min(ROW_TILE, seq)
    row = pl.BlockSpec((tm, D_MODEL), lambda i: (i, 0))
    args, specs, scratch = _combine_operands(*route, n, tm)
    return pl.pallas_call(
        _final_kernel,
        grid=(n // tm,),
        in_specs=specs[:2] + [row] + specs[2:] + [_mod_spec(mod, seq, tm),
                                                  pl.BlockSpec((1, D_MODEL), lambda i: (0, 0))],
        out_specs=row,
        out_shape=jax.ShapeDtypeStruct((n, D_MODEL), F32),
        scratch_shapes=scratch,
        compiler_params=pltpu.CompilerParams(dimension_semantics=("arbitrary",), vmem_limit_bytes=VMEM_LIMIT),
        name="final_norm",
    )(*args[:2], x, *args[2:], mod, g.reshape(1, D_MODEL))


def _block_diag(w):
    eye = jnp.eye(HEADS, dtype=w.dtype)
    return (eye[:, None, :, None] * w[:, :, None, :]).reshape(W_GRP, W_GRP)


def _layer_weights(l, p):
    gate_w = jnp.concatenate([_block_diag(p["lru_wr"][l, 0]), _block_diag(p["lru_wi"][l, 0]),
                              _block_diag(p["lru_wr"][l, 1]), _block_diag(p["lru_wi"][l, 1])], axis=1)
    gate_b = jnp.concatenate([p["lru_br"][l, 0], p["lru_bi"][l, 0], p["lru_br"][l, 1], p["lru_bi"][l, 1]])
    head_of = jnp.arange(W_GRP) // HD
    row = lambda v: v.reshape(1, W_GRP)
    return dict(
        conv_a_w=p["conv_a_w"][l], conv_a_b=row(p["conv_a_b"][l]),
        norm_a_g=row(p["norm_a_g"][l]), norm_a_b=row(p["norm_a_b"][l]),
        conv_b_w=p["conv_b_w"][l], conv_b_b=row(p["conv_b_b"][l]),
        gate_w=gate_w.astype(BF16), gate_b=gate_b.reshape(1, 4 * W_GRP), lru_lam=p["lru_lam"][l],
        sgu_norm_g=row(p["sgu_norm_g"][l]), sgu_norm_b=row(p["sgu_norm_b"][l]),
        sgu_w=p["sgu_w"][l].astype(BF16), sgu_bias=jnp.repeat(p["sgu_b"][l].T, HD, axis=1),
        conv_d_w=p["conv_d_w"][l],
        m_head=((head_of[:, None] == head_of[None, :]).astype(F32) / HD).astype(BF16),
    )


def kernel(x_prompt, x_sample, state_rglru, c, c_ctx, w_ada, b_ada, norm1_g, norm2_g, w_in, conv_a_w,
           conv_a_b, norm_a_g, norm_a_b, conv_b_w, conv_b_b, lru_wr, lru_br, lru_wi, lru_bi, lru_lam,
           sgu_norm_g, sgu_norm_b, sgu_w, sgu_b, conv_d_w, w_out, w_router, b_router, w_gu, b_gu, w_dn,
           b_dn, final_g):
    p = dict(conv_a_w=conv_a_w, conv_a_b=conv_a_b, norm_a_g=norm_a_g, norm_a_b=norm_a_b,
             conv_b_w=conv_b_w, conv_b_b=conv_b_b, lru_wr=lru_wr, lru_br=lru_br, lru_wi=lru_wi,
             lru_bi=lru_bi, lru_lam=lru_lam, sgu_norm_g=sgu_norm_g, sgu_norm_b=sgu_norm_b,
             sgu_w=sgu_w, sgu_b=sgu_b, conv_d_w=conv_d_w)
    bp, tp, _ = x_prompt.shape
    bs, ts, _ = x_sample.shape
    n_p, n_s = bp * tp, bs * ts

    cond_rows = jnp.zeros((COND_ROWS, D_MODEL), F32).at[0].set(c_ctx).at[1:1 + bs].set(c)
    mod = _ada_mod(cond_rows, w_ada, b_ada).reshape(DEPTH, COND_ROWS, 6, D_MODEL)

    xp = x_prompt.reshape(n_p, D_MODEL)
    xs = x_sample.reshape(n_s, D_MODEL)
    h0_ctx = jnp.zeros((bp, 2, W_GRP), F32)
    route_p = route_s = mod_p_prev = mod_s_prev = None
    no_tokens_seen = jnp.zeros((N_EXPERTS, 1), F32)
    states = []
    for l in range(DEPTH):
        lw = _layer_weights(l, p)
        mod_p, mod_s = mod[l, 0:1], mod[l, 1:1 + bs]
        w_in_l = w_in[l].astype(BF16)
        w_out_l = w_out[l].astype(BF16)
        xp, proj_p = _in_proj(xp, route_p, mod_p_prev, mod_p, norm1_g[l], w_in_l, tp)
        xs, proj_s = _in_proj(xs, route_s, mod_s_prev, mod_s, norm1_g[l], w_in_l, ts)
        y_p, st = _mixers(proj_p, h0_ctx, lw, bp, tp, False)
        y_s, _ = _mixers(proj_s, state_rglru[:, l], lw, bs, ts, True)
        states.append(st)
        xp, hn_p, ei_p, pr_p, seen = _out_proj(y_p, xp, mod_p, norm2_g[l], w_out_l, w_router[l],
                                               b_router[l], no_tokens_seen, tp)
        xs, hn_s, ei_s, pr_s, counts = _out_proj(y_s, xs, mod_s, norm2_g[l], w_out_l, w_router[l],
                                                 b_router[l], seen, ts)
        dest_p, dest_s, y_buf = _moe(hn_p, hn_s, ei_p, ei_s, counts, l, w_gu, b_gu[l], w_dn, b_dn[l])
        route_p, route_s = (dest_p, pr_p, y_buf), (dest_s, pr_s, y_buf)
        mod_p_prev, mod_s_prev = mod_p, mod_s
    y_prompt = _final_norm(xp, route_p, mod_p_prev, final_g, tp).reshape(bp, tp, D_MODEL)
    y_sample = _final_norm(xs, route_s, mod_s_prev, final_g, ts).reshape(bs, ts, D_MODEL)
    return y_prompt, y_sample, jnp.stack(states, axis=1)
```

```python
import functools

import jax
import jax.numpy as jnp
from jax import lax
from jax.experimental import pallas as pl
from jax.experimental.pallas import tpu as pltpu

F32 = jnp.float32
BF16 = jnp.bfloat16

D_MODEL = 1024
DEPTH = 2
GRID_W = 64
N_MIXERS = 4
W_GRP = D_MODEL // N_MIXERS
HEADS = 4
HD = W_GRP // HEADS
N_IN = 9 * W_GRP
CONV_A = 31
CONV_B = 4
CONV_D = 3
CHUNK = 128
LRU_C = 8.0
N_EXPERTS = 32
TOP_K = 4
D_FF = D_MODEL
SWIGLU_LIMIT = 7.0
SWIGLU_ALPHA = 1.702
EPS = 1e-6

V7X_SUBLANES = 8
V7X_LANES = 128
TOKEN_ROWS = D_MODEL // V7X_LANES
DMA_PRIORITIES = 2
V7X_VMEM_BYTES = 64 * 1024 * 1024
VMEM_LIMIT = V7X_VMEM_BYTES * 7 // 8

COND_ROWS = 16
ADA_TN = 1536
ROW_TILE = 512
CONV_ROWS = GRID_W
CONV_PAD = 16
NORM_ROWS = 256
MOE_TM = 512
DISPATCH_TM = 512


def _rmsnorm(x, g):
    return x * lax.rsqrt(jnp.mean(x * x, axis=-1, keepdims=True) + EPS) * g


def _group_mean(x, m_ref):
    hi = x.astype(BF16)
    lo = (x - hi.astype(F32)).astype(BF16)
    m = m_ref[...]
    return (jnp.dot(hi, m, preferred_element_type=F32) + jnp.dot(lo, m, preferred_element_type=F32))


def _layernorm(x, g, b, m_ref):
    xc = x - _group_mean(x, m_ref)
    var = _group_mean(xc * xc, m_ref)
    return xc * lax.rsqrt(var + EPS) * g + b


def _ada_kernel(c_ref, w_ref, b_ref, o_ref):
    c = c_ref[...]
    cond = (c * jax.nn.sigmoid(c)).astype(BF16)
    o_ref[...] = jnp.dot(cond, w_ref[...].astype(BF16), preferred_element_type=F32) + b_ref[...]


def _ada_mod(cond_rows, w_ada, b_ada):
    n_col = w_ada.shape[-1]
    return pl.pallas_call(
        _ada_kernel,
        grid=(DEPTH, n_col // ADA_TN),
        in_specs=[
            pl.BlockSpec((COND_ROWS, D_MODEL), lambda l, j: (0, 0)),
            pl.BlockSpec((None, D_MODEL, ADA_TN), lambda l, j: (l, 0, j)),
            pl.BlockSpec((None, 1, ADA_TN), lambda l, j: (l, 0, j)),
        ],
        out_specs=pl.BlockSpec((None, COND_ROWS, ADA_TN), lambda l, j: (l, 0, j)),
        out_shape=jax.ShapeDtypeStruct((DEPTH, COND_ROWS, n_col), F32),
        name="ada_mod",
    )(cond_rows, w_ada, b_ada.reshape(DEPTH, 1, n_col))


def _token_rows(t, count=1):
    return pl.ds(pl.multiple_of(t * TOKEN_ROWS, TOKEN_ROWS), count * TOKEN_ROWS)


def _store_token_tiles(ref, x):
    n = x.shape[0]
    for c in range(TOKEN_ROWS):
        ref[pl.ds(c, n, stride=TOKEN_ROWS), :] = x[:, c * V7X_LANES:(c + 1) * V7X_LANES]


def _load_token_tiles(ref, n):
    return [ref[pl.ds(c, n, stride=TOKEN_ROWS), :] for c in range(TOKEN_ROWS)]


def _combine_experts(dest_cur, dest_nxt, pr_ref, ybuf, rows, sems):
    i = pl.program_id(0)
    n_steps = pl.num_programs(0)
    tm = rows.shape[2] // TOKEN_ROWS
    slot = i % 2

    def row_copy(dref, j, k, slot_):
        return pltpu.make_async_copy(ybuf.at[_token_rows(dref[0, j * TOP_K + k])],
                                     rows.at[slot_, k, _token_rows(j)], sems.at[slot_])

    def issue(dref, slot_):
        def body(j, carry):
            for k in range(TOP_K):
                row_copy(dref, j, k, slot_).start(priority=k % DMA_PRIORITIES)
            return carry

        lax.fori_loop(0, tm, body, 0, unroll=4)

    @pl.when(i == 0)
    def _():
        issue(dest_cur, 0)

    @pl.when(i + 1 < n_steps)
    def _():
        issue(dest_nxt, 1 - slot)

    for k in range(TOP_K):
        pltpu.make_async_copy(ybuf.at[_token_rows(0, tm)], rows.at[slot, k], sems.at[slot]).wait()
    experts = [_load_token_tiles(rows.at[slot, k], tm) for k in range(TOP_K)]
    pad = jnp.zeros((V7X_LANES - pr_ref.shape[0], tm), F32)
    probs = jnp.concatenate([pr_ref[...], pad], axis=0).T
    chunks = []
    for c in range(TOKEN_ROWS):
        acc = probs[:, 0:1] * experts[0][c]
        for k in range(1, TOP_K):
            acc = acc + probs[:, k:k + 1] * experts[k][c]
        chunks.append(acc)
    return jnp.concatenate(chunks, axis=1)


def _inproj_kernel(*refs, has_res):
    if has_res:
        (dest_cur, dest_nxt, x_ref, pr_ref, ybuf, modp_ref, mod_ref, g_ref, w_ref,
         xo_ref, p_ref, rows, sems) = refs
        x = x_ref[...] + modp_ref[5:6, :] * _combine_experts(dest_cur, dest_nxt, pr_ref, ybuf, rows, sems)
        xo_ref[...] = x
    else:
        x_ref, mod_ref, g_ref, w_ref, p_ref = refs
        x = x_ref[...]
    hn = _rmsnorm(x, g_ref[...]) * (1.0 + mod_ref[1:2, :]) + mod_ref[0:1, :]
    p_ref[...] = jnp.dot(hn.astype(BF16), w_ref[...], preferred_element_type=F32)


def _mod_spec(mod, seq, tm):
    if mod.shape[0] == 1:
        return pl.BlockSpec((None, 6, D_MODEL), lambda i: (0, 0, 0))
    return pl.BlockSpec((None, 6, D_MODEL), lambda i: ((i * tm) // seq, 0, 0))


def _per_tile(dest, tm):
    return dest.T.reshape(dest.shape[1] // tm, 1, tm * TOP_K)


def _combine_operands(dest, probs, y_buf, n, tm):
    n_steps = n // tm
    dest3 = _per_tile(dest, tm)
    smem = functools.partial(pl.BlockSpec, (None, 1, tm * TOP_K), memory_space=pltpu.SMEM)
    args = [dest3, dest3, probs, y_buf]
    specs = [smem(lambda i: (i, 0, 0)),
             smem(lambda i: (jnp.minimum(i + 1, n_steps - 1), 0, 0)),
             pl.BlockSpec((2 * TOP_K, tm), lambda i: (0, i)),
             pl.BlockSpec(memory_space=pl.ANY)]
    scratch = [pltpu.VMEM((2, TOP_K, tm * TOKEN_ROWS, V7X_LANES), F32), pltpu.SemaphoreType.DMA((2,))]
    return args, specs, scratch


def _in_proj(x, route, mod_prev, mod, g, w_bf16, seq):
    n = x.shape[0]
    tm = min(ROW_TILE, seq)
    row = pl.BlockSpec((tm, D_MODEL), lambda i: (i, 0))
    has_res = route is not None
    args, specs, scratch = [], [], []
    if has_res:
        args, specs, scratch = _combine_operands(*route, n, tm)
        args = args[:2] + [x] + args[2:] + [mod_prev]
        specs = specs[:2] + [row] + specs[2:] + [_mod_spec(mod_prev, seq, tm)]
    else:
        args, specs = [x], [row]
    args += [mod, g.reshape(1, D_MODEL), w_bf16]
    specs += [_mod_spec(mod, seq, tm),
              pl.BlockSpec((1, D_MODEL), lambda i: (0, 0)),
              pl.BlockSpec((D_MODEL, N_IN), lambda i: (0, 0))]
    proj_shape = jax.ShapeDtypeStruct((n, N_IN), F32)
    proj_spec = pl.BlockSpec((tm, N_IN), lambda i: (i, 0))
    if has_res:
        out_shape = (jax.ShapeDtypeStruct((n, D_MODEL), F32), proj_shape)
        out_specs = (row, proj_spec)
    else:
        out_shape, out_specs = proj_shape, proj_spec
    out = pl.pallas_call(
        functools.partial(_inproj_kernel, has_res=has_res),
        grid=(n // tm,),
        in_specs=specs,
        out_specs=out_specs,
        out_shape=out_shape,
        scratch_shapes=scratch,
        compiler_params=pltpu.CompilerParams(dimension_semantics=("arbitrary",), vmem_limit_bytes=VMEM_LIMIT),
        name="in_proj",
    )(*args)
    return out if has_res else (x, out)


def _group_scan(a, b, reverse):
    shape = a.shape
    grouped = (shape[0] // V7X_SUBLANES, V7X_SUBLANES, shape[1])
    a, b = a.reshape(grouped), b.reshape(grouped)
    ri = lax.broadcasted_iota(jnp.int32, grouped, 1)
    for d in (1, 2, 4):
        shift = V7X_SUBLANES - d if reverse else d
        keep = ri < V7X_SUBLANES - d if reverse else ri >= d
        ra, rb = pltpu.roll(a, shift, 1), pltpu.roll(b, shift, 1)
        b = a * jnp.where(keep, rb, 0.0) + b
        a = a * jnp.where(keep, ra, 1.0)
    return a.reshape(shape), b.reshape(shape)


def _mixer_kernel(s0, s1, s2, caw, cab, nag, nab, cbw, cbb, wg, bg, lam, h0, sng, snb, sw, sbias, cdw,
                  m_head, y_ref, st_ref, pad_s, af_s, bf_s, ab_s, bb_s, *, seq, on_grid):
    mixer = pl.program_id(1)
    n_conv = seq // CONV_ROWS
    win_rows = CONV_ROWS + 2 * CONV_PAD

    def rows_at(c, size):
        return pl.ds(pl.multiple_of(c * size, size), size)

    def fill_padded(fn):
        zeros = jnp.zeros((CONV_PAD, W_GRP), F32)
        pad_s[0:CONV_PAD, :] = zeros
        pad_s[CONV_PAD + seq:2 * CONV_PAD + seq, :] = zeros

        def body(c, carry):
            dst = pl.ds(pl.multiple_of(c * CONV_ROWS + CONV_PAD, V7X_SUBLANES), CONV_ROWS)
            pad_s[dst, :] = fn(rows_at(c, CONV_ROWS))
            return carry

        lax.fori_loop(0, n_conv, body, 0)

    def conv_window(win, w_ref, taps, pad_l):
        acc = jnp.zeros((CONV_ROWS, W_GRP), F32)
        for mis in range(V7X_SUBLANES):
            starts = [(k, CONV_PAD - pad_l + k) for k in range(taps)
                      if (CONV_PAD - pad_l + k) % V7X_SUBLANES == mis]
            if not starts:
                continue
            shifted = pltpu.roll(win, win_rows - mis, 0) if mis else win
            for k, start in starts:
                acc = acc + w_ref[k:k + 1, :] * shifted[start - mis:start - mis + CONV_ROWS, :]
        return acc

    def conv_rows(c, w_ref, taps, pad_l):
        win = pad_s[pl.ds(pl.multiple_of(c * CONV_ROWS, CONV_ROWS), win_rows), :]
        return conv_window(win, w_ref, taps, pad_l)

    @pl.when(mixer == 0)
    def _conformer():
        def glu(rows):
            return s0[rows, :] * jax.nn.sigmoid(s1[rows, :])

        if not on_grid:
            fill_padded(glu)

        def body(c, carry):
            if on_grid:
                edge = jnp.zeros((CONV_PAD, W_GRP), F32)
                win = jnp.concatenate([edge, glu(rows_at(c, CONV_ROWS)), edge], axis=0)
                z = conv_window(win, caw, CONV_A, CONV_A // 2)
            else:
                z = conv_rows(c, caw, CONV_A, CONV_A // 2)
            af_s[rows_at(c, CONV_ROWS), :] = z + cab[...]
            return carry

        lax.fori_loop(0, n_conv, body, 0)

        def norm(c, carry):
            rows = rows_at(c, NORM_ROWS)
            z = _layernorm(af_s[rows, :], nag[...], nab[...], m_head)
            y_ref[rows, :] = (z * jax.nn.sigmoid(z)).astype(BF16)
            return carry

        lax.fori_loop(0, seq // NORM_ROWS, norm, 0, unroll=2)

    @pl.when(mixer == 1)
    def _rglru():
        fill_padded(lambda rows: s0[rows, :])
        lam_v = lam[...]
        softplus_neg = jnp.maximum(-lam_v, 0.0) + jnp.log1p(jnp.exp(-jnp.abs(lam_v)))

        def gates(c, carry):
            rows = rows_at(c, CONV_ROWS)
            xc = conv_rows(c, cbw, CONV_B, 2) + cbb[...]
            gt = jax.nn.sigmoid(jnp.dot(xc.astype(BF16), wg[...], preferred_element_type=F32) + bg[...])
            for d, (a_s, b_s) in enumerate(((af_s, bf_s), (ab_s, bb_s))):
                r_gate = gt[:, (2 * d) * W_GRP:(2 * d + 1) * W_GRP]
                i_gate = gt[:, (2 * d + 1) * W_GRP:(2 * d + 2) * W_GRP]
                log_a = -LRU_C * r_gate * softplus_neg[d:d + 1, :]
                a = jnp.exp(log_a)
                b = jnp.sqrt(jnp.maximum(-jnp.tanh(log_a) * (a * a + 1.0), 0.0)) * (i_gate * xc)
                a, b = _group_scan(a, b, reverse=(d == 1))
                a_s[rows, :] = a
                b_s[rows, :] = b
            return carry

        lax.fori_loop(0, n_conv, gates, 0, unroll=2)

        n_grp = seq // V7X_SUBLANES

        def chain(g, carry):
            cf, cb = carry
            rf = rows_at(g, V7X_SUBLANES)
            rb = rows_at(n_grp - 1 - g, V7X_SUBLANES)
            hf = af_s[rf, :] * cf + bf_s[rf, :]
            bf_s[rf, :] = hf
            hb = ab_s[rb, :] * cb + bb_s[rb, :]
            bb_s[rb, :] = hb
            cf = jnp.broadcast_to(hf[V7X_SUBLANES - 1:V7X_SUBLANES, :], (V7X_SUBLANES, W_GRP))
            cb = jnp.broadcast_to(hb[0:1, :], (V7X_SUBLANES, W_GRP))
            return cf, cb

        init = (jnp.broadcast_to(h0[0:1, :], (V7X_SUBLANES, W_GRP)),
                jnp.broadcast_to(h0[1:2, :], (V7X_SUBLANES, W_GRP)))
        cf, cb = lax.fori_loop(0, n_grp, chain, init, unroll=4)
        st_ref[0:1, :] = cf[0:1, :]
        st_ref[1:2, :] = cb[0:1, :]

        def out(c, carry):
            rows = rows_at(c, CONV_ROWS)
            y_ref[rows, :] = (jax.nn.gelu(s1[rows, :]) * (bf_s[rows, :] + bb_s[rows, :])).astype(BF16)
            return carry

        lax.fori_loop(0, n_conv, out, 0)

    @pl.when(mixer == 2)
    def _sgu():
        lane = lax.broadcasted_iota(jnp.int32, (CHUNK, W_GRP), 1)

        def body(n, carry):
            rows = rows_at(n, CHUNK)
            v = s1[rows, :]
            vc = v - jnp.mean(v, axis=-1, keepdims=True)
            var = jnp.mean(vc * vc, axis=-1, keepdims=True)
            v = (vc * lax.rsqrt(var + EPS) * sng[...] + snb[...]).astype(BF16)
            s = sbias[...]
            for h in range(HEADS):
                sh = jnp.dot(sw[h], v, preferred_element_type=F32)
                s = s + jnp.where((lane >= h * HD) & (lane < (h + 1) * HD), sh, 0.0)
            y_ref[rows, :] = (s0[rows, :] * s).astype(BF16)
            return carry

        lax.fori_loop(0, seq // CHUNK, body, 0, unroll=4)

    @pl.when(mixer == 3)
    def _gated_conv():
        fill_padded(lambda rows: s1[rows, :] * s2[rows, :])

        def body(c, carry):
            rows = rows_at(c, CONV_ROWS)
            y_ref[rows, :] = (s0[rows, :] * conv_rows(c, cdw, CONV_D, CONV_D // 2)).astype(BF16)
            return carry

        lax.fori_loop(0, n_conv, body, 0)


def _mixers(proj, h0, lw, batch, seq, on_grid):
    proj3 = proj.reshape(batch, seq, N_IN)
    slab = (None, seq, W_GRP)

    def const(shape):
        return pl.BlockSpec(shape, lambda b, m: (0,) * len(shape))

    in_specs = [
        pl.BlockSpec(slab, lambda b, m: (b, 0, 2 * m)),
        pl.BlockSpec(slab, lambda b, m: (b, 0, 2 * m + 1)),
        pl.BlockSpec(slab, lambda b, m: (b, 0, 8)),
        const((CONV_A, W_GRP)), const((1, W_GRP)), const((1, W_GRP)), const((1, W_GRP)),
        const((CONV_B, W_GRP)), const((1, W_GRP)),
        const((W_GRP, 4 * W_GRP)), const((1, 4 * W_GRP)), const((2, W_GRP)),
        pl.BlockSpec((None, 2, W_GRP), lambda b, m: (b, 0, 0)),
        const((1, W_GRP)), const((1, W_GRP)), const((HEADS, CHUNK, CHUNK)), const((CHUNK, W_GRP)),
        const((CONV_D, W_GRP)), const((W_GRP, W_GRP)),
    ]
    y, st = pl.pallas_call(
        functools.partial(_mixer_kernel, seq=seq, on_grid=on_grid),
        grid=(batch, N_MIXERS),
        in_specs=in_specs,
        out_specs=(pl.BlockSpec(slab, lambda b, m: (b, 0, m)),
                   pl.BlockSpec((None, 2, W_GRP), lambda b, m: (b, 0, 0))),
        out_shape=(jax.ShapeDtypeStruct((batch, seq, D_MODEL), BF16),
                   jax.ShapeDtypeStruct((batch, 2, W_GRP), F32)),
        scratch_shapes=[pltpu.VMEM((seq + 2 * CONV_PAD, W_GRP), F32)] + [pltpu.VMEM((seq, W_GRP), F32)] * 4,
        compiler_params=pltpu.CompilerParams(dimension_semantics=("arbitrary", "arbitrary"),
                                             vmem_limit_bytes=VMEM_LIMIT),
        name="mixers",
    )(proj3, proj3, proj3, lw["conv_a_w"], lw["conv_a_b"], lw["norm_a_g"], lw["norm_a_b"],
      lw["conv_b_w"], lw["conv_b_b"], lw["gate_w"], lw["gate_b"], lw["lru_lam"], h0,
      lw["sgu_norm_g"], lw["sgu_norm_b"], lw["sgu_w"], lw["sgu_bias"], lw["conv_d_w"],
      lw["m_head"])
    return y.reshape(batch * seq, D_MODEL), st


def _outproj_kernel(y_ref, x_ref, mod_ref, g_ref, wo_ref, wr_ref, br_ref, tri_ref, cin_ref,
                    xm_ref, hn_ref, ei_ref, pr_ref, cnt_ref, seen_s):
    tm = x_ref.shape[0]
    e = N_EXPERTS

    @pl.when(pl.program_id(0) == 0)
    def _():
        seen_s[...] = cin_ref[...]

    y = jnp.dot(y_ref[...], wo_ref[...], preferred_element_type=F32)
    x = x_ref[...] + mod_ref[2:3, :] * y
    xm_ref[...] = x
    hn = _rmsnorm(x, g_ref[...]) * (1.0 + mod_ref[4:5, :]) + mod_ref[3:4, :]
    _store_token_tiles(hn_ref, hn)

    h1 = hn.astype(BF16)
    r1 = hn - h1.astype(F32)
    h2 = r1.astype(BF16)
    h3 = (r1 - h2.astype(F32)).astype(BF16)
    w_t = wr_ref[...]
    contract_features = (((1,), (1,)), ((), ()))
    p1 = lax.dot_general(w_t, h1, contract_features, preferred_element_type=F32)
    p2 = lax.dot_general(w_t, h2, contract_features, preferred_element_type=F32)
    p3 = lax.dot_general(w_t, h3, contract_features, preferred_element_type=F32)
    logits = (p1[2 * e:3 * e] + p2[e:2 * e] + p3[0:e] + p1[e:2 * e] + p2[0:e] + p1[0:e]) + br_ref[...]

    expert = lax.broadcasted_iota(jnp.int32, (e, tm), 0)
    beats = jnp.zeros((e, tm), F32)
    for other in range(e):
        lo = logits[other:other + 1, :]
        ahead = (lo > logits) | ((lo == logits) & (expert > other))
        beats = beats + jnp.where(ahead, 1.0, 0.0)

    chosen = jnp.where(beats < float(TOP_K), 1.0, 0.0)
    earlier = jnp.dot(chosen.astype(BF16), tri_ref[...], preferred_element_type=F32) + seen_s[...]
    seen_s[...] = earlier[:, tm - 1:tm] + chosen[:, tm - 1:tm]
    cnt_ref[...] = seen_s[...]

    def pick(k, values):
        return jnp.sum(jnp.where(beats == float(k), values, 0.0), axis=0, keepdims=True)

    expert_f = expert.astype(F32)
    vals = [pick(k, logits) for k in range(TOP_K)]
    exps = [jnp.exp(v - vals[0]) for v in vals]
    den = exps[0] + exps[1] + exps[2] + exps[3]
    out_row = lax.broadcasted_iota(jnp.int32, (2 * TOP_K, tm), 0)
    ei = jnp.zeros((2 * TOP_K, tm), F32)
    pr = jnp.zeros((2 * TOP_K, tm), F32)
    for k in range(TOP_K):
        ei = jnp.where(out_row == k, pick(k, expert_f), ei)
        ei = jnp.where(out_row == TOP_K + k, pick(k, earlier), ei)
        pr = jnp.where(out_row == k, exps[k] / den, pr)
    ei_ref[...] = ei.astype(jnp.int32)
    pr_ref[...] = pr


def _out_proj(y, x, mod, g, wo_bf16, w_router, b_router, seen, seq):
    n = x.shape[0]
    tm = min(ROW_TILE, seq)
    row = pl.BlockSpec((tm, D_MODEL), lambda i: (i, 0))
    route = pl.BlockSpec((2 * TOP_K, tm), lambda i: (0, i))
    cnt = pl.BlockSpec((N_EXPERTS, 1), lambda i: (0, 0))
    before = jnp.tri(tm, k=-1, dtype=BF16).T
    w1 = w_router.astype(BF16)
    w2 = (w_router - w1.astype(F32)).astype(BF16)
    w3 = (w_router - w1.astype(F32) - w2.astype(F32)).astype(BF16)
    w_t = jnp.concatenate([w1.T, w2.T, w3.T, jnp.zeros_like(w1.T)], axis=0)
    return pl.pallas_call(
        _outproj_kernel,
        grid=(n // tm,),
        in_specs=[row, row, _mod_spec(mod, seq, tm),
                  pl.BlockSpec((1, D_MODEL), lambda i: (0, 0)),
                  pl.BlockSpec((D_MODEL, D_MODEL), lambda i: (0, 0)),
                  pl.BlockSpec((4 * N_EXPERTS, D_MODEL), lambda i: (0, 0)),
                  cnt, pl.BlockSpec((tm, tm), lambda i: (0, 0)), cnt],
        out_specs=(row, pl.BlockSpec((tm * TOKEN_ROWS, V7X_LANES), lambda i: (i, 0)), route, route, cnt),
        out_shape=(jax.ShapeDtypeStruct((n, D_MODEL), F32),
                   jax.ShapeDtypeStruct((n * TOKEN_ROWS, V7X_LANES), F32),
                   jax.ShapeDtypeStruct((2 * TOP_K, n), jnp.int32),
                   jax.ShapeDtypeStruct((2 * TOP_K, n), F32),
                   jax.ShapeDtypeStruct((N_EXPERTS, 1), F32)),
        scratch_shapes=[pltpu.VMEM((N_EXPERTS, 1), F32)],
        compiler_params=pltpu.CompilerParams(dimension_semantics=("arbitrary",)),
        name="out_proj",
    )(y, x, mod, g.reshape(1, D_MODEL), wo_bf16, w_t, b_router.reshape(N_EXPERTS, 1), before, seen)


def _dispatch_kernel(zero_blk, zero_on, dest_ref, hn_p, hn_s, xbuf, zeros_s, sem, zsem, *, tiles_p):
    i = pl.program_id(0)

    @pl.when(i == 0)
    def _():
        zeros_s[...] = jnp.zeros_like(zeros_s)

        def fill(j):
            return pltpu.make_async_copy(zeros_s, xbuf.at[_token_rows(zero_blk[j] * MOE_TM, MOE_TM)], zsem)

        for j in range(zero_blk.shape[0]):
            @pl.when(zero_on[j] == 1)
            def _():
                fill(j).start()

        for j in range(zero_blk.shape[0]):
            @pl.when(zero_on[j] == 1)
            def _():
                fill(j).wait()

    def issue(src):
        def body(j, carry):
            row = src.at[_token_rows(j)]
            for k in range(TOP_K):
                pltpu.make_async_copy(row, xbuf.at[_token_rows(dest_ref[0, j * TOP_K + k])],
                                      sem).start(priority=k % DMA_PRIORITIES)
            return carry

        lax.fori_loop(0, DISPATCH_TM, body, 0, unroll=4)

    @pl.when(i < tiles_p)
    def _():
        issue(hn_p)

    @pl.when(i >= tiles_p)
    def _():
        issue(hn_s)

    for _ in range(TOP_K):
        pltpu.make_async_copy(hn_s, xbuf.at[_token_rows(0, DISPATCH_TM)], sem).wait()


def _dispatch(dest, zero_blk, zero_on, hn_p, hn_s, n_pad):
    n_p, n_s = hn_p.shape[0] // TOKEN_ROWS, hn_s.shape[0] // TOKEN_ROWS
    tiles_p = n_p // DISPATCH_TM
    n_steps = (n_p + n_s) // DISPATCH_TM
    tile = (DISPATCH_TM * TOKEN_ROWS, V7X_LANES)
    grid_spec = pltpu.PrefetchScalarGridSpec(
        num_scalar_prefetch=2,
        grid=(n_steps,),
        in_specs=[pl.BlockSpec((None, 1, DISPATCH_TM * TOP_K), lambda i, zb, zo: (i, 0, 0),
                               memory_space=pltpu.SMEM),
                  pl.BlockSpec(tile, lambda i, zb, zo: (jnp.minimum(i, tiles_p - 1), 0)),
                  pl.BlockSpec(tile, lambda i, zb, zo: (jnp.maximum(i - tiles_p, 0), 0))],
        out_specs=pl.BlockSpec(memory_space=pl.ANY),
        scratch_shapes=[pltpu.VMEM((MOE_TM * TOKEN_ROWS, V7X_LANES), F32), pltpu.SemaphoreType.DMA(()),
                        pltpu.SemaphoreType.DMA(())],
    )
    return pl.pallas_call(
        functools.partial(_dispatch_kernel, tiles_p=tiles_p),
        grid_spec=grid_spec,
        out_shape=jax.ShapeDtypeStruct((n_pad * TOKEN_ROWS, V7X_LANES), F32),
        compiler_params=pltpu.CompilerParams(dimension_semantics=("arbitrary",)),
        name="moe_dispatch",
    )(zero_blk, zero_on, _per_tile(dest, DISPATCH_TM), hn_p, hn_s)


def _moe_kernel(blk_e, blk_new, blk_on, blk_slot, blk_next, x_ref, wgu_hbm, bgu_ref, wdn_hbm, bdn_ref, o_ref,
                wgu_f, wdn_f, wgu_s, wdn_s, sems, *, layer):
    i = pl.program_id(0)

    def fetch(expert, slot):
        return (pltpu.make_async_copy(wgu_hbm.at[layer, expert], wgu_f.at[slot], sems.at[0, slot]),
                pltpu.make_async_copy(wdn_hbm.at[layer, expert], wdn_f.at[slot], sems.at[1, slot]))

    @pl.when(i == 0)
    def _():
        for copy in fetch(blk_e[0], 0):
            copy.start()

    @pl.when(blk_on[i] == 0)
    def _():
        o_ref[...] = jnp.zeros_like(o_ref)

    @pl.when(blk_on[i] == 1)
    def _():
        @pl.when(blk_new[i] == 1)
        def _():
            slot = blk_slot[i]
            for copy in fetch(blk_e[i], slot):
                copy.wait()

            @pl.when(blk_next[i] >= 0)
            def _():
                for copy in fetch(blk_next[i], 1 - slot):
                    copy.start()

            wgu_s[...] = wgu_f[slot].astype(BF16)
            wdn_s[...] = wdn_f[slot].astype(BF16)

        x = jnp.concatenate([c.astype(BF16) for c in _load_token_tiles(x_ref, MOE_TM)], axis=1)
        gu = jnp.dot(x, wgu_s[...], preferred_element_type=F32) + bgu_ref[...]
        g = jnp.minimum(gu[:, :D_FF], SWIGLU_LIMIT)
        u = jnp.clip(gu[:, D_FF:], -SWIGLU_LIMIT, SWIGLU_LIMIT)
        act = (u + 1.0) * (g * jax.nn.sigmoid(SWIGLU_ALPHA * g))
        _store_token_tiles(o_ref, jnp.dot(act.astype(BF16), wdn_s[...], preferred_element_type=F32)
                           + bdn_ref[...])


def _moe_blocks(x_buf, blk, layer, w_gu, b_gu, w_dn, b_dn):
    n_pad = x_buf.shape[0] // TOKEN_ROWS
    n_blk = n_pad // MOE_TM
    tile = pl.BlockSpec((MOE_TM * TOKEN_ROWS, V7X_LANES), lambda i, e, *_: (i, 0))
    grid_spec = pltpu.PrefetchScalarGridSpec(
        num_scalar_prefetch=len(blk),
        grid=(n_blk,),
        in_specs=[
            tile,
            pl.BlockSpec(memory_space=pl.ANY),
            pl.BlockSpec((None, 1, 2 * D_FF), lambda i, e, *_: (e[i], 0, 0)),
            pl.BlockSpec(memory_space=pl.ANY),
            pl.BlockSpec((None, 1, D_MODEL), lambda i, e, *_: (e[i], 0, 0)),
        ],
        out_specs=tile,
        scratch_shapes=[pltpu.VMEM((2, D_MODEL, 2 * D_FF), F32), pltpu.VMEM((2, D_FF, D_MODEL), F32),
                        pltpu.VMEM((D_MODEL, 2 * D_FF), BF16), pltpu.VMEM((D_FF, D_MODEL), BF16),
                        pltpu.SemaphoreType.DMA((2, 2))],
    )
    return pl.pallas_call(
        functools.partial(_moe_kernel, layer=layer),
        grid_spec=grid_spec,
        out_shape=jax.ShapeDtypeStruct((n_pad * TOKEN_ROWS, V7X_LANES), F32),
        compiler_params=pltpu.CompilerParams(dimension_semantics=("arbitrary",),
                                             vmem_limit_bytes=VMEM_LIMIT),
        name="moe_experts",
    )(*blk, x_buf, w_gu, b_gu.reshape(N_EXPERTS, 1, 2 * D_FF), w_dn, b_dn.reshape(N_EXPERTS, 1, D_MODEL))


def _moe(hn_p, hn_s, ei_p, ei_s, counts, layer, w_gu, b_gu, w_dn, b_dn):
    n_asg = (ei_p.shape[1] + ei_s.shape[1]) * TOP_K
    n_pad = (n_asg + N_EXPERTS * (MOE_TM - 1) + MOE_TM - 1) // MOE_TM * MOE_TM
    n_blk = n_pad // MOE_TM
    counts = counts.reshape(N_EXPERTS).astype(jnp.int32)
    padded = (counts + MOE_TM - 1) // MOE_TM * MOE_TM
    pad_end = jnp.cumsum(padded)
    pad_start = pad_end - padded
    blk_row = jnp.arange(n_blk, dtype=jnp.int32) * MOE_TM
    blk_e = jnp.minimum(jnp.sum((blk_row[:, None] >= pad_end[None, :]).astype(jnp.int32), axis=1),
                        N_EXPERTS - 1)
    blk_on = (blk_row < pad_end[-1]).astype(jnp.int32)
    blk_new = jnp.concatenate([jnp.ones((1,), jnp.int32), (blk_e[1:] != blk_e[:-1]).astype(jnp.int32)])
    blk_slot = (jnp.cumsum(blk_new) - 1) % 2
    experts = jnp.arange(N_EXPERTS, dtype=jnp.int32)
    in_use = jnp.where(counts > 0, experts, N_EXPERTS)
    after = jnp.concatenate([lax.cummin(in_use[::-1])[::-1][1:], jnp.full((1,), N_EXPERTS, jnp.int32)])
    blk_next = jnp.where(after[blk_e] < N_EXPERTS, after[blk_e], -1)

    def slots(ei):
        expert, rank = ei[0:TOP_K], ei[TOP_K:2 * TOP_K]
        start = jnp.sum(jnp.where(expert[None] == jnp.arange(N_EXPERTS, dtype=jnp.int32)[:, None, None],
                                  pad_start[:, None, None], 0), axis=0)
        return start + rank

    tail = n_blk - n_asg // MOE_TM
    last_on = (counts % MOE_TM != 0).astype(jnp.int32)
    tail_blk = pad_end[-1] // MOE_TM + jnp.arange(tail, dtype=jnp.int32)
    tail_on = (tail_blk < n_blk).astype(jnp.int32)
    zero_blk = jnp.concatenate([(pad_end // MOE_TM - 1) * last_on, tail_blk * tail_on])
    zero_on = jnp.concatenate([last_on, tail_on])

    dest_p, dest_s = slots(ei_p), slots(ei_s)
    x_buf = _dispatch(jnp.concatenate([dest_p, dest_s], axis=1), zero_blk, zero_on, hn_p, hn_s, n_pad)
    y_buf = _moe_blocks(x_buf, (blk_e, blk_new, blk_on, blk_slot.astype(jnp.int32), blk_next.astype(jnp.int32)),
                        layer, w_gu, b_gu, w_dn, b_dn)
    return dest_p, dest_s, y_buf


def _final_kernel(dest_cur, dest_nxt, x_ref, pr_ref, ybuf, mod_ref, g_ref, o_ref, rows, sems):
    x = x_ref[...] + mod_ref[5:6, :] * _combine_experts(dest_cur, dest_nxt, pr_ref, ybuf, rows, sems)
    o_ref[...] = _rmsnorm(x, g_ref[...])


def _final_norm(x, route, mod, g, seq):
    n = x.shape[0]
    tm = min(ROW_TILE, seq)
    row = pl.BlockSpec((tm, D_MODEL), lambda i: (i, 0))
    args, specs, scratch = _combine_operands(*route, n, tm)
    return pl.pallas_call(
        _final_kernel,
        grid=(n // tm,),
        in_specs=specs[:2] + [row] + specs[2:] + [_mod_spec(mod, seq, tm),
                                                  pl.BlockSpec((1, D_MODEL), lambda i: (0, 0))],
        out_specs=row,
        out_shape=jax.ShapeDtypeStruct((n, D_MODEL), F32),
        scratch_shapes=scratch,
        compiler_params=pltpu.CompilerParams(dimension_semantics=("arbitrary",), vmem_limit_bytes=VMEM_LIMIT),
        name="final_norm",
    )(*args[:2], x, *args[2:], mod, g.reshape(1, D_MODEL))


def _block_diag(w):
    eye = jnp.eye(HEADS, dtype=w.dtype)
    return (eye[:, None, :, None] * w[:, :, None, :]).reshape(W_GRP, W_GRP)


def _layer_weights(l, p):
    gate_w = jnp.concatenate([_block_diag(p["lru_wr"][l, 0]), _block_diag(p["lru_wi"][l, 0]),
                              _block_diag(p["lru_wr"][l, 1]), _block_diag(p["lru_wi"][l, 1])], axis=1)
    gate_b = jnp.concatenate([p["lru_br"][l, 0], p["lru_bi"][l, 0], p["lru_br"][l, 1], p["lru_bi"][l, 1]])
    head_of = jnp.arange(W_GRP) // HD
    row = lambda v: v.reshape(1, W_GRP)
    return dict(
        conv_a_w=p["conv_a_w"][l], conv_a_b=row(p["conv_a_b"][l]),
        norm_a_g=row(p["norm_a_g"][l]), norm_a_b=row(p["norm_a_b"][l]),
        conv_b_w=p["conv_b_w"][l], conv_b_b=row(p["conv_b_b"][l]),
        gate_w=gate_w.astype(BF16), gate_b=gate_b.reshape(1, 4 * W_GRP), lru_lam=p["lru_lam"][l],
        sgu_norm_g=row(p["sgu_norm_g"][l]), sgu_norm_b=row(p["sgu_norm_b"][l]),
        sgu_w=p["sgu_w"][l].astype(BF16), sgu_bias=jnp.repeat(p["sgu_b"][l].T, HD, axis=1),
        conv_d_w=p["conv_d_w"][l],
        m_head=((head_of[:, None] == head_of[None, :]).astype(F32) / HD).astype(BF16),
    )


def kernel(x_prompt, x_sample, state_rglru, c, c_ctx, w_ada, b_ada, norm1_g, norm2_g, w_in, conv_a_w,
           conv_a_b, norm_a_g, norm_a_b, conv_b_w, conv_b_b, lru_wr, lru_br, lru_wi, lru_bi, lru_lam,
           sgu_norm_g, sgu_norm_b, sgu_w, sgu_b, conv_d_w, w_out, w_router, b_router, w_gu, b_gu, w_dn,
           b_dn, final_g):
    p = dict(conv_a_w=conv_a_w, conv_a_b=conv_a_b, norm_a_g=norm_a_g, norm_a_b=norm_a_b,
             conv_b_w=conv_b_w, conv_b_b=conv_b_b, lru_wr=lru_wr, lru_br=lru_br, lru_wi=lru_wi,
             lru_bi=lru_bi, lru_lam=lru_lam, sgu_norm_g=sgu_norm_g, sgu_norm_b=sgu_norm_b,
             sgu_w=sgu_w, sgu_b=sgu_b, conv_d_w=conv_d_w)
    bp, tp, _ = x_prompt.shape
    bs, ts, _ = x_sample.shape
    n_p, n_s = bp * tp, bs * ts

    cond_rows = jnp.zeros((COND_ROWS, D_MODEL), F32).at[0].set(c_ctx).at[1:1 + bs].set(c)
    mod = _ada_mod(cond_rows, w_ada, b_ada).reshape(DEPTH, COND_ROWS, 6, D_MODEL)

    xp = x_prompt.reshape(n_p, D_MODEL)
    xs = x_sample.reshape(n_s, D_MODEL)
    h0_ctx = jnp.zeros((bp, 2, W_GRP), F32)
    route_p = route_s = mod_p_prev = mod_s_prev = None
    no_tokens_seen = jnp.zeros((N_EXPERTS, 1), F32)
    states = []
    for l in range(DEPTH):
        lw = _layer_weights(l, p)
        mod_p, mod_s = mod[l, 0:1], mod[l, 1:1 + bs]
        w_in_l = w_in[l].astype(BF16)
        w_out_l = w_out[l].astype(BF16)
        xp, proj_p = _in_proj(xp, route_p, mod_p_prev, mod_p, norm1_g[l], w_in_l, tp)
        xs, proj_s = _in_proj(xs, route_s, mod_s_prev, mod_s, norm1_g[l], w_in_l, ts)
        y_p, st = _mixers(proj_p, h0_ctx, lw, bp, tp, False)
        y_s, _ = _mixers(proj_s, state_rglru[:, l], lw, bs, ts, True)
        states.append(st)
        xp, hn_p, ei_p, pr_p, seen = _out_proj(y_p, xp, mod_p, norm2_g[l], w_out_l, w_router[l],
                                               b_router[l], no_tokens_seen, tp)
        xs, hn_s, ei_s, pr_s, counts = _out_proj(y_s, xs, mod_s, norm2_g[l], w_out_l, w_router[l],
                                                 b_router[l], seen, ts)
        dest_p, dest_s, y_buf = _moe(hn_p, hn_s, ei_p, ei_s, counts, l, w_gu, b_gu[l], w_dn, b_dn[l])
        route_p, route_s = (dest_p, pr_p, y_buf), (dest_s, pr_s, y_buf)
        mod_p_prev, mod_s_prev = mod_p, mod_s
    y_prompt = _final_norm(xp, route_p, mod_p_prev, final_g, tp).reshape(bp, tp, D_MODEL)
    y_sample = _final_norm(xs, route_s, mod_s_prev, final_g, ts).reshape(bs, ts, D_MODEL)
    return y_prompt, y_sample, jnp.stack(states, axis=1)
```

```python
import functools

import jax
import jax.numpy as jnp
from jax import lax
from jax.experimental import pallas as pl
from jax.experimental.pallas import tpu as pltpu

F32 = jnp.float32
BF16 = jnp.bfloat16

D_MODEL = 1024
DEPTH = 2
GRID_W = 64
N_MIXERS = 4
W_GRP = D_MODEL // N_MIXERS
HEADS = 4
HD = W_GRP // HEADS
N_IN = 9 * W_GRP
CONV_A = 31
CONV_B = 4
CONV_D = 3
CHUNK = 128
LRU_C = 8.0
N_EXPERTS = 32
TOP_K = 4
D_FF = D_MODEL
SWIGLU_LIMIT = 7.0
SWIGLU_ALPHA = 1.702
EPS = 1e-6

V7X_SUBLANES = 8
V7X_LANES = 128
TOKEN_ROWS = D_MODEL // V7X_LANES
DMA_PRIORITIES = 2
V7X_VMEM_BYTES = 64 * 1024 * 1024
VMEM_LIMIT = V7X_VMEM_BYTES * 7 // 8

COND_ROWS = 16
ADA_TN = 1536
ROW_TILE = 512
CONV_ROWS = GRID_W
CONV_PAD = 16
NORM_ROWS = 256
MOE_TM = 512
BLK_UNUSED, BLK_HALF, BLK_FULL = 0, 1, 2
DISPATCH_TM = 512


def _rmsnorm(x, g):
    return x * lax.rsqrt(jnp.mean(x * x, axis=-1, keepdims=True) + EPS) * g


def _group_mean(x, m_ref):
    hi = x.astype(BF16)
    lo = (x - hi.astype(F32)).astype(BF16)
    m = m_ref[...]
    return (jnp.dot(hi, m, preferred_element_type=F32) + jnp.dot(lo, m, preferred_element_type=F32))


def _layernorm(x, g, b, m_ref):
    xc = x - _group_mean(x, m_ref)
    var = _group_mean(xc * xc, m_ref)
    return xc * lax.rsqrt(var + EPS) * g + b


def _ada_kernel(c_ref, w_ref, b_ref, o_ref):
    c = c_ref[...]
    cond = (c * jax.nn.sigmoid(c)).astype(BF16)
    o_ref[...] = jnp.dot(cond, w_ref[...].astype(BF16), preferred_element_type=F32) + b_ref[...]


def _ada_mod(cond_rows, w_ada, b_ada):
    n_col = w_ada.shape[-1]
    return pl.pallas_call(
        _ada_kernel,
        grid=(DEPTH, n_col // ADA_TN),
        in_specs=[
            pl.BlockSpec((COND_ROWS, D_MODEL), lambda l, j: (0, 0)),
            pl.BlockSpec((None, D_MODEL, ADA_TN), lambda l, j: (l, 0, j)),
            pl.BlockSpec((None, 1, ADA_TN), lambda l, j: (l, 0, j)),
        ],
        out_specs=pl.BlockSpec((None, COND_ROWS, ADA_TN), lambda l, j: (l, 0, j)),
        out_shape=jax.ShapeDtypeStruct((DEPTH, COND_ROWS, n_col), F32),
        name="ada_mod",
    )(cond_rows, w_ada, b_ada.reshape(DEPTH, 1, n_col))


def _token_rows(t, count=1):
    return pl.ds(pl.multiple_of(t * TOKEN_ROWS, TOKEN_ROWS), count * TOKEN_ROWS)


def _store_token_tiles(ref, x):
    n = x.shape[0]
    for c in range(TOKEN_ROWS):
        ref[pl.ds(c, n, stride=TOKEN_ROWS), :] = x[:, c * V7X_LANES:(c + 1) * V7X_LANES]


def _load_token_tiles(ref, n):
    return [ref[pl.ds(c, n, stride=TOKEN_ROWS), :] for c in range(TOKEN_ROWS)]


def _combine_experts(dest_cur, dest_nxt, pr_ref, ybuf, rows, sems):
    i = pl.program_id(0)
    n_steps = pl.num_programs(0)
    tm = rows.shape[2] // TOKEN_ROWS
    slot = i % 2

    def row_copy(dref, j, k, slot_):
        return pltpu.make_async_copy(ybuf.at[_token_rows(dref[0, k * tm + j])],
                                     rows.at[slot_, k, _token_rows(j)], sems.at[slot_])

    def issue(dref, slot_):
        def body(j, carry):
            for k in range(TOP_K):
                row_copy(dref, j, k, slot_).start(priority=k % DMA_PRIORITIES)
            return carry

        lax.fori_loop(0, tm, body, 0, unroll=4)

    @pl.when(i == 0)
    def _():
        issue(dest_cur, 0)

    @pl.when(i + 1 < n_steps)
    def _():
        issue(dest_nxt, 1 - slot)

    for k in range(TOP_K):
        pltpu.make_async_copy(ybuf.at[_token_rows(0, tm)], rows.at[slot, k], sems.at[slot]).wait()
    experts = [_load_token_tiles(rows.at[slot, k], tm) for k in range(TOP_K)]
    pad = jnp.zeros((V7X_LANES - pr_ref.shape[0], tm), F32)
    probs = jnp.concatenate([pr_ref[...], pad], axis=0).T
    chunks = []
    for c in range(TOKEN_ROWS):
        acc = probs[:, 0:1] * experts[0][c]
        for k in range(1, TOP_K):
            acc = acc + probs[:, k:k + 1] * experts[k][c]
        chunks.append(acc)
    return jnp.concatenate(chunks, axis=1)


def _inproj_kernel(*refs, has_res):
    if has_res:
        (dest_cur, dest_nxt, x_ref, pr_ref, ybuf, modp_ref, mod_ref, g_ref, w_ref,
         xo_ref, p_ref, rows, sems) = refs
        x = x_ref[...] + modp_ref[5:6, :] * _combine_experts(dest_cur, dest_nxt, pr_ref, ybuf, rows, sems)
        xo_ref[...] = x
    else:
        x_ref, mod_ref, g_ref, w_ref, p_ref = refs
        x = x_ref[...]
    hn = _rmsnorm(x, g_ref[...]) * (1.0 + mod_ref[1:2, :]) + mod_ref[0:1, :]
    p_ref[...] = jnp.dot(hn.astype(BF16), w_ref[...], preferred_element_type=F32)


def _mod_spec(mod, seq, tm):
    if mod.shape[0] == 1:
        return pl.BlockSpec((None, 6, D_MODEL), lambda i: (0, 0, 0))
    return pl.BlockSpec((None, 6, D_MODEL), lambda i: ((i * tm) // seq, 0, 0))


def _per_tile(dest, tm):
    n_steps = dest.shape[1] // tm
    return dest.reshape(TOP_K, n_steps, tm).transpose(1, 0, 2).reshape(n_steps, 1, TOP_K * tm)


def _combine_operands(dest, probs, y_buf, n, tm):
    n_steps = n // tm
    dest3 = _per_tile(dest, tm)
    smem = functools.partial(pl.BlockSpec, (None, 1, tm * TOP_K), memory_space=pltpu.SMEM)
    args = [dest3, dest3, probs, y_buf]
    specs = [smem(lambda i: (i, 0, 0)),
             smem(lambda i: (jnp.minimum(i + 1, n_steps - 1), 0, 0)),
             pl.BlockSpec((2 * TOP_K, tm), lambda i: (0, i)),
             pl.BlockSpec(memory_space=pl.ANY)]
    scratch = [pltpu.VMEM((2, TOP_K, tm * TOKEN_ROWS, V7X_LANES), F32), pltpu.SemaphoreType.DMA((2,))]
    return args, specs, scratch


def _in_proj(x, route, mod_prev, mod, g, w_bf16, seq):
    n = x.shape[0]
    tm = min(ROW_TILE, seq)
    row = pl.BlockSpec((tm, D_MODEL), lambda i: (i, 0))
    has_res = route is not None
    args, specs, scratch = [], [], []
    if has_res:
        args, specs, scratch = _combine_operands(*route, n, tm)
        args = args[:2] + [x] + args[2:] + [mod_prev]
        specs = specs[:2] + [row] + specs[2:] + [_mod_spec(mod_prev, seq, tm)]
    else:
        args, specs = [x], [row]
    args += [mod, g.reshape(1, D_MODEL), w_bf16]
    specs += [_mod_spec(mod, seq, tm),
              pl.BlockSpec((1, D_MODEL), lambda i: (0, 0)),
              pl.BlockSpec((D_MODEL, N_IN), lambda i: (0, 0))]
    proj_shape = jax.ShapeDtypeStruct((n, N_IN), F32)
    proj_spec = pl.BlockSpec((tm, N_IN), lambda i: (i, 0))
    if has_res:
        out_shape = (jax.ShapeDtypeStruct((n, D_MODEL), F32), proj_shape)
        out_specs = (row, proj_spec)
    else:
        out_shape, out_specs = proj_shape, proj_spec
    out = pl.pallas_call(
        functools.partial(_inproj_kernel, has_res=has_res),
        grid=(n // tm,),
        in_specs=specs,
        out_specs=out_specs,
        out_shape=out_shape,
        scratch_shapes=scratch,
        compiler_params=pltpu.CompilerParams(dimension_semantics=("arbitrary",), vmem_limit_bytes=VMEM_LIMIT),
        name="in_proj",
    )(*args)
    return out if has_res else (x, out)


def _group_scan(a, b, reverse):
    shape = a.shape
    grouped = (shape[0] // V7X_SUBLANES, V7X_SUBLANES, shape[1])
    a, b = a.reshape(grouped), b.reshape(grouped)
    ri = lax.broadcasted_iota(jnp.int32, grouped, 1)
    for d in (1, 2, 4):
        shift = V7X_SUBLANES - d if reverse else d
        keep = ri < V7X_SUBLANES - d if reverse else ri >= d
        ra, rb = pltpu.roll(a, shift, 1), pltpu.roll(b, shift, 1)
        b = a * jnp.where(keep, rb, 0.0) + b
        a = a * jnp.where(keep, ra, 1.0)
    return a.reshape(shape), b.reshape(shape)


def _mixer_kernel(s0, s1, s2, caw, cab, nag, nab, cbw, cbb, wg, bg, lam, h0, sng, snb, sw, sbias, cdw,
                  m_head, y_ref, st_ref, pad_s, af_s, bf_s, ab_s, bb_s, *, seq, on_grid):
    mixer = pl.program_id(1)
    n_conv = seq // CONV_ROWS
    win_rows = CONV_ROWS + 2 * CONV_PAD

    def rows_at(c, size):
        return pl.ds(pl.multiple_of(c * size, size), size)

    def fill_padded(fn):
        zeros = jnp.zeros((CONV_PAD, W_GRP), F32)
        pad_s[0:CONV_PAD, :] = zeros
        pad_s[CONV_PAD + seq:2 * CONV_PAD + seq, :] = zeros

        def body(c, carry):
            dst = pl.ds(pl.multiple_of(c * CONV_ROWS + CONV_PAD, V7X_SUBLANES), CONV_ROWS)
            pad_s[dst, :] = fn(rows_at(c, CONV_ROWS))
            return carry

        lax.fori_loop(0, n_conv, body, 0)

    def conv_window(win, w_ref, taps, pad_l):
        acc = jnp.zeros((CONV_ROWS, W_GRP), F32)
        for mis in range(V7X_SUBLANES):
            starts = [(k, CONV_PAD - pad_l + k) for k in range(taps)
                      if (CONV_PAD - pad_l + k) % V7X_SUBLANES == mis]
            if not starts:
                continue
            shifted = pltpu.roll(win, win_rows - mis, 0) if mis else win
            for k, start in starts:
                acc = acc + w_ref[k:k + 1, :] * shifted[start - mis:start - mis + CONV_ROWS, :]
        return acc

    def conv_rows(c, w_ref, taps, pad_l):
        win = pad_s[pl.ds(pl.multiple_of(c * CONV_ROWS, CONV_ROWS), win_rows), :]
        return conv_window(win, w_ref, taps, pad_l)

    @pl.when(mixer == 0)
    def _conformer():
        def glu(rows):
            return s0[rows, :] * jax.nn.sigmoid(s1[rows, :])

        if not on_grid:
            fill_padded(glu)

        def body(c, carry):
            if on_grid:
                edge = jnp.zeros((CONV_PAD, W_GRP), F32)
                win = jnp.concatenate([edge, glu(rows_at(c, CONV_ROWS)), edge], axis=0)
                z = conv_window(win, caw, CONV_A, CONV_A // 2)
            else:
                z = conv_rows(c, caw, CONV_A, CONV_A // 2)
            af_s[rows_at(c, CONV_ROWS), :] = z + cab[...]
            return carry

        lax.fori_loop(0, n_conv, body, 0)

        def norm(c, carry):
            rows = rows_at(c, NORM_ROWS)
            z = _layernorm(af_s[rows, :], nag[...], nab[...], m_head)
            y_ref[rows, :] = (z * jax.nn.sigmoid(z)).astype(BF16)
            return carry

        lax.fori_loop(0, seq // NORM_ROWS, norm, 0, unroll=2)

    @pl.when(mixer == 1)
    def _rglru():
        fill_padded(lambda rows: s0[rows, :])
        lam_v = lam[...]
        softplus_neg = jnp.maximum(-lam_v, 0.0) + jnp.log1p(jnp.exp(-jnp.abs(lam_v)))

        def gates(c, carry):
            rows = rows_at(c, CONV_ROWS)
            xc = conv_rows(c, cbw, CONV_B, 2) + cbb[...]
            gt = jax.nn.sigmoid(jnp.dot(xc.astype(BF16), wg[...], preferred_element_type=F32) + bg[...])
            for d, (a_s, b_s) in enumerate(((af_s, bf_s), (ab_s, bb_s))):
                r_gate = gt[:, (2 * d) * W_GRP:(2 * d + 1) * W_GRP]
                i_gate = gt[:, (2 * d + 1) * W_GRP:(2 * d + 2) * W_GRP]
                log_a = -LRU_C * r_gate * softplus_neg[d:d + 1, :]
                a = jnp.exp(log_a)
                b = jnp.sqrt(jnp.maximum(-jnp.tanh(log_a) * (a * a + 1.0), 0.0)) * (i_gate * xc)
                a, b = _group_scan(a, b, reverse=(d == 1))
                a_s[rows, :] = a
                b_s[rows, :] = b
            return carry

        lax.fori_loop(0, n_conv, gates, 0, unroll=2)

        n_grp = seq // V7X_SUBLANES

        def chain(g, carry):
            cf, cb = carry
            rf = rows_at(g, V7X_SUBLANES)
            rb = rows_at(n_grp - 1 - g, V7X_SUBLANES)
            hf = af_s[rf, :] * cf + bf_s[rf, :]
            bf_s[rf, :] = hf
            hb = ab_s[rb, :] * cb + bb_s[rb, :]
            bb_s[rb, :] = hb
            cf = jnp.broadcast_to(hf[V7X_SUBLANES - 1:V7X_SUBLANES, :], (V7X_SUBLANES, W_GRP))
            cb = jnp.broadcast_to(hb[0:1, :], (V7X_SUBLANES, W_GRP))
            return cf, cb

        init = (jnp.broadcast_to(h0[0:1, :], (V7X_SUBLANES, W_GRP)),
                jnp.broadcast_to(h0[1:2, :], (V7X_SUBLANES, W_GRP)))
        cf, cb = lax.fori_loop(0, n_grp, chain, init, unroll=4)
        st_ref[0:1, :] = cf[0:1, :]
        st_ref[1:2, :] = cb[0:1, :]

        def out(c, carry):
            rows = rows_at(c, CONV_ROWS)
            y_ref[rows, :] = (jax.nn.gelu(s1[rows, :]) * (bf_s[rows, :] + bb_s[rows, :])).astype(BF16)
            return carry

        lax.fori_loop(0, n_conv, out, 0)

    @pl.when(mixer == 2)
    def _sgu():
        lane = lax.broadcasted_iota(jnp.int32, (CHUNK, W_GRP), 1)

        def body(n, carry):
            rows = rows_at(n, CHUNK)
            v = s1[rows, :]
            vc = v - jnp.mean(v, axis=-1, keepdims=True)
            var = jnp.mean(vc * vc, axis=-1, keepdims=True)
            v = (vc * lax.rsqrt(var + EPS) * sng[...] + snb[...]).astype(BF16)
            s = sbias[...]
            for h in range(HEADS):
                sh = jnp.dot(sw[h], v, preferred_element_type=F32)
                s = s + jnp.where((lane >= h * HD) & (lane < (h + 1) * HD), sh, 0.0)
            y_ref[rows, :] = (s0[rows, :] * s).astype(BF16)
            return carry

        lax.fori_loop(0, seq // CHUNK, body, 0, unroll=4)

    @pl.when(mixer == 3)
    def _gated_conv():
        fill_padded(lambda rows: s1[rows, :] * s2[rows, :])

        def body(c, carry):
            rows = rows_at(c, CONV_ROWS)
            y_ref[rows, :] = (s0[rows, :] * conv_rows(c, cdw, CONV_D, CONV_D // 2)).astype(BF16)
            return carry

        lax.fori_loop(0, n_conv, body, 0)


def _mixers(proj, h0, lw, batch, seq, on_grid):
    proj3 = proj.reshape(batch, seq, N_IN)
    slab = (None, seq, W_GRP)

    def const(shape):
        return pl.BlockSpec(shape, lambda b, m: (0,) * len(shape))

    in_specs = [
        pl.BlockSpec(slab, lambda b, m: (b, 0, 2 * m)),
        pl.BlockSpec(slab, lambda b, m: (b, 0, 2 * m + 1)),
        pl.BlockSpec(slab, lambda b, m: (b, 0, 8)),
        const((CONV_A, W_GRP)), const((1, W_GRP)), const((1, W_GRP)), const((1, W_GRP)),
        const((CONV_B, W_GRP)), const((1, W_GRP)),
        const((W_GRP, 4 * W_GRP)), const((1, 4 * W_GRP)), const((2, W_GRP)),
        pl.BlockSpec((None, 2, W_GRP), lambda b, m: (b, 0, 0)),
        const((1, W_GRP)), const((1, W_GRP)), const((HEADS, CHUNK, CHUNK)), const((CHUNK, W_GRP)),
        const((CONV_D, W_GRP)), const((W_GRP, W_GRP)),
    ]
    y, st = pl.pallas_call(
        functools.partial(_mixer_kernel, seq=seq, on_grid=on_grid),
        grid=(batch, N_MIXERS),
        in_specs=in_specs,
        out_specs=(pl.BlockSpec(slab, lambda b, m: (b, 0, m)),
                   pl.BlockSpec((None, 2, W_GRP), lambda b, m: (b, 0, 0))),
        out_shape=(jax.ShapeDtypeStruct((batch, seq, D_MODEL), BF16),
                   jax.ShapeDtypeStruct((batch, 2, W_GRP), F32)),
        scratch_shapes=[pltpu.VMEM((seq + 2 * CONV_PAD, W_GRP), F32)] + [pltpu.VMEM((seq, W_GRP), F32)] * 4,
        compiler_params=pltpu.CompilerParams(dimension_semantics=("arbitrary", "arbitrary"),
                                             vmem_limit_bytes=VMEM_LIMIT),
        name="mixers",
    )(proj3, proj3, proj3, lw["conv_a_w"], lw["conv_a_b"], lw["norm_a_g"], lw["norm_a_b"],
      lw["conv_b_w"], lw["conv_b_b"], lw["gate_w"], lw["gate_b"], lw["lru_lam"], h0,
      lw["sgu_norm_g"], lw["sgu_norm_b"], lw["sgu_w"], lw["sgu_bias"], lw["conv_d_w"],
      lw["m_head"])
    return y.reshape(batch * seq, D_MODEL), st


def _outproj_kernel(y_ref, x_ref, mod_ref, g_ref, wo_ref, wr_ref, br_ref, tri_ref, cin_ref,
                    xm_ref, hn_ref, ei_ref, pr_ref, cnt_ref, seen_s):
    tm = x_ref.shape[0]
    e = N_EXPERTS

    @pl.when(pl.program_id(0) == 0)
    def _():
        seen_s[...] = cin_ref[...]

    y = jnp.dot(y_ref[...], wo_ref[...], preferred_element_type=F32)
    x = x_ref[...] + mod_ref[2:3, :] * y
    xm_ref[...] = x
    hn = _rmsnorm(x, g_ref[...]) * (1.0 + mod_ref[4:5, :]) + mod_ref[3:4, :]
    _store_token_tiles(hn_ref, hn)

    h1 = hn.astype(BF16)
    r1 = hn - h1.astype(F32)
    h2 = r1.astype(BF16)
    h3 = (r1 - h2.astype(F32)).astype(BF16)
    w_t = wr_ref[...]
    contract_features = (((1,), (1,)), ((), ()))
    p1 = lax.dot_general(w_t, h1, contract_features, preferred_element_type=F32)
    p2 = lax.dot_general(w_t, h2, contract_features, preferred_element_type=F32)
    p3 = lax.dot_general(w_t, h3, contract_features, preferred_element_type=F32)
    logits = (p1[2 * e:3 * e] + p2[e:2 * e] + p3[0:e] + p1[e:2 * e] + p2[0:e] + p1[0:e]) + br_ref[...]

    expert = lax.broadcasted_iota(jnp.int32, (e, tm), 0)
    beats = jnp.zeros((e, tm), F32)
    for other in range(e):
        lo = logits[other:other + 1, :]
        ahead = (lo > logits) | ((lo == logits) & (expert > other))
        beats = beats + jnp.where(ahead, 1.0, 0.0)

    chosen = jnp.where(beats < float(TOP_K), 1.0, 0.0)
    earlier = jnp.dot(chosen.astype(BF16), tri_ref[...], preferred_element_type=F32) + seen_s[...]
    seen_s[...] = earlier[:, tm - 1:tm] + chosen[:, tm - 1:tm]
    cnt_ref[...] = seen_s[...]

    def pick(k, values):
        return jnp.sum(jnp.where(beats == float(k), values, 0.0), axis=0, keepdims=True)

    expert_f = expert.astype(F32)
    vals = [pick(k, logits) for k in range(TOP_K)]
    exps = [jnp.exp(v - vals[0]) for v in vals]
    den = exps[0] + exps[1] + exps[2] + exps[3]
    out_row = lax.broadcasted_iota(jnp.int32, (2 * TOP_K, tm), 0)
    ei = jnp.zeros((2 * TOP_K, tm), F32)
    pr = jnp.zeros((2 * TOP_K, tm), F32)
    for k in range(TOP_K):
        ei = jnp.where(out_row == k, pick(k, expert_f), ei)
        ei = jnp.where(out_row == TOP_K + k, pick(k, earlier), ei)
        pr = jnp.where(out_row == k, exps[k] / den, pr)
    ei_ref[...] = ei.astype(jnp.int32)
    pr_ref[...] = pr


def _out_proj(y, x, mod, g, wo_bf16, w_router, b_router, seen, seq):
    n = x.shape[0]
    tm = min(ROW_TILE, seq)
    row = pl.BlockSpec((tm, D_MODEL), lambda i: (i, 0))
    route = pl.BlockSpec((2 * TOP_K, tm), lambda i: (0, i))
    cnt = pl.BlockSpec((N_EXPERTS, 1), lambda i: (0, 0))
    before = jnp.tri(tm, k=-1, dtype=BF16).T
    w1 = w_router.astype(BF16)
    w2 = (w_router - w1.astype(F32)).astype(BF16)
    w3 = (w_router - w1.astype(F32) - w2.astype(F32)).astype(BF16)
    w_t = jnp.concatenate([w1.T, w2.T, w3.T, jnp.zeros_like(w1.T)], axis=0)
    return pl.pallas_call(
        _outproj_kernel,
        grid=(n // tm,),
        in_specs=[row, row, _mod_spec(mod, seq, tm),
                  pl.BlockSpec((1, D_MODEL), lambda i: (0, 0)),
                  pl.BlockSpec((D_MODEL, D_MODEL), lambda i: (0, 0)),
                  pl.BlockSpec((4 * N_EXPERTS, D_MODEL), lambda i: (0, 0)),
                  cnt, pl.BlockSpec((tm, tm), lambda i: (0, 0)), cnt],
        out_specs=(row, pl.BlockSpec((tm * TOKEN_ROWS, V7X_LANES), lambda i: (i, 0)), route, route, cnt),
        out_shape=(jax.ShapeDtypeStruct((n, D_MODEL), F32),
                   jax.ShapeDtypeStruct((n * TOKEN_ROWS, V7X_LANES), F32),
                   jax.ShapeDtypeStruct((2 * TOP_K, n), jnp.int32),
                   jax.ShapeDtypeStruct((2 * TOP_K, n), F32),
                   jax.ShapeDtypeStruct((N_EXPERTS, 1), F32)),
        scratch_shapes=[pltpu.VMEM((N_EXPERTS, 1), F32)],
        compiler_params=pltpu.CompilerParams(dimension_semantics=("arbitrary",)),
        name="out_proj",
    )(y, x, mod, g.reshape(1, D_MODEL), wo_bf16, w_t, b_router.reshape(N_EXPERTS, 1), before, seen)


def _dispatch_kernel(zero_blk, zero_on, dest_ref, hn_p, hn_s, xbuf, zeros_s, sem, zsem, *, tiles_p):
    i = pl.program_id(0)

    @pl.when(i == 0)
    def _():
        zeros_s[...] = jnp.zeros_like(zeros_s)

        def fill(j):
            return pltpu.make_async_copy(zeros_s, xbuf.at[_token_rows(zero_blk[j] * MOE_TM, MOE_TM)], zsem)

        for j in range(zero_blk.shape[0]):
            @pl.when(zero_on[j] == 1)
            def _():
                fill(j).start()

        for j in range(zero_blk.shape[0]):
            @pl.when(zero_on[j] == 1)
            def _():
                fill(j).wait()

    def issue(src):
        def body(j, carry):
            row = src.at[_token_rows(j)]
            for k in range(TOP_K):
                pltpu.make_async_copy(row, xbuf.at[_token_rows(dest_ref[0, k * DISPATCH_TM + j])],
                                      sem).start(priority=k % DMA_PRIORITIES)
            return carry

        lax.fori_loop(0, DISPATCH_TM, body, 0, unroll=4)

    @pl.when(i < tiles_p)
    def _():
        issue(hn_p)

    @pl.when(i >= tiles_p)
    def _():
        issue(hn_s)

    for _ in range(TOP_K):
        pltpu.make_async_copy(hn_s, xbuf.at[_token_rows(0, DISPATCH_TM)], sem).wait()


def _dispatch(dest, zero_blk, zero_on, hn_p, hn_s, n_pad):
    n_p, n_s = hn_p.shape[0] // TOKEN_ROWS, hn_s.shape[0] // TOKEN_ROWS
    tiles_p = n_p // DISPATCH_TM
    n_steps = (n_p + n_s) // DISPATCH_TM
    tile = (DISPATCH_TM * TOKEN_ROWS, V7X_LANES)
    grid_spec = pltpu.PrefetchScalarGridSpec(
        num_scalar_prefetch=2,
        grid=(n_steps,),
        in_specs=[pl.BlockSpec((None, 1, DISPATCH_TM * TOP_K), lambda i, zb, zo: (i, 0, 0),
                               memory_space=pltpu.SMEM),
                  pl.BlockSpec(tile, lambda i, zb, zo: (jnp.minimum(i, tiles_p - 1), 0)),
                  pl.BlockSpec(tile, lambda i, zb, zo: (jnp.maximum(i - tiles_p, 0), 0))],
        out_specs=pl.BlockSpec(memory_space=pl.ANY),
        scratch_shapes=[pltpu.VMEM((MOE_TM * TOKEN_ROWS, V7X_LANES), F32), pltpu.SemaphoreType.DMA(()),
                        pltpu.SemaphoreType.DMA(())],
    )
    return pl.pallas_call(
        functools.partial(_dispatch_kernel, tiles_p=tiles_p),
        grid_spec=grid_spec,
        out_shape=jax.ShapeDtypeStruct((n_pad * TOKEN_ROWS, V7X_LANES), F32),
        compiler_params=pltpu.CompilerParams(dimension_semantics=("arbitrary",)),
        name="moe_dispatch",
    )(zero_blk, zero_on, _per_tile(dest, DISPATCH_TM), hn_p, hn_s)


def _moe_kernel(blk_e, blk_new, blk_on, blk_slot, blk_next, x_ref, wgu_hbm, bgu_ref, wdn_hbm, bdn_ref, o_ref,
                wgu_f, wdn_f, wgu_s, wdn_s, sems, *, layer):
    i = pl.program_id(0)

    def fetch(expert, slot):
        return (pltpu.make_async_copy(wgu_hbm.at[layer, expert], wgu_f.at[slot], sems.at[0, slot]),
                pltpu.make_async_copy(wdn_hbm.at[layer, expert], wdn_f.at[slot], sems.at[1, slot]))

    @pl.when(i == 0)
    def _():
        for copy in fetch(blk_e[0], 0):
            copy.start()

    @pl.when(blk_on[i] == BLK_UNUSED)
    def _():
        o_ref[...] = jnp.zeros_like(o_ref)

    @pl.when(blk_on[i] != BLK_UNUSED)
    def _():
        @pl.when(blk_new[i] == 1)
        def _():
            slot = blk_slot[i]
            for copy in fetch(blk_e[i], slot):
                copy.wait()

            @pl.when(blk_next[i] >= 0)
            def _():
                for copy in fetch(blk_next[i], 1 - slot):
                    copy.start()

            wgu_s[...] = wgu_f[slot].astype(BF16)
            wdn_s[...] = wdn_f[slot].astype(BF16)

        def expert_rows(n):
            x = jnp.concatenate([c.astype(BF16) for c in _load_token_tiles(x_ref, n)], axis=1)
            gu = jnp.dot(x, wgu_s[...], preferred_element_type=F32) + bgu_ref[...]
            g = jnp.minimum(gu[:, :D_FF], SWIGLU_LIMIT)
            u = jnp.clip(gu[:, D_FF:], -SWIGLU_LIMIT, SWIGLU_LIMIT)
            act = (u + 1.0) * (g * jax.nn.sigmoid(SWIGLU_ALPHA * g))
            _store_token_tiles(o_ref, jnp.dot(act.astype(BF16), wdn_s[...], preferred_element_type=F32)
                               + bdn_ref[...])

        @pl.when(blk_on[i] == BLK_FULL)
        def _():
            expert_rows(MOE_TM)

        @pl.when(blk_on[i] == BLK_HALF)
        def _():
            expert_rows(MOE_TM // 2)
            o_ref[_token_rows(MOE_TM // 2, MOE_TM // 2), :] = jnp.zeros(
                (MOE_TM // 2 * TOKEN_ROWS, V7X_LANES), F32)


def _moe_blocks(x_buf, blk, layer, w_gu, b_gu, w_dn, b_dn):
    n_pad = x_buf.shape[0] // TOKEN_ROWS
    n_blk = n_pad // MOE_TM
    tile = pl.BlockSpec((MOE_TM * TOKEN_ROWS, V7X_LANES), lambda i, e, *_: (i, 0))
    grid_spec = pltpu.PrefetchScalarGridSpec(
        num_scalar_prefetch=len(blk),
        grid=(n_blk,),
        in_specs=[
            tile,
            pl.BlockSpec(memory_space=pl.ANY),
            pl.BlockSpec((None, 1, 2 * D_FF), lambda i, e, *_: (e[i], 0, 0)),
            pl.BlockSpec(memory_space=pl.ANY),
            pl.BlockSpec((None, 1, D_MODEL), lambda i, e, *_: (e[i], 0, 0)),
        ],
        out_specs=tile,
        scratch_shapes=[pltpu.VMEM((2, D_MODEL, 2 * D_FF), F32), pltpu.VMEM((2, D_FF, D_MODEL), F32),
                        pltpu.VMEM((D_MODEL, 2 * D_FF), BF16), pltpu.VMEM((D_FF, D_MODEL), BF16),
                        pltpu.SemaphoreType.DMA((2, 2))],
    )
    return pl.pallas_call(
        functools.partial(_moe_kernel, layer=layer),
        grid_spec=grid_spec,
        out_shape=jax.ShapeDtypeStruct((n_pad * TOKEN_ROWS, V7X_LANES), F32),
        compiler_params=pltpu.CompilerParams(dimension_semantics=("arbitrary",),
                                             vmem_limit_bytes=VMEM_LIMIT),
        name="moe_experts",
    )(*blk, x_buf, w_gu, b_gu.reshape(N_EXPERTS, 1, 2 * D_FF), w_dn, b_dn.reshape(N_EXPERTS, 1, D_MODEL))


def _moe(hn_p, hn_s, ei_p, ei_s, counts, layer, w_gu, b_gu, w_dn, b_dn):
    n_asg = (ei_p.shape[1] + ei_s.shape[1]) * TOP_K
    n_pad = (n_asg + N_EXPERTS * (MOE_TM - 1) + MOE_TM - 1) // MOE_TM * MOE_TM
    n_blk = n_pad // MOE_TM
    counts = counts.reshape(N_EXPERTS).astype(jnp.int32)
    padded = (counts + MOE_TM - 1) // MOE_TM * MOE_TM
    pad_end = jnp.cumsum(padded)
    pad_start = pad_end - padded
    blk_row = jnp.arange(n_blk, dtype=jnp.int32) * MOE_TM
    blk_e = jnp.minimum(jnp.sum((blk_row[:, None] >= pad_end[None, :]).astype(jnp.int32), axis=1),
                        N_EXPERTS - 1)
    blk_tokens = jnp.clip(counts[blk_e] - (blk_row - pad_start[blk_e]), 0, MOE_TM)
    blk_on = jnp.where(blk_row >= pad_end[-1], BLK_UNUSED,
                       jnp.where(blk_tokens <= MOE_TM // 2, BLK_HALF, BLK_FULL)).astype(jnp.int32)
    blk_new = jnp.concatenate([jnp.ones((1,), jnp.int32), (blk_e[1:] != blk_e[:-1]).astype(jnp.int32)])
    blk_slot = (jnp.cumsum(blk_new) - 1) % 2
    experts = jnp.arange(N_EXPERTS, dtype=jnp.int32)
    in_use = jnp.where(counts > 0, experts, N_EXPERTS)
    after = jnp.concatenate([lax.cummin(in_use[::-1])[::-1][1:], jnp.full((1,), N_EXPERTS, jnp.int32)])
    blk_next = jnp.where(after[blk_e] < N_EXPERTS, after[blk_e], -1)

    def slots(ei):
        expert, rank = ei[0:TOP_K], ei[TOP_K:2 * TOP_K]
        start = jnp.sum(jnp.where(expert[None] == jnp.arange(N_EXPERTS, dtype=jnp.int32)[:, None, None],
                                  pad_start[:, None, None], 0), axis=0)
        return start + rank

    tail = n_blk - n_asg // MOE_TM
    last_on = (counts % MOE_TM != 0).astype(jnp.int32)
    tail_blk = pad_end[-1] // MOE_TM + jnp.arange(tail, dtype=jnp.int32)
    tail_on = (tail_blk < n_blk).astype(jnp.int32)
    zero_blk = jnp.concatenate([(pad_end // MOE_TM - 1) * last_on, tail_blk * tail_on])
    zero_on = jnp.concatenate([last_on, tail_on])

    dest_p, dest_s = slots(ei_p), slots(ei_s)
    x_buf = _dispatch(jnp.concatenate([dest_p, dest_s], axis=1), zero_blk, zero_on, hn_p, hn_s, n_pad)
    y_buf = _moe_blocks(x_buf, (blk_e, blk_new, blk_on, blk_slot.astype(jnp.int32), blk_next.astype(jnp.int32)),
                        layer, w_gu, b_gu, w_dn, b_dn)
    return dest_p, dest_s, y_buf


def _final_kernel(dest_cur, dest_nxt, x_ref, pr_ref, ybuf, mod_ref, g_ref, o_ref, rows, sems):
    x = x_ref[...] + mod_ref[5:6, :] * _combine_experts(dest_cur, dest_nxt, pr_ref, ybuf, rows, sems)
    o_ref[...] = _rmsnorm(x, g_ref[...])


def _final_norm(x, route, mod, g, seq):
    n = x.shape[0]
    tm = min(ROW_TILE, seq)
    row = pl.BlockSpec((tm, D_MODEL), lambda i: (i, 0))
    args, specs, scratch = _combine_operands(*route, n, tm)
    return pl.pallas_call(
        _final_kernel,
        grid=(n // tm,),
        in_specs=specs[:2] + [row] + specs[2:] + [_mod_spec(mod, seq, tm),
                                                  pl.BlockSpec((1, D_MODEL), lambda i: (0, 0))],
        out_specs=row,
        out_shape=jax.ShapeDtypeStruct((n, D_MODEL), F32),
        scratch_shapes=scratch,
        compiler_params=pltpu.CompilerParams(dimension_semantics=("arbitrary",), vmem_limit_bytes=VMEM_LIMIT),
        name="final_norm",
    )(*args[:2], x, *args[2:], mod, g.reshape(1, D_MODEL))


def _block_diag(w):
    eye = jnp.eye(HEADS, dtype=w.dtype)
    return (eye[:, None, :, None] * w[:, :, None, :]).reshape(W_GRP, W_GRP)


def _layer_weights(l, p):
    gate_w = jnp.concatenate([_block_diag(p["lru_wr"][l, 0]), _block_diag(p["lru_wi"][l, 0]),
                              _block_diag(p["lru_wr"][l, 1]), _block_diag(p["lru_wi"][l, 1])], axis=1)
    gate_b = jnp.concatenate([p["lru_br"][l, 0], p["lru_bi"][l, 0], p["lru_br"][l, 1], p["lru_bi"][l, 1]])
    head_of = jnp.arange(W_GRP) // HD
    row = lambda v: v.reshape(1, W_GRP)
    return dict(
        conv_a_w=p["conv_a_w"][l], conv_a_b=row(p["conv_a_b"][l]),
        norm_a_g=row(p["norm_a_g"][l]), norm_a_b=row(p["norm_a_b"][l]),
        conv_b_w=p["conv_b_w"][l], conv_b_b=row(p["conv_b_b"][l]),
        gate_w=gate_w.astype(BF16), gate_b=gate_b.reshape(1, 4 * W_GRP), lru_lam=p["lru_lam"][l],
        sgu_norm_g=row(p["sgu_norm_g"][l]), sgu_norm_b=row(p["sgu_norm_b"][l]),
        sgu_w=p["sgu_w"][l].astype(BF16), sgu_bias=jnp.repeat(p["sgu_b"][l].T, HD, axis=1),
        conv_d_w=p["conv_d_w"][l],
        m_head=((head_of[:, None] == head_of[None, :]).astype(F32) / HD).astype(BF16),
    )


def kernel(x_prompt, x_sample, state_rglru, c, c_ctx, w_ada, b_ada, norm1_g, norm2_g, w_in, conv_a_w,
           conv_a_b, norm_a_g, norm_a_b, conv_b_w, conv_b_b, lru_wr, lru_br, lru_wi, lru_bi, lru_lam,
           sgu_norm_g, sgu_norm_b, sgu_w, sgu_b, conv_d_w, w_out, w_router, b_router, w_gu, b_gu, w_dn,
           b_dn, final_g):
    p = dict(conv_a_w=conv_a_w, conv_a_b=conv_a_b, norm_a_g=norm_a_g, norm_a_b=norm_a_b,
             conv_b_w=conv_b_w, conv_b_b=conv_b_b, lru_wr=lru_wr, lru_br=lru_br, lru_wi=lru_wi,
             lru_bi=lru_bi, lru_lam=lru_lam, sgu_norm_g=sgu_norm_g, sgu_norm_b=sgu_norm_b,
             sgu_w=sgu_w, sgu_b=sgu_b, conv_d_w=conv_d_w)
    bp, tp, _ = x_prompt.shape
    bs, ts, _ = x_sample.shape
    n_p, n_s = bp * tp, bs * ts

    cond_rows = jnp.zeros((COND_ROWS, D_MODEL), F32).at[0].set(c_ctx).at[1:1 + bs].set(c)
    mod = _ada_mod(cond_rows, w_ada, b_ada).reshape(DEPTH, COND_ROWS, 6, D_MODEL)

    xp = x_prompt.reshape(n_p, D_MODEL)
    xs = x_sample.reshape(n_s, D_MODEL)
    h0_ctx = jnp.zeros((bp, 2, W_GRP), F32)
    route_p = route_s = mod_p_prev = mod_s_prev = None
    no_tokens_seen = jnp.zeros((N_EXPERTS, 1), F32)
    states = []
    for l in range(DEPTH):
        lw = _layer_weights(l, p)
        mod_p, mod_s = mod[l, 0:1], mod[l, 1:1 + bs]
        w_in_l = w_in[l].astype(BF16)
        w_out_l = w_out[l].astype(BF16)
        xp, proj_p = _in_proj(xp, route_p, mod_p_prev, mod_p, norm1_g[l], w_in_l, tp)
        xs, proj_s = _in_proj(xs, route_s, mod_s_prev, mod_s, norm1_g[l], w_in_l, ts)
        y_p, st = _mixers(proj_p, h0_ctx, lw, bp, tp, False)
        y_s, _ = _mixers(proj_s, state_rglru[:, l], lw, bs, ts, True)
        states.append(st)
        xp, hn_p, ei_p, pr_p, seen = _out_proj(y_p, xp, mod_p, norm2_g[l], w_out_l, w_router[l],
                                               b_router[l], no_tokens_seen, tp)
        xs, hn_s, ei_s, pr_s, counts = _out_proj(y_s, xs, mod_s, norm2_g[l], w_out_l, w_router[l],
                                                 b_router[l], seen, ts)
        dest_p, dest_s, y_buf = _moe(hn_p, hn_s, ei_p, ei_s, counts, l, w_gu, b_gu[l], w_dn, b_dn[l])
        route_p, route_s = (dest_p, pr_p, y_buf), (dest_s, pr_s, y_buf)
        mod_p_prev, mod_s_prev = mod_p, mod_s
    y_prompt = _final_norm(xp, route_p, mod_p_prev, final_g, tp).reshape(bp, tp, D_MODEL)
    y_sample = _final_norm(xs, route_s, mod_s_prev, final_g, ts).reshape(bs, ts, D_MODEL)
    return y_prompt, y_sample, jnp.stack(states, axis=1)
```

```python
import functools

import jax
import jax.numpy as jnp
from jax import lax
from jax.experimental import pallas as pl
from jax.experimental.pallas import tpu as pltpu

F32 = jnp.float32
BF16 = jnp.bfloat16

D_MODEL = 1024
DEPTH = 2
GRID_W = 64
N_MIXERS = 4
W_GRP = D_MODEL // N_MIXERS
HEADS = 4
HD = W_GRP // HEADS
N_IN = 9 * W_GRP
CONV_A = 31
CONV_B = 4
CONV_D = 3
CHUNK = 128
LRU_C = 8.0
N_EXPERTS = 32
TOP_K = 4
D_FF = D_MODEL
SWIGLU_LIMIT = 7.0
SWIGLU_ALPHA = 1.702
EPS = 1e-6

V7X_SUBLANES = 8
V7X_LANES = 128
TOKEN_ROWS = D_MODEL // V7X_LANES
DMA_PRIORITIES = 2
V7X_VMEM_BYTES = 64 * 1024 * 1024
VMEM_LIMIT = V7X_VMEM_BYTES * 7 // 8

COND_ROWS = 16
ADA_TN = 1536
ROW_TILE = 512
CONV_ROWS = GRID_W
CONV_PAD = 16
NORM_ROWS = 256
MIX_ROWS = 1024
MOE_TM = 512
BLK_UNUSED, BLK_HALF, BLK_FULL = 0, 1, 2
DISPATCH_TM = 512
SLOT_TN = 2048


def _rmsnorm(x, g):
    return x * lax.rsqrt(jnp.mean(x * x, axis=-1, keepdims=True) + EPS) * g


def _group_mean(x, m_ref):
    hi = x.astype(BF16)
    lo = (x - hi.astype(F32)).astype(BF16)
    m = m_ref[...]
    return (jnp.dot(hi, m, preferred_element_type=F32) + jnp.dot(lo, m, preferred_element_type=F32))


def _layernorm(x, g, b, m_ref):
    xc = x - _group_mean(x, m_ref)
    var = _group_mean(xc * xc, m_ref)
    return xc * lax.rsqrt(var + EPS) * g + b


def _ada_kernel(c_ref, w_ref, b_ref, o_ref):
    c = c_ref[...]
    cond = (c * jax.nn.sigmoid(c)).astype(BF16)
    o_ref[...] = jnp.dot(cond, w_ref[...].astype(BF16), preferred_element_type=F32) + b_ref[...]


def _ada_mod(cond_rows, w_ada, b_ada):
    n_col = w_ada.shape[-1]
    return pl.pallas_call(
        _ada_kernel,
        grid=(DEPTH, n_col // ADA_TN),
        in_specs=[
            pl.BlockSpec((COND_ROWS, D_MODEL), lambda l, j: (0, 0)),
            pl.BlockSpec((None, D_MODEL, ADA_TN), lambda l, j: (l, 0, j)),
            pl.BlockSpec((None, 1, ADA_TN), lambda l, j: (l, 0, j)),
        ],
        out_specs=pl.BlockSpec((None, COND_ROWS, ADA_TN), lambda l, j: (l, 0, j)),
        out_shape=jax.ShapeDtypeStruct((DEPTH, COND_ROWS, n_col), F32),
        name="ada_mod",
    )(cond_rows, w_ada, b_ada.reshape(DEPTH, 1, n_col))


def _token_rows(t, count=1):
    return pl.ds(pl.multiple_of(t * TOKEN_ROWS, TOKEN_ROWS), count * TOKEN_ROWS)


def _store_token_tiles(ref, x):
    n = x.shape[0]
    for c in range(TOKEN_ROWS):
        ref[pl.ds(c, n, stride=TOKEN_ROWS), :] = x[:, c * V7X_LANES:(c + 1) * V7X_LANES]


def _load_token_tiles(ref, n):
    return [ref[pl.ds(c, n, stride=TOKEN_ROWS), :] for c in range(TOKEN_ROWS)]


def _combine_experts(dest_cur, dest_nxt, pr_ref, ybuf, rows, sems):
    i = pl.program_id(0)
    n_steps = pl.num_programs(0)
    tm = rows.shape[2] // TOKEN_ROWS
    slot = i % 2

    def row_copy(dref, j, k, slot_):
        return pltpu.make_async_copy(ybuf.at[_token_rows(dref[0, k * tm + j])],
                                     rows.at[slot_, k, _token_rows(j)], sems.at[slot_])

    def issue(dref, slot_):
        def body(j, carry):
            for k in range(TOP_K):
                row_copy(dref, j, k, slot_).start(priority=k % DMA_PRIORITIES)
            return carry

        lax.fori_loop(0, tm, body, 0, unroll=4)

    @pl.when(i == 0)
    def _():
        issue(dest_cur, 0)

    @pl.when(i + 1 < n_steps)
    def _():
        issue(dest_nxt, 1 - slot)

    for k in range(TOP_K):
        pltpu.make_async_copy(ybuf.at[_token_rows(0, tm)], rows.at[slot, k], sems.at[slot]).wait()
    experts = [_load_token_tiles(rows.at[slot, k], tm) for k in range(TOP_K)]
    pad = jnp.zeros((V7X_LANES - pr_ref.shape[0], tm), F32)
    probs = jnp.concatenate([pr_ref[...], pad], axis=0).T
    chunks = []
    for c in range(TOKEN_ROWS):
        acc = probs[:, 0:1] * experts[0][c]
        for k in range(1, TOP_K):
            acc = acc + probs[:, k:k + 1] * experts[k][c]
        chunks.append(acc)
    return jnp.concatenate(chunks, axis=1)


def _inproj_kernel(*refs, has_res):
    if has_res:
        (dest_cur, dest_nxt, x_ref, pr_ref, ybuf, modp_ref, mod_ref, g_ref, w_ref,
         xo_ref, p_ref, rows, sems) = refs
        x = x_ref[...] + modp_ref[5:6, :] * _combine_experts(dest_cur, dest_nxt, pr_ref, ybuf, rows, sems)
        xo_ref[...] = x
    else:
        x_ref, mod_ref, g_ref, w_ref, p_ref = refs
        x = x_ref[...]
    hn = _rmsnorm(x, g_ref[...]) * (1.0 + mod_ref[1:2, :]) + mod_ref[0:1, :]
    p_ref[...] = jnp.dot(hn.astype(BF16), w_ref[...], preferred_element_type=F32)


def _mod_spec(mod, seq, tm):
    if mod.shape[0] == 1:
        return pl.BlockSpec((None, 6, D_MODEL), lambda i: (0, 0, 0))
    return pl.BlockSpec((None, 6, D_MODEL), lambda i: ((i * tm) // seq, 0, 0))


def _per_tile(dest, tm):
    n_steps = dest.shape[1] // tm
    return dest.reshape(TOP_K, n_steps, tm).transpose(1, 0, 2).reshape(n_steps, 1, TOP_K * tm)


def _combine_operands(dest, probs, y_buf, n, tm):
    n_steps = n // tm
    dest3 = _per_tile(dest, tm)
    smem = functools.partial(pl.BlockSpec, (None, 1, tm * TOP_K), memory_space=pltpu.SMEM)
    args = [dest3, dest3, probs, y_buf]
    specs = [smem(lambda i: (i, 0, 0)),
             smem(lambda i: (jnp.minimum(i + 1, n_steps - 1), 0, 0)),
             pl.BlockSpec((2 * TOP_K, tm), lambda i: (0, i)),
             pl.BlockSpec(memory_space=pl.ANY)]
    scratch = [pltpu.VMEM((2, TOP_K, tm * TOKEN_ROWS, V7X_LANES), F32), pltpu.SemaphoreType.DMA((2,))]
    return args, specs, scratch


def _in_proj(x, route, mod_prev, mod, g, w_bf16, seq):
    n = x.shape[0]
    tm = min(ROW_TILE, seq)
    row = pl.BlockSpec((tm, D_MODEL), lambda i: (i, 0))
    has_res = route is not None
    args, specs, scratch = [], [], []
    if has_res:
        args, specs, scratch = _combine_operands(*route, n, tm)
        args = args[:2] + [x] + args[2:] + [mod_prev]
        specs = specs[:2] + [row] + specs[2:] + [_mod_spec(mod_prev, seq, tm)]
    else:
        args, specs = [x], [row]
    args += [mod, g.reshape(1, D_MODEL), w_bf16]
    specs += [_mod_spec(mod, seq, tm),
              pl.BlockSpec((1, D_MODEL), lambda i: (0, 0)),
              pl.BlockSpec((D_MODEL, N_IN), lambda i: (0, 0))]
    proj_shape = jax.ShapeDtypeStruct((n, N_IN), F32)
    proj_spec = pl.BlockSpec((tm, N_IN), lambda i: (i, 0))
    if has_res:
        out_shape = (jax.ShapeDtypeStruct((n, D_MODEL), F32), proj_shape)
        out_specs = (row, proj_spec)
    else:
        out_shape, out_specs = proj_shape, proj_spec
    out = pl.pallas_call(
        functools.partial(_inproj_kernel, has_res=has_res),
        grid=(n // tm,),
        in_specs=specs,
        out_specs=out_specs,
        out_shape=out_shape,
        scratch_shapes=scratch,
        compiler_params=pltpu.CompilerParams(dimension_semantics=("arbitrary",), vmem_limit_bytes=VMEM_LIMIT),
        name="in_proj",
    )(*args)
    return out if has_res else (x, out)


def _group_scan(a, b, reverse):
    shape = a.shape
    grouped = (shape[0] // V7X_SUBLANES, V7X_SUBLANES, shape[1])
    a, b = a.reshape(grouped), b.reshape(grouped)
    ri = lax.broadcasted_iota(jnp.int32, grouped, 1)
    for d in (1, 2, 4):
        shift = V7X_SUBLANES - d if reverse else d
        keep = ri < V7X_SUBLANES - d if reverse else ri >= d
        ra, rb = pltpu.roll(a, shift, 1), pltpu.roll(b, shift, 1)
        b = a * jnp.where(keep, rb, 0.0) + b
        a = a * jnp.where(keep, ra, 1.0)
    return a.reshape(shape), b.reshape(shape)


def _mixer_kernel(s0, s1, s2, caw, cab, nag, nab, cbw, cbb, wg, bg, lam, h0, sng, snb, sw, sbias, cdw,
                  m_head, y_ref, st_ref, *scratch, seq, on_grid):
    def one_sequence(b, carry):
        _mix_sequence(s0.at[b], s1.at[b], s2.at[b], caw, cab, nag, nab, cbw, cbb, wg, bg, lam, h0.at[b],
                      sng, snb, sw, sbias, cdw, m_head, y_ref.at[b], st_ref.at[b], *scratch,
                      seq=seq, on_grid=on_grid)
        return carry

    lax.fori_loop(0, s0.shape[0], one_sequence, 0)


def _mix_sequence(s0, s1, s2, caw, cab, nag, nab, cbw, cbb, wg, bg, lam, h0, sng, snb, sw, sbias, cdw,
                  m_head, y_ref, st_ref, pad_s, af_s, bf_s, ab_s, bb_s, *, seq, on_grid):
    mixer = pl.program_id(1)
    n_conv = seq // CONV_ROWS
    win_rows = CONV_ROWS + 2 * CONV_PAD

    def rows_at(c, size):
        return pl.ds(pl.multiple_of(c * size, size), size)

    def fill_padded(fn):
        zeros = jnp.zeros((CONV_PAD, W_GRP), F32)
        pad_s[0:CONV_PAD, :] = zeros
        pad_s[CONV_PAD + seq:2 * CONV_PAD + seq, :] = zeros

        def body(c, carry):
            dst = pl.ds(pl.multiple_of(c * CONV_ROWS + CONV_PAD, V7X_SUBLANES), CONV_ROWS)
            pad_s[dst, :] = fn(rows_at(c, CONV_ROWS))
            return carry

        lax.fori_loop(0, n_conv, body, 0)

    def conv_window(win, w_ref, taps, pad_l):
        acc = jnp.zeros((CONV_ROWS, W_GRP), F32)
        for mis in range(V7X_SUBLANES):
            starts = [(k, CONV_PAD - pad_l + k) for k in range(taps)
                      if (CONV_PAD - pad_l + k) % V7X_SUBLANES == mis]
            if not starts:
                continue
            shifted = pltpu.roll(win, win_rows - mis, 0) if mis else win
            for k, start in starts:
                acc = acc + w_ref[k:k + 1, :] * shifted[start - mis:start - mis + CONV_ROWS, :]
        return acc

    def conv_rows(c, w_ref, taps, pad_l):
        win = pad_s[pl.ds(pl.multiple_of(c * CONV_ROWS, CONV_ROWS), win_rows), :]
        return conv_window(win, w_ref, taps, pad_l)

    @pl.when(mixer == 0)
    def _conformer():
        def glu(rows):
            return s0[rows, :] * jax.nn.sigmoid(s1[rows, :])

        if not on_grid:
            fill_padded(glu)

        def body(c, carry):
            if on_grid:
                edge = jnp.zeros((CONV_PAD, W_GRP), F32)
                win = jnp.concatenate([edge, glu(rows_at(c, CONV_ROWS)), edge], axis=0)
                z = conv_window(win, caw, CONV_A, CONV_A // 2)
            else:
                z = conv_rows(c, caw, CONV_A, CONV_A // 2)
            af_s[rows_at(c, CONV_ROWS), :] = z + cab[...]
            return carry

        lax.fori_loop(0, n_conv, body, 0)

        def norm(c, carry):
            rows = rows_at(c, NORM_ROWS)
            z = _layernorm(af_s[rows, :], nag[...], nab[...], m_head)
            y_ref[rows, :] = (z * jax.nn.sigmoid(z)).astype(BF16)
            return carry

        lax.fori_loop(0, seq // NORM_ROWS, norm, 0, unroll=2)

    @pl.when(mixer == 1)
    def _rglru():
        fill_padded(lambda rows: s0[rows, :])
        lam_v = lam[...]
        softplus_neg = jnp.maximum(-lam_v, 0.0) + jnp.log1p(jnp.exp(-jnp.abs(lam_v)))

        def gates(c, carry):
            rows = rows_at(c, CONV_ROWS)
            xc = conv_rows(c, cbw, CONV_B, 2) + cbb[...]
            gt = jax.nn.sigmoid(jnp.dot(xc.astype(BF16), wg[...], preferred_element_type=F32) + bg[...])
            for d, (a_s, b_s) in enumerate(((af_s, bf_s), (ab_s, bb_s))):
                r_gate = gt[:, (2 * d) * W_GRP:(2 * d + 1) * W_GRP]
                i_gate = gt[:, (2 * d + 1) * W_GRP:(2 * d + 2) * W_GRP]
                log_a = -LRU_C * r_gate * softplus_neg[d:d + 1, :]
                a = jnp.exp(log_a)
                b = jnp.sqrt(jnp.maximum(-jnp.tanh(log_a) * (a * a + 1.0), 0.0)) * (i_gate * xc)
                a, b = _group_scan(a, b, reverse=(d == 1))
                a_s[rows, :] = a
                b_s[rows, :] = b
            return carry

        lax.fori_loop(0, n_conv, gates, 0, unroll=2)

        n_grp = seq // V7X_SUBLANES

        def chain(g, carry):
            cf, cb = carry
            rf = rows_at(g, V7X_SUBLANES)
            rb = rows_at(n_grp - 1 - g, V7X_SUBLANES)
            hf = af_s[rf, :] * cf + bf_s[rf, :]
            bf_s[rf, :] = hf
            hb = ab_s[rb, :] * cb + bb_s[rb, :]
            bb_s[rb, :] = hb
            cf = jnp.broadcast_to(hf[V7X_SUBLANES - 1:V7X_SUBLANES, :], (V7X_SUBLANES, W_GRP))
            cb = jnp.broadcast_to(hb[0:1, :], (V7X_SUBLANES, W_GRP))
            return cf, cb

        init = (jnp.broadcast_to(h0[0:1, :], (V7X_SUBLANES, W_GRP)),
                jnp.broadcast_to(h0[1:2, :], (V7X_SUBLANES, W_GRP)))
        cf, cb = lax.fori_loop(0, n_grp, chain, init, unroll=4)
        st_ref[0:1, :] = cf[0:1, :]
        st_ref[1:2, :] = cb[0:1, :]

        def out(c, carry):
            rows = rows_at(c, CONV_ROWS)
            y_ref[rows, :] = (jax.nn.gelu(s1[rows, :]) * (bf_s[rows, :] + bb_s[rows, :])).astype(BF16)
            return carry

        lax.fori_loop(0, n_conv, out, 0)

    @pl.when(mixer == 2)
    def _sgu():
        lane = lax.broadcasted_iota(jnp.int32, (CHUNK, W_GRP), 1)

        def body(n, carry):
            rows = rows_at(n, CHUNK)
            v = s1[rows, :]
            vc = v - jnp.mean(v, axis=-1, keepdims=True)
            var = jnp.mean(vc * vc, axis=-1, keepdims=True)
            v = (vc * lax.rsqrt(var + EPS) * sng[...] + snb[...]).astype(BF16)
            s = sbias[...]
            for h in range(HEADS):
                sh = jnp.dot(sw[h], v, preferred_element_type=F32)
                s = s + jnp.where((lane >= h * HD) & (lane < (h + 1) * HD), sh, 0.0)
            y_ref[rows, :] = (s0[rows, :] * s).astype(BF16)
            return carry

        lax.fori_loop(0, seq // CHUNK, body, 0, unroll=4)

    @pl.when(mixer == 3)
    def _gated_conv():
        fill_padded(lambda rows: s1[rows, :] * s2[rows, :])

        def body(c, carry):
            rows = rows_at(c, CONV_ROWS)
            y_ref[rows, :] = (s0[rows, :] * conv_rows(c, cdw, CONV_D, CONV_D // 2)).astype(BF16)
            return carry

        lax.fori_loop(0, n_conv, body, 0)


def _mixers(proj, h0, lw, batch, seq, on_grid):
    proj3 = proj.reshape(batch, seq, N_IN)
    seqs = max(1, MIX_ROWS // seq)
    slab = (seqs, seq, W_GRP)
    state = pl.BlockSpec((seqs, 2, W_GRP), lambda b, m: (b, 0, 0))

    def const(shape):
        return pl.BlockSpec(shape, lambda b, m: (0,) * len(shape))

    in_specs = [
        pl.BlockSpec(slab, lambda b, m: (b, 0, 2 * m)),
        pl.BlockSpec(slab, lambda b, m: (b, 0, 2 * m + 1)),
        pl.BlockSpec(slab, lambda b, m: (b, 0, 8)),
        const((CONV_A, W_GRP)), const((1, W_GRP)), const((1, W_GRP)), const((1, W_GRP)),
        const((CONV_B, W_GRP)), const((1, W_GRP)),
        const((W_GRP, 4 * W_GRP)), const((1, 4 * W_GRP)), const((2, W_GRP)),
        state,
        const((1, W_GRP)), const((1, W_GRP)), const((HEADS, CHUNK, CHUNK)), const((CHUNK, W_GRP)),
        const((CONV_D, W_GRP)), const((W_GRP, W_GRP)),
    ]
    y, st = pl.pallas_call(
        functools.partial(_mixer_kernel, seq=seq, on_grid=on_grid),
        grid=(batch // seqs, N_MIXERS),
        in_specs=in_specs,
        out_specs=(pl.BlockSpec(slab, lambda b, m: (b, 0, m)), state),
        out_shape=(jax.ShapeDtypeStruct((batch, seq, D_MODEL), BF16),
                   jax.ShapeDtypeStruct((batch, 2, W_GRP), F32)),
        scratch_shapes=[pltpu.VMEM((seq + 2 * CONV_PAD, W_GRP), F32)] + [pltpu.VMEM((seq, W_GRP), F32)] * 4,
        compiler_params=pltpu.CompilerParams(dimension_semantics=("arbitrary", "arbitrary"),
                                             vmem_limit_bytes=VMEM_LIMIT),
        name="mixers",
    )(proj3, proj3, proj3, lw["conv_a_w"], lw["conv_a_b"], lw["norm_a_g"], lw["norm_a_b"],
      lw["conv_b_w"], lw["conv_b_b"], lw["gate_w"], lw["gate_b"], lw["lru_lam"], h0,
      lw["sgu_norm_g"], lw["sgu_norm_b"], lw["sgu_w"], lw["sgu_bias"], lw["conv_d_w"],
      lw["m_head"])
    return y.reshape(batch * seq, D_MODEL), st


def _outproj_kernel(y_ref, x_ref, mod_ref, g_ref, wo_ref, wr_ref, br_ref, tri_ref, cin_ref,
                    xm_ref, hn_ref, ei_ref, pr_ref, cnt_ref, seen_s):
    tm = x_ref.shape[0]
    e = N_EXPERTS

    @pl.when(pl.program_id(0) == 0)
    def _():
        seen_s[...] = cin_ref[...]

    y = jnp.dot(y_ref[...], wo_ref[...], preferred_element_type=F32)
    x = x_ref[...] + mod_ref[2:3, :] * y
    xm_ref[...] = x
    hn = _rmsnorm(x, g_ref[...]) * (1.0 + mod_ref[4:5, :]) + mod_ref[3:4, :]
    _store_token_tiles(hn_ref, hn)

    h1 = hn.astype(BF16)
    r1 = hn - h1.astype(F32)
    h2 = r1.astype(BF16)
    h3 = (r1 - h2.astype(F32)).astype(BF16)
    w_t = wr_ref[...]
    contract_features = (((1,), (1,)), ((), ()))
    p1 = lax.dot_general(w_t, h1, contract_features, preferred_element_type=F32)
    p2 = lax.dot_general(w_t, h2, contract_features, preferred_element_type=F32)
    p3 = lax.dot_general(w_t, h3, contract_features, preferred_element_type=F32)
    logits = (p1[2 * e:3 * e] + p2[e:2 * e] + p3[0:e] + p1[e:2 * e] + p2[0:e] + p1[0:e]) + br_ref[...]

    expert = lax.broadcasted_iota(jnp.int32, (e, tm), 0)
    beats = jnp.zeros((e, tm), F32)
    for other in range(e):
        lo = logits[other:other + 1, :]
        ahead = (lo > logits) | ((lo == logits) & (expert > other))
        beats = beats + jnp.where(ahead, 1.0, 0.0)

    chosen = jnp.where(beats < float(TOP_K), 1.0, 0.0)
    earlier = jnp.dot(chosen.astype(BF16), tri_ref[...], preferred_element_type=F32) + seen_s[...]
    seen_s[...] = earlier[:, tm - 1:tm] + chosen[:, tm - 1:tm]
    cnt_ref[...] = seen_s[...]

    def pick(k, values):
        return jnp.sum(jnp.where(beats == float(k), values, 0.0), axis=0, keepdims=True)

    expert_f = expert.astype(F32)
    vals = [pick(k, logits) for k in range(TOP_K)]
    exps = [jnp.exp(v - vals[0]) for v in vals]
    den = exps[0] + exps[1] + exps[2] + exps[3]
    out_row = lax.broadcasted_iota(jnp.int32, (2 * TOP_K, tm), 0)
    ei = jnp.zeros((2 * TOP_K, tm), F32)
    pr = jnp.zeros((2 * TOP_K, tm), F32)
    for k in range(TOP_K):
        ei = jnp.where(out_row == k, pick(k, expert_f), ei)
        ei = jnp.where(out_row == TOP_K + k, pick(k, earlier), ei)
        pr = jnp.where(out_row == k, exps[k] / den, pr)
    ei_ref[...] = ei.astype(jnp.int32)
    pr_ref[...] = pr


def _out_proj(y, x, mod, g, wo_bf16, w_router, b_router, seen, seq):
    n = x.shape[0]
    tm = min(ROW_TILE, seq)
    row = pl.BlockSpec((tm, D_MODEL), lambda i: (i, 0))
    route = pl.BlockSpec((2 * TOP_K, tm), lambda i: (0, i))
    cnt = pl.BlockSpec((N_EXPERTS, 1), lambda i: (0, 0))
    before = jnp.tri(tm, k=-1, dtype=BF16).T
    w1 = w_router.astype(BF16)
    w2 = (w_router - w1.astype(F32)).astype(BF16)
    w3 = (w_router - w1.astype(F32) - w2.astype(F32)).astype(BF16)
    w_t = jnp.concatenate([w1.T, w2.T, w3.T, jnp.zeros_like(w1.T)], axis=0)
    return pl.pallas_call(
        _outproj_kernel,
        grid=(n // tm,),
        in_specs=[row, row, _mod_spec(mod, seq, tm),
                  pl.BlockSpec((1, D_MODEL), lambda i: (0, 0)),
                  pl.BlockSpec((D_MODEL, D_MODEL), lambda i: (0, 0)),
                  pl.BlockSpec((4 * N_EXPERTS, D_MODEL), lambda i: (0, 0)),
                  cnt, pl.BlockSpec((tm, tm), lambda i: (0, 0)), cnt],
        out_specs=(row, pl.BlockSpec((tm * TOKEN_ROWS, V7X_LANES), lambda i: (i, 0)), route, route, cnt),
        out_shape=(jax.ShapeDtypeStruct((n, D_MODEL), F32),
                   jax.ShapeDtypeStruct((n * TOKEN_ROWS, V7X_LANES), F32),
                   jax.ShapeDtypeStruct((2 * TOP_K, n), jnp.int32),
                   jax.ShapeDtypeStruct((2 * TOP_K, n), F32),
                   jax.ShapeDtypeStruct((N_EXPERTS, 1), F32)),
        scratch_shapes=[pltpu.VMEM((N_EXPERTS, 1), F32)],
        compiler_params=pltpu.CompilerParams(dimension_semantics=("arbitrary",)),
        name="out_proj",
    )(y, x, mod, g.reshape(1, D_MODEL), wo_bf16, w_t, b_router.reshape(N_EXPERTS, 1), before, seen)


def _dispatch_kernel(zero_blk, zero_on, dest_ref, hn_p, hn_s, xbuf, zeros_s, sem, zsem, *, tiles_p):
    i = pl.program_id(0)

    @pl.when(i == 0)
    def _():
        zeros_s[...] = jnp.zeros_like(zeros_s)

        def fill(j):
            return pltpu.make_async_copy(zeros_s, xbuf.at[_token_rows(zero_blk[j] * MOE_TM, MOE_TM)], zsem)

        for j in range(zero_blk.shape[0]):
            @pl.when(zero_on[j] == 1)
            def _():
                fill(j).start()

        for j in range(zero_blk.shape[0]):
            @pl.when(zero_on[j] == 1)
            def _():
                fill(j).wait()

    def issue(src):
        def body(j, carry):
            row = src.at[_token_rows(j)]
            for k in range(TOP_K):
                pltpu.make_async_copy(row, xbuf.at[_token_rows(dest_ref[0, k * DISPATCH_TM + j])],
                                      sem).start(priority=k % DMA_PRIORITIES)
            return carry

        lax.fori_loop(0, DISPATCH_TM, body, 0, unroll=4)

    @pl.when(i < tiles_p)
    def _():
        issue(hn_p)

    @pl.when(i >= tiles_p)
    def _():
        issue(hn_s)

    for _ in range(TOP_K):
        pltpu.make_async_copy(hn_s, xbuf.at[_token_rows(0, DISPATCH_TM)], sem).wait()


def _dispatch(dest, zero_blk, zero_on, hn_p, hn_s, n_pad):
    n_p, n_s = hn_p.shape[0] // TOKEN_ROWS, hn_s.shape[0] // TOKEN_ROWS
    tiles_p = n_p // DISPATCH_TM
    n_steps = (n_p + n_s) // DISPATCH_TM
    tile = (DISPATCH_TM * TOKEN_ROWS, V7X_LANES)
    grid_spec = pltpu.PrefetchScalarGridSpec(
        num_scalar_prefetch=2,
        grid=(n_steps,),
        in_specs=[pl.BlockSpec((None, 1, DISPATCH_TM * TOP_K), lambda i, zb, zo: (i, 0, 0),
                               memory_space=pltpu.SMEM),
                  pl.BlockSpec(tile, lambda i, zb, zo: (jnp.minimum(i, tiles_p - 1), 0)),
                  pl.BlockSpec(tile, lambda i, zb, zo: (jnp.maximum(i - tiles_p, 0), 0))],
        out_specs=pl.BlockSpec(memory_space=pl.ANY),
        scratch_shapes=[pltpu.VMEM((MOE_TM * TOKEN_ROWS, V7X_LANES), F32), pltpu.SemaphoreType.DMA(()),
                        pltpu.SemaphoreType.DMA(())],
    )
    return pl.pallas_call(
        functools.partial(_dispatch_kernel, tiles_p=tiles_p),
        grid_spec=grid_spec,
        out_shape=jax.ShapeDtypeStruct((n_pad * TOKEN_ROWS, V7X_LANES), F32),
        compiler_params=pltpu.CompilerParams(dimension_semantics=("arbitrary",)),
        name="moe_dispatch",
    )(zero_blk, zero_on, _per_tile(dest, DISPATCH_TM), hn_p, hn_s)


def _moe_kernel(blk_e, blk_new, blk_on, blk_slot, blk_next, x_ref, wgu_hbm, bgu_ref, wdn_hbm, bdn_ref, o_ref,
                wgu_f, wdn_f, wgu_s, wdn_s, sems, *, layer):
    i = pl.program_id(0)

    def fetch(expert, slot):
        return (pltpu.make_async_copy(wgu_hbm.at[layer, expert], wgu_f.at[slot], sems.at[0, slot]),
                pltpu.make_async_copy(wdn_hbm.at[layer, expert], wdn_f.at[slot], sems.at[1, slot]))

    @pl.when(i == 0)
    def _():
        for copy in fetch(blk_e[0], 0):
            copy.start()

    @pl.when(blk_on[i] == BLK_UNUSED)
    def _():
        o_ref[...] = jnp.zeros_like(o_ref)

    @pl.when(blk_on[i] != BLK_UNUSED)
    def _():
        @pl.when(blk_new[i] == 1)
        def _():
            slot = blk_slot[i]
            for copy in fetch(blk_e[i], slot):
                copy.wait()

            @pl.when(blk_next[i] >= 0)
            def _():
                for copy in fetch(blk_next[i], 1 - slot):
                    copy.start()

            wgu_s[...] = wgu_f[slot].astype(BF16)
            wdn_s[...] = wdn_f[slot].astype(BF16)

        def expert_rows(n):
            x = jnp.concatenate([c.astype(BF16) for c in _load_token_tiles(x_ref, n)], axis=1)
            gu = jnp.dot(x, wgu_s[...], preferred_element_type=F32) + bgu_ref[...]
            g = jnp.minimum(gu[:, :D_FF], SWIGLU_LIMIT)
            u = jnp.clip(gu[:, D_FF:], -SWIGLU_LIMIT, SWIGLU_LIMIT)
            act = (u + 1.0) * (g * jax.nn.sigmoid(SWIGLU_ALPHA * g))
            _store_token_tiles(o_ref, jnp.dot(act.astype(BF16), wdn_s[...], preferred_element_type=F32)
                               + bdn_ref[...])

        @pl.when(blk_on[i] == BLK_FULL)
        def _():
            expert_rows(MOE_TM)

        @pl.when(blk_on[i] == BLK_HALF)
        def _():
            expert_rows(MOE_TM // 2)
            o_ref[_token_rows(MOE_TM // 2, MOE_TM // 2), :] = jnp.zeros(
                (MOE_TM // 2 * TOKEN_ROWS, V7X_LANES), F32)


def _moe_blocks(x_buf, blk, layer, w_gu, b_gu, w_dn, b_dn):
    n_pad = x_buf.shape[0] // TOKEN_ROWS
    n_blk = n_pad // MOE_TM
    tile = pl.BlockSpec((MOE_TM * TOKEN_ROWS, V7X_LANES), lambda i, e, *_: (i, 0))
    grid_spec = pltpu.PrefetchScalarGridSpec(
        num_scalar_prefetch=len(blk),
        grid=(n_blk,),
        in_specs=[
            tile,
            pl.BlockSpec(memory_space=pl.ANY),
            pl.BlockSpec((None, 1, 2 * D_FF), lambda i, e, *_: (e[i], 0, 0)),
            pl.BlockSpec(memory_space=pl.ANY),
            pl.BlockSpec((None, 1, D_MODEL), lambda i, e, *_: (e[i], 0, 0)),
        ],
        out_specs=tile,
        scratch_shapes=[pltpu.VMEM((2, D_MODEL, 2 * D_FF), F32), pltpu.VMEM((2, D_FF, D_MODEL), F32),
                        pltpu.VMEM((D_MODEL, 2 * D_FF), BF16), pltpu.VMEM((D_FF, D_MODEL), BF16),
                        pltpu.SemaphoreType.DMA((2, 2))],
    )
    return pl.pallas_call(
        functools.partial(_moe_kernel, layer=layer),
        grid_spec=grid_spec,
        out_shape=jax.ShapeDtypeStruct((n_pad * TOKEN_ROWS, V7X_LANES), F32),
        compiler_params=pltpu.CompilerParams(dimension_semantics=("arbitrary",),
                                             vmem_limit_bytes=VMEM_LIMIT),
        name="moe_experts",
    )(*blk, x_buf, w_gu, b_gu.reshape(N_EXPERTS, 1, 2 * D_FF), w_dn, b_dn.reshape(N_EXPERTS, 1, D_MODEL))


def _slot_kernel(start_ref, ei_ref, dest_ref):
    expert, rank = ei_ref[0:TOP_K, :], ei_ref[TOP_K:2 * TOP_K, :]
    start = jnp.zeros_like(expert)
    for e in range(N_EXPERTS):
        start = jnp.where(expert == e, start_ref[e], start)
    dest_ref[...] = start + rank


def _slots(ei, seg_start):
    n = ei.shape[1]
    grid_spec = pltpu.PrefetchScalarGridSpec(
        num_scalar_prefetch=1,
        grid=(n // SLOT_TN,),
        in_specs=[pl.BlockSpec((2 * TOP_K, SLOT_TN), lambda i, s: (0, i))],
        out_specs=pl.BlockSpec((TOP_K, SLOT_TN), lambda i, s: (0, i)),
    )
    return pl.pallas_call(
        _slot_kernel,
        grid_spec=grid_spec,
        out_shape=jax.ShapeDtypeStruct((TOP_K, n), jnp.int32),
        name="moe_slots",
    )(seg_start, ei)


def _moe(hn_p, hn_s, ei_p, ei_s, counts, layer, w_gu, b_gu, w_dn, b_dn):
    n_asg = (ei_p.shape[1] + ei_s.shape[1]) * TOP_K
    n_pad = (n_asg + N_EXPERTS * (MOE_TM - 1) + MOE_TM - 1) // MOE_TM * MOE_TM
    n_blk = n_pad // MOE_TM
    counts = counts.reshape(N_EXPERTS).astype(jnp.int32)
    padded = (counts + MOE_TM - 1) // MOE_TM * MOE_TM
    pad_end = jnp.cumsum(padded)
    pad_start = pad_end - padded
    blk_row = jnp.arange(n_blk, dtype=jnp.int32) * MOE_TM
    blk_e = jnp.minimum(jnp.sum((blk_row[:, None] >= pad_end[None, :]).astype(jnp.int32), axis=1),
                        N_EXPERTS - 1)
    blk_tokens = jnp.clip(counts[blk_e] - (blk_row - pad_start[blk_e]), 0, MOE_TM)
    blk_on = jnp.where(blk_row >= pad_end[-1], BLK_UNUSED,
                       jnp.where(blk_tokens <= MOE_TM // 2, BLK_HALF, BLK_FULL)).astype(jnp.int32)
    blk_new = jnp.concatenate([jnp.ones((1,), jnp.int32), (blk_e[1:] != blk_e[:-1]).astype(jnp.int32)])
    blk_slot = (jnp.cumsum(blk_new) - 1) % 2
    experts = jnp.arange(N_EXPERTS, dtype=jnp.int32)
    in_use = jnp.where(counts > 0, experts, N_EXPERTS)
    after = jnp.concatenate([lax.cummin(in_use[::-1])[::-1][1:], jnp.full((1,), N_EXPERTS, jnp.int32)])
    blk_next = jnp.where(after[blk_e] < N_EXPERTS, after[blk_e], -1)

    tail = n_blk - n_asg // MOE_TM
    last_on = (counts % MOE_TM != 0).astype(jnp.int32)
    tail_blk = pad_end[-1] // MOE_TM + jnp.arange(tail, dtype=jnp.int32)
    tail_on = (tail_blk < n_blk).astype(jnp.int32)
    zero_blk = jnp.concatenate([(pad_end // MOE_TM - 1) * last_on, tail_blk * tail_on])
    zero_on = jnp.concatenate([last_on, tail_on])

    dest = _slots(jnp.concatenate([ei_p, ei_s], axis=1), pad_start)
    dest_p, dest_s = dest[:, :ei_p.shape[1]], dest[:, ei_p.shape[1]:]
    x_buf = _dispatch(dest, zero_blk, zero_on, hn_p, hn_s, n_pad)
    y_buf = _moe_blocks(x_buf, (blk_e, blk_new, blk_on, blk_slot.astype(jnp.int32), blk_next.astype(jnp.int32)),
                        layer, w_gu, b_gu, w_dn, b_dn)
    return dest_p, dest_s, y_buf


def _final_kernel(dest_cur, dest_nxt, x_ref, pr_ref, ybuf, mod_ref, g_ref, o_ref, rows, sems):
    x = x_ref[...] + mod_ref[5:6, :] * _combine_experts(dest_cur, dest_nxt, pr_ref, ybuf, rows, sems)
    o_ref[...] = _rmsnorm(x, g_ref[...])


def _final_norm(x, route, mod, g, seq):
    n = x.shape[0]
    tm = min(ROW_TILE, seq)
    row = pl.BlockSpec((tm, D_MODEL), lambda i: (i, 0))
    args, specs, scratch = _combine_operands(*route, n, tm)
    return pl.pallas_call(
        _final_kernel,
        grid=(n // tm,),
        in_specs=specs[:2] + [row] + specs[2:] + [_mod_spec(mod, seq, tm),
                                                  pl.BlockSpec((1, D_MODEL), lambda i: (0, 0))],
        out_specs=row,
        out_shape=jax.ShapeDtypeStruct((n, D_MODEL), F32),
        scratch_shapes=scratch,
        compiler_params=pltpu.CompilerParams(dimension_semantics=("arbitrary",), vmem_limit_bytes=VMEM_LIMIT),
        name="final_norm",
    )(*args[:2], x, *args[2:], mod, g.reshape(1, D_MODEL))


def _block_diag(w):
    eye = jnp.eye(HEADS, dtype=w.dtype)
    return (eye[:, None, :, None] * w[:, :, None, :]).reshape(W_GRP, W_GRP)


def _layer_weights(l, p):
    gate_w = jnp.concatenate([_block_diag(p["lru_wr"][l, 0]), _block_diag(p["lru_wi"][l, 0]),
                              _block_diag(p["lru_wr"][l, 1]), _block_diag(p["lru_wi"][l, 1])], axis=1)
    gate_b = jnp.concatenate([p["lru_br"][l, 0], p["lru_bi"][l, 0], p["lru_br"][l, 1], p["lru_bi"][l, 1]])
    head_of = jnp.arange(W_GRP) // HD
    row = lambda v: v.reshape(1, W_GRP)
    return dict(
        conv_a_w=p["conv_a_w"][l], conv_a_b=row(p["conv_a_b"][l]),
        norm_a_g=row(p["norm_a_g"][l]), norm_a_b=row(p["norm_a_b"][l]),
        conv_b_w=p["conv_b_w"][l], conv_b_b=row(p["conv_b_b"][l]),
        gate_w=gate_w.astype(BF16), gate_b=gate_b.reshape(1, 4 * W_GRP), lru_lam=p["lru_lam"][l],
        sgu_norm_g=row(p["sgu_norm_g"][l]), sgu_norm_b=row(p["sgu_norm_b"][l]),
        sgu_w=p["sgu_w"][l].astype(BF16), sgu_bias=jnp.repeat(p["sgu_b"][l].T, HD, axis=1),
        conv_d_w=p["conv_d_w"][l],
        m_head=((head_of[:, None] == head_of[None, :]).astype(F32) / HD).astype(BF16),
    )


def kernel(x_prompt, x_sample, state_rglru, c, c_ctx, w_ada, b_ada, norm1_g, norm2_g, w_in, conv_a_w,
           conv_a_b, norm_a_g, norm_a_b, conv_b_w, conv_b_b, lru_wr, lru_br, lru_wi, lru_bi, lru_lam,
           sgu_norm_g, sgu_norm_b, sgu_w, sgu_b, conv_d_w, w_out, w_router, b_router, w_gu, b_gu, w_dn,
           b_dn, final_g):
    p = dict(conv_a_w=conv_a_w, conv_a_b=conv_a_b, norm_a_g=norm_a_g, norm_a_b=norm_a_b,
             conv_b_w=conv_b_w, conv_b_b=conv_b_b, lru_wr=lru_wr, lru_br=lru_br, lru_wi=lru_wi,
             lru_bi=lru_bi, lru_lam=lru_lam, sgu_norm_g=sgu_norm_g, sgu_norm_b=sgu_norm_b,
             sgu_w=sgu_w, sgu_b=sgu_b, conv_d_w=conv_d_w)
    bp, tp, _ = x_prompt.shape
    bs, ts, _ = x_sample.shape
    n_p, n_s = bp * tp, bs * ts

    cond_rows = jnp.zeros((COND_ROWS, D_MODEL), F32).at[0].set(c_ctx).at[1:1 + bs].set(c)
    mod = _ada_mod(cond_rows, w_ada, b_ada).reshape(DEPTH, COND_ROWS, 6, D_MODEL)

    xp = x_prompt.reshape(n_p, D_MODEL)
    xs = x_sample.reshape(n_s, D_MODEL)
    h0_ctx = jnp.zeros((bp, 2, W_GRP), F32)
    route_p = route_s = mod_p_prev = mod_s_prev = None
    no_tokens_seen = jnp.zeros((N_EXPERTS, 1), F32)
    states = []
    for l in range(DEPTH):
        lw = _layer_weights(l, p)
        mod_p, mod_s = mod[l, 0:1], mod[l, 1:1 + bs]
        w_in_l = w_in[l].astype(BF16)
        w_out_l = w_out[l].astype(BF16)
        xp, proj_p = _in_proj(xp, route_p, mod_p_prev, mod_p, norm1_g[l], w_in_l, tp)
        xs, proj_s = _in_proj(xs, route_s, mod_s_prev, mod_s, norm1_g[l], w_in_l, ts)
        y_p, st = _mixers(proj_p, h0_ctx, lw, bp, tp, False)
        y_s, _ = _mixers(proj_s, state_rglru[:, l], lw, bs, ts, True)
        states.append(st)
        xp, hn_p, ei_p, pr_p, seen = _out_proj(y_p, xp, mod_p, norm2_g[l], w_out_l, w_router[l],
                                               b_router[l], no_tokens_seen, tp)
        xs, hn_s, ei_s, pr_s, counts = _out_proj(y_s, xs, mod_s, norm2_g[l], w_out_l, w_router[l],
                                                 b_router[l], seen, ts)
        dest_p, dest_s, y_buf = _moe(hn_p, hn_s, ei_p, ei_s, counts, l, w_gu, b_gu[l], w_dn, b_dn[l])
        route_p, route_s = (dest_p, pr_p, y_buf), (dest_s, pr_s, y_buf)
        mod_p_prev, mod_s_prev = mod_p, mod_s
    y_prompt = _final_norm(xp, route_p, mod_p_prev, final_g, tp).reshape(bp, tp, D_MODEL)
    y_sample = _final_norm(xs, route_s, mod_s_prev, final_g, ts).reshape(bs, ts, D_MODEL)
    return y_prompt, y_sample, jnp.stack(states, axis=1)
```

```python
import functools

import jax
import jax.numpy as jnp
from jax import lax
from jax.experimental import pallas as pl
from jax.experimental.pallas import tpu as pltpu

F32 = jnp.float32
BF16 = jnp.bfloat16

D_MODEL = 1024
DEPTH = 2
GRID_W = 64
N_MIXERS = 4
W_GRP = D_MODEL // N_MIXERS
HEADS = 4
HD = W_GRP // HEADS
N_IN = 9 * W_GRP
CONV_A = 31
CONV_B = 4
CONV_D = 3
CHUNK = 128
LRU_C = 8.0
N_EXPERTS = 32
TOP_K = 4
D_FF = D_MODEL
SWIGLU_LIMIT = 7.0
SWIGLU_ALPHA = 1.702
EPS = 1e-6

V7X_SUBLANES = 8
V7X_LANES = 128
TOKEN_ROWS = D_MODEL // V7X_LANES
DMA_PRIORITIES = 2
V7X_VMEM_BYTES = 64 * 1024 * 1024
VMEM_LIMIT = V7X_VMEM_BYTES * 7 // 8

COND_ROWS = 16
ADA_TN = 1536
ROW_TILE = 512
CONV_ROWS = GRID_W
CONV_PAD = 16
NORM_ROWS = 256
MIX_ROWS = 1024
MOE_TM = 512
BLK_UNUSED, BLK_HALF, BLK_FULL = 0, 1, 2
DISPATCH_TM = 512
SLOT_TN = 2048
COMBINE_ROWS = 32


def _rmsnorm(x, g):
    return x * lax.rsqrt(jnp.mean(x * x, axis=-1, keepdims=True) + EPS) * g


def _group_mean(x, m_ref):
    hi = x.astype(BF16)
    lo = (x - hi.astype(F32)).astype(BF16)
    m = m_ref[...]
    return (jnp.dot(hi, m, preferred_element_type=F32) + jnp.dot(lo, m, preferred_element_type=F32))


def _layernorm(x, g, b, m_ref):
    xc = x - _group_mean(x, m_ref)
    var = _group_mean(xc * xc, m_ref)
    return xc * lax.rsqrt(var + EPS) * g + b


def _ada_kernel(c_ref, w_ref, b_ref, o_ref):
    c = c_ref[...]
    cond = (c * jax.nn.sigmoid(c)).astype(BF16)
    o_ref[...] = jnp.dot(cond, w_ref[...].astype(BF16), preferred_element_type=F32) + b_ref[...]


def _ada_mod(cond_rows, w_ada, b_ada):
    n_col = w_ada.shape[-1]
    return pl.pallas_call(
        _ada_kernel,
        grid=(DEPTH, n_col // ADA_TN),
        in_specs=[
            pl.BlockSpec((COND_ROWS, D_MODEL), lambda l, j: (0, 0)),
            pl.BlockSpec((None, D_MODEL, ADA_TN), lambda l, j: (l, 0, j)),
            pl.BlockSpec((None, 1, ADA_TN), lambda l, j: (l, 0, j)),
        ],
        out_specs=pl.BlockSpec((None, COND_ROWS, ADA_TN), lambda l, j: (l, 0, j)),
        out_shape=jax.ShapeDtypeStruct((DEPTH, COND_ROWS, n_col), F32),
        name="ada_mod",
    )(cond_rows, w_ada, b_ada.reshape(DEPTH, 1, n_col))


def _token_rows(t, count=1):
    return pl.ds(pl.multiple_of(t * TOKEN_ROWS, TOKEN_ROWS), count * TOKEN_ROWS)


def _store_token_tiles(ref, x):
    n = x.shape[0]
    for c in range(TOKEN_ROWS):
        ref[pl.ds(c, n, stride=TOKEN_ROWS), :] = x[:, c * V7X_LANES:(c + 1) * V7X_LANES]


def _load_token_tiles(ref, n):
    return [ref[pl.ds(c, n, stride=TOKEN_ROWS), :] for c in range(TOKEN_ROWS)]


def _combine_experts(dest_cur, dest_nxt, pr_ref, ybuf, rows, moe_s, sems):
    i = pl.program_id(0)
    n_steps = pl.num_programs(0)
    tm = rows.shape[2] // TOKEN_ROWS
    slot = i % 2

    def row_copy(dref, j, k, slot_):
        return pltpu.make_async_copy(ybuf.at[_token_rows(dref[0, k * tm + j])],
                                     rows.at[slot_, k, _token_rows(j)], sems.at[slot_])

    def issue(dref, slot_):
        def body(j, carry):
            for k in range(TOP_K):
                row_copy(dref, j, k, slot_).start(priority=k % DMA_PRIORITIES)
            return carry

        lax.fori_loop(0, tm, body, 0, unroll=4)

    @pl.when(i == 0)
    def _():
        issue(dest_cur, 0)

    for nxt in (0, 1):
        @pl.when((i + 1 < n_steps) & (slot != nxt))
        def _():
            issue(dest_nxt, nxt)

    for k in range(TOP_K):
        pltpu.make_async_copy(ybuf.at[_token_rows(0, tm)], rows.at[slot, k], sems.at[slot]).wait()
    pad = jnp.zeros((V7X_LANES - pr_ref.shape[0], tm), F32)
    probs = jnp.concatenate([pr_ref[...], pad], axis=0).T
    for t0 in range(0, tm, COMBINE_ROWS):
        weight = [probs[t0:t0 + COMBINE_ROWS, k:k + 1] for k in range(TOP_K)]
        for c in range(TOKEN_ROWS):
            chunk = pl.ds(t0 * TOKEN_ROWS + c, COMBINE_ROWS, stride=TOKEN_ROWS)
            acc = weight[0] * rows.at[slot, 0][chunk, :]
            for k in range(1, TOP_K):
                acc = acc + weight[k] * rows.at[slot, k][chunk, :]
            moe_s[t0:t0 + COMBINE_ROWS, c * V7X_LANES:(c + 1) * V7X_LANES] = acc
    return moe_s[...]


def _inproj_kernel(*refs, has_res):
    if has_res:
        (dest_cur, dest_nxt, x_ref, pr_ref, ybuf, modp_ref, mod_ref, g_ref, w_ref,
         xo_ref, p_ref, rows, moe_s, sems) = refs
        x = x_ref[...] + modp_ref[5:6, :] * _combine_experts(dest_cur, dest_nxt, pr_ref, ybuf, rows, moe_s, sems)
        xo_ref[...] = x
    else:
        x_ref, mod_ref, g_ref, w_ref, p_ref = refs
        x = x_ref[...]
    hn = _rmsnorm(x, g_ref[...]) * (1.0 + mod_ref[1:2, :]) + mod_ref[0:1, :]
    p_ref[...] = jnp.dot(hn.astype(BF16), w_ref[...], preferred_element_type=F32)


def _mod_spec(mod, seq, tm):
    if mod.shape[0] == 1:
        return pl.BlockSpec((None, 6, D_MODEL), lambda i: (0, 0, 0))
    return pl.BlockSpec((None, 6, D_MODEL), lambda i: ((i * tm) // seq, 0, 0))


def _per_tile(dest, tm):
    n_steps = dest.shape[1] // tm
    return dest.reshape(TOP_K, n_steps, tm).transpose(1, 0, 2).reshape(n_steps, 1, TOP_K * tm)


def _combine_operands(dest, probs, y_buf, n, tm):
    n_steps = n // tm
    dest3 = _per_tile(dest, tm)
    smem = functools.partial(pl.BlockSpec, (None, 1, tm * TOP_K), memory_space=pltpu.SMEM)
    args = [dest3, dest3, probs, y_buf]
    specs = [smem(lambda i: (i, 0, 0)),
             smem(lambda i: (jnp.minimum(i + 1, n_steps - 1), 0, 0)),
             pl.BlockSpec((2 * TOP_K, tm), lambda i: (0, i)),
             pl.BlockSpec(memory_space=pl.ANY)]
    scratch = [pltpu.VMEM((2, TOP_K, tm * TOKEN_ROWS, V7X_LANES), F32), pltpu.VMEM((tm, D_MODEL), F32),
               pltpu.SemaphoreType.DMA((2,))]
    return args, specs, scratch


def _in_proj(x, route, mod_prev, mod, g, w_bf16, seq):
    n = x.shape[0]
    tm = min(ROW_TILE, seq)
    row = pl.BlockSpec((tm, D_MODEL), lambda i: (i, 0))
    has_res = route is not None
    args, specs, scratch = [], [], []
    if has_res:
        args, specs, scratch = _combine_operands(*route, n, tm)
        args = args[:2] + [x] + args[2:] + [mod_prev]
        specs = specs[:2] + [row] + specs[2:] + [_mod_spec(mod_prev, seq, tm)]
    else:
        args, specs = [x], [row]
    args += [mod, g.reshape(1, D_MODEL), w_bf16]
    specs += [_mod_spec(mod, seq, tm),
              pl.BlockSpec((1, D_MODEL), lambda i: (0, 0)),
              pl.BlockSpec((D_MODEL, N_IN), lambda i: (0, 0))]
    proj_shape = jax.ShapeDtypeStruct((n, N_IN), F32)
    proj_spec = pl.BlockSpec((tm, N_IN), lambda i: (i, 0))
    if has_res:
        out_shape = (jax.ShapeDtypeStruct((n, D_MODEL), F32), proj_shape)
        out_specs = (row, proj_spec)
    else:
        out_shape, out_specs = proj_shape, proj_spec
    out = pl.pallas_call(
        functools.partial(_inproj_kernel, has_res=has_res),
        grid=(n // tm,),
        in_specs=specs,
        out_specs=out_specs,
        out_shape=out_shape,
        scratch_shapes=scratch,
        compiler_params=pltpu.CompilerParams(dimension_semantics=("arbitrary",), vmem_limit_bytes=VMEM_LIMIT),
        name="in_proj",
    )(*args)
    return out if has_res else (x, out)


def _group_scan(a, b, reverse):
    shape = a.shape
    grouped = (shape[0] // V7X_SUBLANES, V7X_SUBLANES, shape[1])
    a, b = a.reshape(grouped), b.reshape(grouped)
    ri = lax.broadcasted_iota(jnp.int32, grouped, 1)
    for d in (1, 2, 4):
        shift = V7X_SUBLANES - d if reverse else d
        keep = ri < V7X_SUBLANES - d if reverse else ri >= d
        ra, rb = pltpu.roll(a, shift, 1), pltpu.roll(b, shift, 1)
        b = a * jnp.where(keep, rb, 0.0) + b
        a = a * jnp.where(keep, ra, 1.0)
    return a.reshape(shape), b.reshape(shape)


def _mixer_kernel(s0, s1, s2, caw, cab, nag, nab, cbw, cbb, wg, bg, lam, h0, sng, snb, sw, sbias, cdw,
                  m_head, y_ref, st_ref, *scratch, seq, on_grid):
    def one_sequence(b, carry):
        _mix_sequence(s0.at[b], s1.at[b], s2.at[b], caw, cab, nag, nab, cbw, cbb, wg, bg, lam, h0.at[b],
                      sng, snb, sw, sbias, cdw, m_head, y_ref.at[b], st_ref.at[b], *scratch,
                      seq=seq, on_grid=on_grid)
        return carry

    lax.fori_loop(0, s0.shape[0], one_sequence, 0)


def _mix_sequence(s0, s1, s2, caw, cab, nag, nab, cbw, cbb, wg, bg, lam, h0, sng, snb, sw, sbias, cdw,
                  m_head, y_ref, st_ref, pad_s, af_s, bf_s, ab_s, bb_s, *, seq, on_grid):
    mixer = pl.program_id(1)
    n_conv = seq // CONV_ROWS
    win_rows = CONV_ROWS + 2 * CONV_PAD

    def rows_at(c, size):
        return pl.ds(pl.multiple_of(c * size, size), size)

    def fill_padded(fn):
        zeros = jnp.zeros((CONV_PAD, W_GRP), F32)
        pad_s[0:CONV_PAD, :] = zeros
        pad_s[CONV_PAD + seq:2 * CONV_PAD + seq, :] = zeros

        def body(c, carry):
            dst = pl.ds(pl.multiple_of(c * CONV_ROWS + CONV_PAD, V7X_SUBLANES), CONV_ROWS)
            pad_s[dst, :] = fn(rows_at(c, CONV_ROWS))
            return carry

        lax.fori_loop(0, n_conv, body, 0)

    def conv_window(win, w_ref, taps, pad_l):
        acc = jnp.zeros((CONV_ROWS, W_GRP), F32)
        for mis in range(V7X_SUBLANES):
            starts = [(k, CONV_PAD - pad_l + k) for k in range(taps)
                      if (CONV_PAD - pad_l + k) % V7X_SUBLANES == mis]
            if not starts:
                continue
            shifted = pltpu.roll(win, win_rows - mis, 0) if mis else win
            for k, start in starts:
                acc = acc + w_ref[k:k + 1, :] * shifted[start - mis:start - mis + CONV_ROWS, :]
        return acc

    def conv_rows(c, w_ref, taps, pad_l):
        win = pad_s[pl.ds(pl.multiple_of(c * CONV_ROWS, CONV_ROWS), win_rows), :]
        return conv_window(win, w_ref, taps, pad_l)

    @pl.when(mixer == 0)
    def _conformer():
        def glu(rows):
            return s0[rows, :] * jax.nn.sigmoid(s1[rows, :])

        if not on_grid:
            fill_padded(glu)

        def body(c, carry):
            if on_grid:
                edge = jnp.zeros((CONV_PAD, W_GRP), F32)
                win = jnp.concatenate([edge, glu(rows_at(c, CONV_ROWS)), edge], axis=0)
                z = conv_window(win, caw, CONV_A, CONV_A // 2)
            else:
                z = conv_rows(c, caw, CONV_A, CONV_A // 2)
            af_s[rows_at(c, CONV_ROWS), :] = z + cab[...]
            return carry

        lax.fori_loop(0, n_conv, body, 0)

        def norm(c, carry):
            rows = rows_at(c, NORM_ROWS)
            z = _layernorm(af_s[rows, :], nag[...], nab[...], m_head)
            y_ref[rows, :] = (z * jax.nn.sigmoid(z)).astype(BF16)
            return carry

        lax.fori_loop(0, seq // NORM_ROWS, norm, 0, unroll=2)

    @pl.when(mixer == 1)
    def _rglru():
        fill_padded(lambda rows: s0[rows, :])
        lam_v = lam[...]
        softplus_neg = jnp.maximum(-lam_v, 0.0) + jnp.log1p(jnp.exp(-jnp.abs(lam_v)))

        def gates(c, carry):
            rows = rows_at(c, CONV_ROWS)
            xc = conv_rows(c, cbw, CONV_B, 2) + cbb[...]
            gt = jax.nn.sigmoid(jnp.dot(xc.astype(BF16), wg[...], preferred_element_type=F32) + bg[...])
            for d, (a_s, b_s) in enumerate(((af_s, bf_s), (ab_s, bb_s))):
                r_gate = gt[:, (2 * d) * W_GRP:(2 * d + 1) * W_GRP]
                i_gate = gt[:, (2 * d + 1) * W_GRP:(2 * d + 2) * W_GRP]
                log_a = -LRU_C * r_gate * softplus_neg[d:d + 1, :]
                a = jnp.exp(log_a)
                b = jnp.sqrt(jnp.maximum(-jnp.tanh(log_a) * (a * a + 1.0), 0.0)) * (i_gate * xc)
                a, b = _group_scan(a, b, reverse=(d == 1))
                a_s[rows, :] = a
                b_s[rows, :] = b
            return carry

        lax.fori_loop(0, n_conv, gates, 0, unroll=2)

        n_grp = seq // V7X_SUBLANES

        def chain(g, carry):
            cf, cb = carry
            rf = rows_at(g, V7X_SUBLANES)
            rb = rows_at(n_grp - 1 - g, V7X_SUBLANES)
            hf = af_s[rf, :] * cf + bf_s[rf, :]
            bf_s[rf, :] = hf
            hb = ab_s[rb, :] * cb + bb_s[rb, :]
            bb_s[rb, :] = hb
            cf = jnp.broadcast_to(hf[V7X_SUBLANES - 1:V7X_SUBLANES, :], (V7X_SUBLANES, W_GRP))
            cb = jnp.broadcast_to(hb[0:1, :], (V7X_SUBLANES, W_GRP))
            return cf, cb

        init = (jnp.broadcast_to(h0[0:1, :], (V7X_SUBLANES, W_GRP)),
                jnp.broadcast_to(h0[1:2, :], (V7X_SUBLANES, W_GRP)))
        cf, cb = lax.fori_loop(0, n_grp, chain, init, unroll=4)
        st_ref[0:1, :] = cf[0:1, :]
        st_ref[1:2, :] = cb[0:1, :]

        def out(c, carry):
            rows = rows_at(c, CONV_ROWS)
            y_ref[rows, :] = (jax.nn.gelu(s1[rows, :]) * (bf_s[rows, :] + bb_s[rows, :])).astype(BF16)
            return carry

        lax.fori_loop(0, n_conv, out, 0)

    @pl.when(mixer == 2)
    def _sgu():
        lane = lax.broadcasted_iota(jnp.int32, (CHUNK, W_GRP), 1)

        def body(n, carry):
            rows = rows_at(n, CHUNK)
            v = s1[rows, :]
            vc = v - jnp.mean(v, axis=-1, keepdims=True)
            var = jnp.mean(vc * vc, axis=-1, keepdims=True)
            v = (vc * lax.rsqrt(var + EPS) * sng[...] + snb[...]).astype(BF16)
            s = sbias[...]
            for h in range(HEADS):
                sh = jnp.dot(sw[h], v, preferred_element_type=F32)
                s = s + jnp.where((lane >= h * HD) & (lane < (h + 1) * HD), sh, 0.0)
            y_ref[rows, :] = (s0[rows, :] * s).astype(BF16)
            return carry

        lax.fori_loop(0, seq // CHUNK, body, 0, unroll=4)

    @pl.when(mixer == 3)
    def _gated_conv():
        fill_padded(lambda rows: s1[rows, :] * s2[rows, :])

        def body(c, carry):
            rows = rows_at(c, CONV_ROWS)
            y_ref[rows, :] = (s0[rows, :] * conv_rows(c, cdw, CONV_D, CONV_D // 2)).astype(BF16)
            return carry

        lax.fori_loop(0, n_conv, body, 0)


def _mixers(proj, h0, lw, batch, seq, on_grid):
    proj3 = proj.reshape(batch, seq, N_IN)
    seqs = max(1, MIX_ROWS // seq)
    slab = (seqs, seq, W_GRP)
    state = pl.BlockSpec((seqs, 2, W_GRP), lambda b, m: (b, 0, 0))

    def const(shape):
        return pl.BlockSpec(shape, lambda b, m: (0,) * len(shape))

    in_specs = [
        pl.BlockSpec(slab, lambda b, m: (b, 0, 2 * m)),
        pl.BlockSpec(slab, lambda b, m: (b, 0, 2 * m + 1)),
        pl.BlockSpec(slab, lambda b, m: (b, 0, 8)),
        const((CONV_A, W_GRP)), const((1, W_GRP)), const((1, W_GRP)), const((1, W_GRP)),
        const((CONV_B, W_GRP)), const((1, W_GRP)),
        const((W_GRP, 4 * W_GRP)), const((1, 4 * W_GRP)), const((2, W_GRP)),
        state,
        const((1, W_GRP)), const((1, W_GRP)), const((HEADS, CHUNK, CHUNK)), const((CHUNK, W_GRP)),
        const((CONV_D, W_GRP)), const((W_GRP, W_GRP)),
    ]
    y, st = pl.pallas_call(
        functools.partial(_mixer_kernel, seq=seq, on_grid=on_grid),
        grid=(batch // seqs, N_MIXERS),
        in_specs=in_specs,
        out_specs=(pl.BlockSpec(slab, lambda b, m: (b, 0, m)), state),
        out_shape=(jax.ShapeDtypeStruct((batch, seq, D_MODEL), BF16),
                   jax.ShapeDtypeStruct((batch, 2, W_GRP), F32)),
        scratch_shapes=[pltpu.VMEM((seq + 2 * CONV_PAD, W_GRP), F32)] + [pltpu.VMEM((seq, W_GRP), F32)] * 4,
        compiler_params=pltpu.CompilerParams(dimension_semantics=("arbitrary", "arbitrary"),
                                             vmem_limit_bytes=VMEM_LIMIT),
        name="mixers",
    )(proj3, proj3, proj3, lw["conv_a_w"], lw["conv_a_b"], lw["norm_a_g"], lw["norm_a_b"],
      lw["conv_b_w"], lw["conv_b_b"], lw["gate_w"], lw["gate_b"], lw["lru_lam"], h0,
      lw["sgu_norm_g"], lw["sgu_norm_b"], lw["sgu_w"], lw["sgu_bias"], lw["conv_d_w"],
      lw["m_head"])
    return y.reshape(batch * seq, D_MODEL), st


def _outproj_kernel(y_ref, x_ref, mod_ref, g_ref, wo_ref, wr_ref, br_ref, tri_ref, cin_ref,
                    xm_ref, hn_ref, ei_ref, pr_ref, cnt_ref, seen_s):
    tm = x_ref.shape[0]
    e = N_EXPERTS

    @pl.when(pl.program_id(0) == 0)
    def _():
        seen_s[...] = cin_ref[...]

    y = jnp.dot(y_ref[...], wo_ref[...], preferred_element_type=F32)
    x = x_ref[...] + mod_ref[2:3, :] * y
    xm_ref[...] = x
    hn = _rmsnorm(x, g_ref[...]) * (1.0 + mod_ref[4:5, :]) + mod_ref[3:4, :]
    _store_token_tiles(hn_ref, hn)

    h1 = hn.astype(BF16)
    r1 = hn - h1.astype(F32)
    h2 = r1.astype(BF16)
    h3 = (r1 - h2.astype(F32)).astype(BF16)
    w_t = wr_ref[...]
    contract_features = (((1,), (1,)), ((), ()))
    p1 = lax.dot_general(w_t, h1, contract_features, preferred_element_type=F32)
    p2 = lax.dot_general(w_t, h2, contract_features, preferred_element_type=F32)
    p3 = lax.dot_general(w_t, h3, contract_features, preferred_element_type=F32)
    logits = (p1[2 * e:3 * e] + p2[e:2 * e] + p3[0:e] + p1[e:2 * e] + p2[0:e] + p1[0:e]) + br_ref[...]

    expert = lax.broadcasted_iota(jnp.int32, (e, tm), 0)
    beats = jnp.zeros((e, tm), F32)
    for other in range(e):
        lo = logits[other:other + 1, :]
        ahead = (lo > logits) | ((lo == logits) & (expert > other))
        beats = beats + jnp.where(ahead, 1.0, 0.0)

    chosen = jnp.where(beats < float(TOP_K), 1.0, 0.0)
    earlier = jnp.dot(chosen.astype(BF16), tri_ref[...], preferred_element_type=F32) + seen_s[...]
    seen_s[...] = earlier[:, tm - 1:tm] + chosen[:, tm - 1:tm]
    cnt_ref[...] = seen_s[...]

    def pick(k, values):
        return jnp.sum(jnp.where(beats == float(k), values, 0.0), axis=0, keepdims=True)

    expert_f = expert.astype(F32)
    vals = [pick(k, logits) for k in range(TOP_K)]
    exps = [jnp.exp(v - vals[0]) for v in vals]
    den = exps[0] + exps[1] + exps[2] + exps[3]
    out_row = lax.broadcasted_iota(jnp.int32, (2 * TOP_K, tm), 0)
    ei = jnp.zeros((2 * TOP_K, tm), F32)
    pr = jnp.zeros((2 * TOP_K, tm), F32)
    for k in range(TOP_K):
        ei = jnp.where(out_row == k, pick(k, expert_f), ei)
        ei = jnp.where(out_row == TOP_K + k, pick(k, earlier), ei)
        pr = jnp.where(out_row == k, exps[k] / den, pr)
    ei_ref[...] = ei.astype(jnp.int32)
    pr_ref[...] = pr


def _out_proj(y, x, mod, g, wo_bf16, w_router, b_router, seen, seq):
    n = x.shape[0]
    tm = min(ROW_TILE, seq)
    row = pl.BlockSpec((tm, D_MODEL), lambda i: (i, 0))
    route = pl.BlockSpec((2 * TOP_K, tm), lambda i: (0, i))
    cnt = pl.BlockSpec((N_EXPERTS, 1), lambda i: (0, 0))
    before = jnp.tri(tm, k=-1, dtype=BF16).T
    w1 = w_router.astype(BF16)
    w2 = (w_router - w1.astype(F32)).astype(BF16)
    w3 = (w_router - w1.astype(F32) - w2.astype(F32)).astype(BF16)
    w_t = jnp.concatenate([w1.T, w2.T, w3.T, jnp.zeros_like(w1.T)], axis=0)
    return pl.pallas_call(
        _outproj_kernel,
        grid=(n // tm,),
        in_specs=[row, row, _mod_spec(mod, seq, tm),
                  pl.BlockSpec((1, D_MODEL), lambda i: (0, 0)),
                  pl.BlockSpec((D_MODEL, D_MODEL), lambda i: (0, 0)),
                  pl.BlockSpec((4 * N_EXPERTS, D_MODEL), lambda i: (0, 0)),
                  cnt, pl.BlockSpec((tm, tm), lambda i: (0, 0)), cnt],
        out_specs=(row, pl.BlockSpec((tm * TOKEN_ROWS, V7X_LANES), lambda i: (i, 0)), route, route, cnt),
        out_shape=(jax.ShapeDtypeStruct((n, D_MODEL), F32),
                   jax.ShapeDtypeStruct((n * TOKEN_ROWS, V7X_LANES), F32),
                   jax.ShapeDtypeStruct((2 * TOP_K, n), jnp.int32),
                   jax.ShapeDtypeStruct((2 * TOP_K, n), F32),
                   jax.ShapeDtypeStruct((N_EXPERTS, 1), F32)),
        scratch_shapes=[pltpu.VMEM((N_EXPERTS, 1), F32)],
        compiler_params=pltpu.CompilerParams(dimension_semantics=("arbitrary",)),
        name="out_proj",
    )(y, x, mod, g.reshape(1, D_MODEL), wo_bf16, w_t, b_router.reshape(N_EXPERTS, 1), before, seen)


def _dispatch_kernel(zero_blk, zero_on, dest_ref, hn_p, hn_s, xbuf, zeros_s, sem, zsem, *, tiles_p):
    i = pl.program_id(0)

    @pl.when(i == 0)
    def _():
        zeros_s[...] = jnp.zeros_like(zeros_s)

        def fill(j):
            return pltpu.make_async_copy(zeros_s, xbuf.at[_token_rows(zero_blk[j] * MOE_TM, MOE_TM)], zsem)

        for j in range(zero_blk.shape[0]):
            @pl.when(zero_on[j] == 1)
            def _():
                fill(j).start()

        for j in range(zero_blk.shape[0]):
            @pl.when(zero_on[j] == 1)
            def _():
                fill(j).wait()

    def issue(src):
        def body(j, carry):
            row = src.at[_token_rows(j)]
            for k in range(TOP_K):
                pltpu.make_async_copy(row, xbuf.at[_token_rows(dest_ref[0, k * DISPATCH_TM + j])],
                                      sem).start(priority=k % DMA_PRIORITIES)
            return carry

        lax.fori_loop(0, DISPATCH_TM, body, 0, unroll=4)

    @pl.when(i < tiles_p)
    def _():
        issue(hn_p)

    @pl.when(i >= tiles_p)
    def _():
        issue(hn_s)

    for _ in range(TOP_K):
        pltpu.make_async_copy(hn_s, xbuf.at[_token_rows(0, DISPATCH_TM)], sem).wait()


def _dispatch(dest, zero_blk, zero_on, hn_p, hn_s, n_pad):
    n_p, n_s = hn_p.shape[0] // TOKEN_ROWS, hn_s.shape[0] // TOKEN_ROWS
    tiles_p = n_p // DISPATCH_TM
    n_steps = (n_p + n_s) // DISPATCH_TM
    tile = (DISPATCH_TM * TOKEN_ROWS, V7X_LANES)
    grid_spec = pltpu.PrefetchScalarGridSpec(
        num_scalar_prefetch=2,
        grid=(n_steps,),
        in_specs=[pl.BlockSpec((None, 1, DISPATCH_TM * TOP_K), lambda i, zb, zo: (i, 0, 0),
                               memory_space=pltpu.SMEM),
                  pl.BlockSpec(tile, lambda i, zb, zo: (jnp.minimum(i, tiles_p - 1), 0)),
                  pl.BlockSpec(tile, lambda i, zb, zo: (jnp.maximum(i - tiles_p, 0), 0))],
        out_specs=pl.BlockSpec(memory_space=pl.ANY),
        scratch_shapes=[pltpu.VMEM((MOE_TM * TOKEN_ROWS, V7X_LANES), F32), pltpu.SemaphoreType.DMA(()),
                        pltpu.SemaphoreType.DMA(())],
    )
    return pl.pallas_call(
        functools.partial(_dispatch_kernel, tiles_p=tiles_p),
        grid_spec=grid_spec,
        out_shape=jax.ShapeDtypeStruct((n_pad * TOKEN_ROWS, V7X_LANES), F32),
        compiler_params=pltpu.CompilerParams(dimension_semantics=("arbitrary",)),
        name="moe_dispatch",
    )(zero_blk, zero_on, _per_tile(dest, DISPATCH_TM), hn_p, hn_s)


def _moe_kernel(blk_e, blk_new, blk_on, blk_slot, blk_next, x_ref, wgu_hbm, bgu_ref, wdn_hbm, bdn_ref, o_ref,
                wgu_f, wdn_f, wgu_s, wdn_s, sems, *, layer):
    i = pl.program_id(0)

    def fetch(expert, slot):
        return (pltpu.make_async_copy(wgu_hbm.at[layer, expert], wgu_f.at[slot], sems.at[0, slot]),
                pltpu.make_async_copy(wdn_hbm.at[layer, expert], wdn_f.at[slot], sems.at[1, slot]))

    @pl.when(i == 0)
    def _():
        for copy in fetch(blk_e[0], 0):
            copy.start()

    @pl.when(blk_on[i] == BLK_UNUSED)
    def _():
        o_ref[...] = jnp.zeros_like(o_ref)

    @pl.when(blk_on[i] != BLK_UNUSED)
    def _():
        @pl.when(blk_new[i] == 1)
        def _():
            slot = blk_slot[i]
            for copy in fetch(blk_e[i], slot):
                copy.wait()

            @pl.when(blk_next[i] >= 0)
            def _():
                for copy in fetch(blk_next[i], 1 - slot):
                    copy.start()

            wgu_s[...] = wgu_f[slot].astype(BF16)
            wdn_s[...] = wdn_f[slot].astype(BF16)

        def expert_rows(n):
            x = jnp.concatenate([c.astype(BF16) for c in _load_token_tiles(x_ref, n)], axis=1)
            gu = jnp.dot(x, wgu_s[...], preferred_element_type=F32) + bgu_ref[...]
            g = jnp.minimum(gu[:, :D_FF], SWIGLU_LIMIT)
            u = jnp.clip(gu[:, D_FF:], -SWIGLU_LIMIT, SWIGLU_LIMIT)
            act = (u + 1.0) * (g * jax.nn.sigmoid(SWIGLU_ALPHA * g))
            _store_token_tiles(o_ref, jnp.dot(act.astype(BF16), wdn_s[...], preferred_element_type=F32)
                               + bdn_ref[...])

        @pl.when(blk_on[i] == BLK_FULL)
        def _():
            expert_rows(MOE_TM)

        @pl.when(blk_on[i] == BLK_HALF)
        def _():
            expert_rows(MOE_TM // 2)
            o_ref[_token_rows(MOE_TM // 2, MOE_TM // 2), :] = jnp.zeros(
                (MOE_TM // 2 * TOKEN_ROWS, V7X_LANES), F32)


def _moe_blocks(x_buf, blk, layer, w_gu, b_gu, w_dn, b_dn):
    n_pad = x_buf.shape[0] // TOKEN_ROWS
    n_blk = n_pad // MOE_TM
    tile = pl.BlockSpec((MOE_TM * TOKEN_ROWS, V7X_LANES), lambda i, e, *_: (i, 0))
    grid_spec = pltpu.PrefetchScalarGridSpec(
        num_scalar_prefetch=len(blk),
        grid=(n_blk,),
        in_specs=[
            tile,
            pl.BlockSpec(memory_space=pl.ANY),
            pl.BlockSpec((None, 1, 2 * D_FF), lambda i, e, *_: (e[i], 0, 0)),
            pl.BlockSpec(memory_space=pl.ANY),
            pl.BlockSpec((None, 1, D_MODEL), lambda i, e, *_: (e[i], 0, 0)),
        ],
        out_specs=tile,
        scratch_shapes=[pltpu.VMEM((2, D_MODEL, 2 * D_FF), F32), pltpu.VMEM((2, D_FF, D_MODEL), F32),
                        pltpu.VMEM((D_MODEL, 2 * D_FF), BF16), pltpu.VMEM((D_FF, D_MODEL), BF16),
                        pltpu.SemaphoreType.DMA((2, 2))],
    )
    return pl.pallas_call(
        functools.partial(_moe_kernel, layer=layer),
        grid_spec=grid_spec,
        out_shape=jax.ShapeDtypeStruct((n_pad * TOKEN_ROWS, V7X_LANES), F32),
        compiler_params=pltpu.CompilerParams(dimension_semantics=("arbitrary",),
                                             vmem_limit_bytes=VMEM_LIMIT),
        name="moe_experts",
    )(*blk, x_buf, w_gu, b_gu.reshape(N_EXPERTS, 1, 2 * D_FF), w_dn, b_dn.reshape(N_EXPERTS, 1, D_MODEL))


def _slot_kernel(start_ref, ei_ref, dest_ref):
    expert, rank = ei_ref[0:TOP_K, :], ei_ref[TOP_K:2 * TOP_K, :]
    start = jnp.zeros_like(expert)
    for e in range(N_EXPERTS):
        start = jnp.where(expert == e, start_ref[e], start)
    dest_ref[...] = start + rank


def _slots(ei, seg_start):
    n = ei.shape[1]
    grid_spec = pltpu.PrefetchScalarGridSpec(
        num_scalar_prefetch=1,
        grid=(n // SLOT_TN,),
        in_specs=[pl.BlockSpec((2 * TOP_K, SLOT_TN), lambda i, s: (0, i))],
        out_specs=pl.BlockSpec((TOP_K, SLOT_TN), lambda i, s: (0, i)),
    )
    return pl.pallas_call(
        _slot_kernel,
        grid_spec=grid_spec,
        out_shape=jax.ShapeDtypeStruct((TOP_K, n), jnp.int32),
        name="moe_slots",
    )(seg_start, ei)


PLAN_ROWS = ("blk_e", "blk_new", "blk_on", "blk_slot", "blk_next", "zero_blk", "zero_on", "seg_start")
PLAN_LANES = 256


def _plan_kernel(cnt_ref, plan_ref, *, n_blk, tail):
    e, lanes = N_EXPERTS, PLAN_LANES
    cnt = cnt_ref[...]
    expert = lax.broadcasted_iota(jnp.int32, (e, lanes), 0).astype(F32)
    lane = lax.broadcasted_iota(jnp.int32, (e, lanes), 1).astype(F32)
    lane_row = lane[0:1, :]

    def as_row(col):
        return jnp.sum(jnp.where(expert == lane, col, 0.0), axis=0, keepdims=True)

    nblk = jnp.floor((cnt + (MOE_TM - 1)) / MOE_TM)
    nblk_row = as_row(nblk)
    end = jnp.sum(jnp.where(lane <= expert, nblk_row, 0.0), axis=1, keepdims=True)
    start = end - nblk
    end_row = jnp.sum(jnp.where(expert <= lane, nblk, 0.0), axis=0, keepdims=True)
    total = end_row[:, e - 1:e]

    blk_e = jnp.minimum(jnp.sum(jnp.where(lane_row >= end, 1.0, 0.0), axis=0, keepdims=True), e - 1.0)
    mine = expert == blk_e

    def per_block(col):
        return jnp.sum(jnp.where(mine, col, 0.0), axis=0, keepdims=True)

    blk_start = per_block(start)
    tokens = jnp.clip(per_block(cnt) - (lane_row - blk_start) * MOE_TM, 0.0, float(MOE_TM))
    blk_on = jnp.where(lane_row >= total, float(BLK_UNUSED),
                       jnp.where(tokens <= MOE_TM // 2, float(BLK_HALF), float(BLK_FULL)))
    blk_new = jnp.where(lane_row == blk_start, 1.0, 0.0)

    used_row = jnp.where(as_row(cnt) > 0.0, 1.0, 0.0)
    runs_before = jnp.sum(jnp.where(lane < expert, used_row, 0.0), axis=1, keepdims=True)
    slot = runs_before - 2.0 * jnp.floor(runs_before / 2.0)
    after = jnp.min(jnp.where((lane > expert) & (used_row > 0.0), lane, float(lanes)), axis=1, keepdims=True)
    after = jnp.where(after >= e, -1.0, after)

    cnt_row = as_row(cnt)
    partly = jnp.where((cnt_row - MOE_TM * jnp.floor(cnt_row / MOE_TM) != 0.0) & (lane_row < e), 1.0, 0.0)
    tail_blk = total + (lane_row - e)
    in_tail = (lane_row >= e) & (lane_row < e + tail) & (tail_blk < n_blk)
    zero_blk = jnp.where(in_tail, tail_blk, (end_row - 1.0) * partly)
    zero_on = jnp.where(in_tail, 1.0, partly)
    seg_start = as_row(start) * MOE_TM

    rows = dict(blk_e=blk_e, blk_new=blk_new, blk_on=blk_on, blk_slot=per_block(slot), blk_next=per_block(after),
                zero_blk=zero_blk, zero_on=zero_on, seg_start=seg_start)
    out_row = lax.broadcasted_iota(jnp.int32, (len(PLAN_ROWS), lanes), 0)
    plan = jnp.zeros((len(PLAN_ROWS), lanes), F32)
    for r, name in enumerate(PLAN_ROWS):
        plan = jnp.where(out_row == r, rows[name], plan)
    plan_ref[...] = plan.astype(jnp.int32)


def _moe(hn_p, hn_s, ei_p, ei_s, counts, layer, w_gu, b_gu, w_dn, b_dn):
    n_asg = (ei_p.shape[1] + ei_s.shape[1]) * TOP_K
    n_pad = (n_asg + N_EXPERTS * (MOE_TM - 1) + MOE_TM - 1) // MOE_TM * MOE_TM
    n_blk = n_pad // MOE_TM
    tail = n_blk - n_asg // MOE_TM
    assert n_blk <= PLAN_LANES and N_EXPERTS + tail <= PLAN_LANES
    plan = pl.pallas_call(
        functools.partial(_plan_kernel, n_blk=n_blk, tail=tail),
        out_shape=jax.ShapeDtypeStruct((len(PLAN_ROWS), PLAN_LANES), jnp.int32),
        name="moe_plan",
    )(counts)
    row = {name: plan[r] for r, name in enumerate(PLAN_ROWS)}
    blk = tuple(row[name][:n_blk] for name in ("blk_e", "blk_new", "blk_on", "blk_slot", "blk_next"))
    n_zero = N_EXPERTS + tail

    dest = _slots(jnp.concatenate([ei_p, ei_s], axis=1), row["seg_start"][:N_EXPERTS])
    dest_p, dest_s = dest[:, :ei_p.shape[1]], dest[:, ei_p.shape[1]:]
    x_buf = _dispatch(dest, row["zero_blk"][:n_zero], row["zero_on"][:n_zero], hn_p, hn_s, n_pad)
    y_buf = _moe_blocks(x_buf, blk, layer, w_gu, b_gu, w_dn, b_dn)
    return dest_p, dest_s, y_buf


def _final_kernel(dest_cur, dest_nxt, x_ref, pr_ref, ybuf, mod_ref, g_ref, o_ref, rows, moe_s, sems):
    x = x_ref[...] + mod_ref[5:6, :] * _combine_experts(dest_cur, dest_nxt, pr_ref, ybuf, rows, moe_s, sems)
    o_ref[...] = _rmsnorm(x, g_ref[...])


def _final_norm(x, route, mod, g, seq):
    n = x.shape[0]
    tm = min(ROW_TILE, seq)
    row = pl.BlockSpec((tm, D_MODEL), lambda i: (i, 0))
    args, specs, scratch = _combine_operands(*route, n, tm)
    return pl.pallas_call(
        _final_kernel,
        grid=(n // tm,),
        in_specs=specs[:2] + [row] + specs[2:] + [_mod_spec(mod, seq, tm),
                                                  pl.BlockSpec((1, D_MODEL), lambda i: (0, 0))],
        out_specs=row,
        out_shape=jax.ShapeDtypeStruct((n, D_MODEL), F32),
        scratch_shapes=scratch,
        compiler_params=pltpu.CompilerParams(dimension_semantics=("arbitrary",), vmem_limit_bytes=VMEM_LIMIT),
        name="final_norm",
    )(*args[:2], x, *args[2:], mod, g.reshape(1, D_MODEL))


def _block_diag(w):
    eye = jnp.eye(HEADS, dtype=w.dtype)
    return (eye[:, None, :, None] * w[:, :, None, :]).reshape(W_GRP, W_GRP)


def _layer_weights(l, p):
    gate_w = jnp.concatenate([_block_diag(p["lru_wr"][l, 0]), _block_diag(p["lru_wi"][l, 0]),
                              _block_diag(p["lru_wr"][l, 1]), _block_diag(p["lru_wi"][l, 1])], axis=1)
    gate_b = jnp.concatenate([p["lru_br"][l, 0], p["lru_bi"][l, 0], p["lru_br"][l, 1], p["lru_bi"][l, 1]])
    head_of = jnp.arange(W_GRP) // HD
    row = lambda v: v.reshape(1, W_GRP)
    return dict(
        conv_a_w=p["conv_a_w"][l], conv_a_b=row(p["conv_a_b"][l]),
        norm_a_g=row(p["norm_a_g"][l]), norm_a_b=row(p["norm_a_b"][l]),
        conv_b_w=p["conv_b_w"][l], conv_b_b=row(p["conv_b_b"][l]),
        gate_w=gate_w.astype(BF16), gate_b=gate_b.reshape(1, 4 * W_GRP), lru_lam=p["lru_lam"][l],
        sgu_norm_g=row(p["sgu_norm_g"][l]), sgu_norm_b=row(p["sgu_norm_b"][l]),
        sgu_w=p["sgu_w"][l].astype(BF16), sgu_bias=jnp.repeat(p["sgu_b"][l].T, HD, axis=1),
        conv_d_w=p["conv_d_w"][l],
        m_head=((head_of[:, None] == head_of[None, :]).astype(F32) / HD).astype(BF16),
    )


def kernel(x_prompt, x_sample, state_rglru, c, c_ctx, w_ada, b_ada, norm1_g, norm2_g, w_in, conv_a_w,
           conv_a_b, norm_a_g, norm_a_b, conv_b_w, conv_b_b, lru_wr, lru_br, lru_wi, lru_bi, lru_lam,
           sgu_norm_g, sgu_norm_b, sgu_w, sgu_b, conv_d_w, w_out, w_router, b_router, w_gu, b_gu, w_dn,
           b_dn, final_g):
    p = dict(conv_a_w=conv_a_w, conv_a_b=conv_a_b, norm_a_g=norm_a_g, norm_a_b=norm_a_b,
             conv_b_w=conv_b_w, conv_b_b=conv_b_b, lru_wr=lru_wr, lru_br=lru_br, lru_wi=lru_wi,
             lru_bi=lru_bi, lru_lam=lru_lam, sgu_norm_g=sgu_norm_g, sgu_norm_b=sgu_norm_b,
             sgu_w=sgu_w, sgu_b=sgu_b, conv_d_w=conv_d_w)
    bp, tp, _ = x_prompt.shape
    bs, ts, _ = x_sample.shape
    n_p, n_s = bp * tp, bs * ts

    cond_rows = jnp.zeros((COND_ROWS, D_MODEL), F32).at[0].set(c_ctx).at[1:1 + bs].set(c)
    mod = _ada_mod(cond_rows, w_ada, b_ada).reshape(DEPTH, COND_ROWS, 6, D_MODEL)

    xp = x_prompt.reshape(n_p, D_MODEL)
    xs = x_sample.reshape(n_s, D_MODEL)
    h0_ctx = jnp.zeros((bp, 2, W_GRP), F32)
    route_p = route_s = mod_p_prev = mod_s_prev = None
    no_tokens_seen = jnp.zeros((N_EXPERTS, 1), F32)
    states = []
    for l in range(DEPTH):
        lw = _layer_weights(l, p)
        mod_p, mod_s = mod[l, 0:1], mod[l, 1:1 + bs]
        w_in_l = w_in[l].astype(BF16)
        w_out_l = w_out[l].astype(BF16)
        xp, proj_p = _in_proj(xp, route_p, mod_p_prev, mod_p, norm1_g[l], w_in_l, tp)
        xs, proj_s = _in_proj(xs, route_s, mod_s_prev, mod_s, norm1_g[l], w_in_l, ts)
        y_p, st = _mixers(proj_p, h0_ctx, lw, bp, tp, False)
        y_s, _ = _mixers(proj_s, state_rglru[:, l], lw, bs, ts, True)
        states.append(st)
        xp, hn_p, ei_p, pr_p, seen = _out_proj(y_p, xp, mod_p, norm2_g[l], w_out_l, w_router[l],
                                               b_router[l], no_tokens_seen, tp)
        xs, hn_s, ei_s, pr_s, counts = _out_proj(y_s, xs, mod_s, norm2_g[l], w_out_l, w_router[l],
                                                 b_router[l], seen, ts)
        dest_p, dest_s, y_buf = _moe(hn_p, hn_s, ei_p, ei_s, counts, l, w_gu, b_gu[l], w_dn, b_dn[l])
        route_p, route_s = (dest_p, pr_p, y_buf), (dest_s, pr_s, y_buf)
        mod_p_prev, mod_s_prev = mod_p, mod_s
    y_prompt = _final_norm(xp, route_p, mod_p_prev, final_g, tp).reshape(bp, tp, D_MODEL)
    y_sample = _final_norm(xs, route_s, mod_s_prev, final_g, ts).reshape(bs, ts, D_MODEL)
    return y_prompt, y_sample, jnp.stack(states, axis=1)
```

```python
import functools

import jax
import jax.numpy as jnp
from jax import lax
from jax.experimental import pallas as pl
from jax.experimental.pallas import tpu as pltpu

F32 = jnp.float32
BF16 = jnp.bfloat16

D_MODEL = 1024
DEPTH = 2
GRID_W = 64
N_MIXERS = 4
W_GRP = D_MODEL // N_MIXERS
HEADS = 4
HD = W_GRP // HEADS
N_IN = 9 * W_GRP
CONV_A = 31
CONV_B = 4
CONV_D = 3
CHUNK = 128
LRU_C = 8.0
N_EXPERTS = 32
TOP_K = 4
D_FF = D_MODEL
SWIGLU_LIMIT = 7.0
SWIGLU_ALPHA = 1.702
EPS = 1e-6

V7X_SUBLANES = 8
V7X_LANES = 128
TOKEN_ROWS = D_MODEL // V7X_LANES
DMA_PRIORITIES = 2
V7X_VMEM_BYTES = 64 * 1024 * 1024
VMEM_LIMIT = V7X_VMEM_BYTES * 7 // 8

COND_ROWS = 16
ADA_TN = 1536
ROW_TILE = 512
CONV_ROWS = GRID_W
CONV_PAD = 16
NORM_ROWS = 256
MIX_ROWS = 1024
MOE_TM = 512
BLK_UNUSED, BLK_HALF, BLK_FULL = 0, 1, 2
DISPATCH_TM = 512
SLOT_TN = 2048
COMBINE_ROWS = 32


def _rmsnorm(x, g):
    return x * lax.rsqrt(jnp.mean(x * x, axis=-1, keepdims=True) + EPS) * g


def _group_mean(x, m_ref):
    hi = x.astype(BF16)
    lo = (x - hi.astype(F32)).astype(BF16)
    m = m_ref[...]
    return (jnp.dot(hi, m, preferred_element_type=F32) + jnp.dot(lo, m, preferred_element_type=F32))


def _layernorm(x, g, b, m_ref):
    xc = x - _group_mean(x, m_ref)
    var = _group_mean(xc * xc, m_ref)
    return xc * lax.rsqrt(var + EPS) * g + b


def _ada_kernel(c_ref, w_ref, b_ref, o_ref):
    c = c_ref[...]
    cond = (c * jax.nn.sigmoid(c)).astype(BF16)
    o_ref[...] = jnp.dot(cond, w_ref[...].astype(BF16), preferred_element_type=F32) + b_ref[...]


def _ada_mod(cond_rows, w_ada, b_ada):
    n_col = w_ada.shape[-1]
    return pl.pallas_call(
        _ada_kernel,
        grid=(DEPTH, n_col // ADA_TN),
        in_specs=[
            pl.BlockSpec((COND_ROWS, D_MODEL), lambda l, j: (0, 0)),
            pl.BlockSpec((None, D_MODEL, ADA_TN), lambda l, j: (l, 0, j)),
            pl.BlockSpec((None, 1, ADA_TN), lambda l, j: (l, 0, j)),
        ],
        out_specs=pl.BlockSpec((None, COND_ROWS, ADA_TN), lambda l, j: (l, 0, j)),
        out_shape=jax.ShapeDtypeStruct((DEPTH, COND_ROWS, n_col), F32),
        name="ada_mod",
    )(cond_rows, w_ada, b_ada.reshape(DEPTH, 1, n_col))


def _token_rows(t, count=1):
    return pl.ds(pl.multiple_of(t * TOKEN_ROWS, TOKEN_ROWS), count * TOKEN_ROWS)


def _store_token_tiles(ref, x):
    n = x.shape[0]
    for c in range(TOKEN_ROWS):
        ref[pl.ds(c, n, stride=TOKEN_ROWS), :] = x[:, c * V7X_LANES:(c + 1) * V7X_LANES]


def _load_token_tiles(ref, n):
    return [ref[pl.ds(c, n, stride=TOKEN_ROWS), :] for c in range(TOKEN_ROWS)]


def _combine_experts(dest_cur, dest_nxt, pr_ref, ybuf, rows, moe_s, sems):
    i = pl.program_id(0)
    n_steps = pl.num_programs(0)
    tm = rows.shape[2] // TOKEN_ROWS
    slot = i % 2

    def row_copy(dref, j, k, slot_):
        return pltpu.make_async_copy(ybuf.at[_token_rows(dref[0, k * tm + j])],
                                     rows.at[slot_, k, _token_rows(j)], sems.at[slot_])

    def issue(dref, slot_):
        def body(j, carry):
            for k in range(TOP_K):
                row_copy(dref, j, k, slot_).start(priority=k % DMA_PRIORITIES)
            return carry

        lax.fori_loop(0, tm, body, 0, unroll=4)

    @pl.when(i == 0)
    def _():
        issue(dest_cur, 0)

    for nxt in (0, 1):
        @pl.when((i + 1 < n_steps) & (slot != nxt))
        def _():
            issue(dest_nxt, nxt)

    for k in range(TOP_K):
        pltpu.make_async_copy(ybuf.at[_token_rows(0, tm)], rows.at[slot, k], sems.at[slot]).wait()
    pad = jnp.zeros((V7X_LANES - pr_ref.shape[0], tm), F32)
    probs = jnp.concatenate([pr_ref[...], pad], axis=0).T
    for t0 in range(0, tm, COMBINE_ROWS):
        weight = [probs[t0:t0 + COMBINE_ROWS, k:k + 1] for k in range(TOP_K)]
        for c in range(TOKEN_ROWS):
            chunk = pl.ds(t0 * TOKEN_ROWS + c, COMBINE_ROWS, stride=TOKEN_ROWS)
            acc = weight[0] * rows.at[slot, 0][chunk, :]
            for k in range(1, TOP_K):
                acc = acc + weight[k] * rows.at[slot, k][chunk, :]
            moe_s[t0:t0 + COMBINE_ROWS, c * V7X_LANES:(c + 1) * V7X_LANES] = acc
    return moe_s[...]


def _inproj_kernel(*refs, has_res):
    if has_res:
        (dest_cur, dest_nxt, x_ref, pr_ref, ybuf, modp_ref, mod_ref, g_ref, w_ref,
         xo_ref, p_ref, rows, moe_s, sems) = refs
        x = x_ref[...] + modp_ref[5:6, :] * _combine_experts(dest_cur, dest_nxt, pr_ref, ybuf, rows, moe_s, sems)
        xo_ref[...] = x
    else:
        x_ref, mod_ref, g_ref, w_ref, p_ref = refs
        x = x_ref[...]
    hn = _rmsnorm(x, g_ref[...]) * (1.0 + mod_ref[1:2, :]) + mod_ref[0:1, :]
    p_ref[...] = jnp.dot(hn.astype(BF16), w_ref[...], preferred_element_type=F32)


def _row_tile(mod, seq):
    return ROW_TILE if mod.shape[0] == 1 else min(ROW_TILE, seq)


def _mod_spec(mod, seq, tm):
    if mod.shape[0] == 1:
        return pl.BlockSpec((None, 6, D_MODEL), lambda i: (0, 0, 0))
    return pl.BlockSpec((None, 6, D_MODEL), lambda i: ((i * tm) // seq, 0, 0))


def _per_tile(dest, tm):
    n_steps = dest.shape[1] // tm
    return dest.reshape(TOP_K, n_steps, tm).transpose(1, 0, 2).reshape(n_steps, 1, TOP_K * tm)


def _combine_operands(dest, probs, y_buf, n, tm):
    n_steps = n // tm
    dest3 = _per_tile(dest, tm)
    smem = functools.partial(pl.BlockSpec, (None, 1, tm * TOP_K), memory_space=pltpu.SMEM)
    args = [dest3, dest3, probs, y_buf]
    specs = [smem(lambda i: (i, 0, 0)),
             smem(lambda i: (jnp.minimum(i + 1, n_steps - 1), 0, 0)),
             pl.BlockSpec((2 * TOP_K, tm), lambda i: (0, i)),
             pl.BlockSpec(memory_space=pl.ANY)]
    scratch = [pltpu.VMEM((2, TOP_K, tm * TOKEN_ROWS, V7X_LANES), F32), pltpu.VMEM((tm, D_MODEL), F32),
               pltpu.SemaphoreType.DMA((2,))]
    return args, specs, scratch


def _in_proj(x, route, mod_prev, mod, g, w_bf16, seq):
    n = x.shape[0]
    tm = _row_tile(mod, seq)
    row = pl.BlockSpec((tm, D_MODEL), lambda i: (i, 0))
    has_res = route is not None
    args, specs, scratch = [], [], []
    if has_res:
        args, specs, scratch = _combine_operands(*route, n, tm)
        args = args[:2] + [x] + args[2:] + [mod_prev]
        specs = specs[:2] + [row] + specs[2:] + [_mod_spec(mod_prev, seq, tm)]
    else:
        args, specs = [x], [row]
    args += [mod, g.reshape(1, D_MODEL), w_bf16]
    specs += [_mod_spec(mod, seq, tm),
              pl.BlockSpec((1, D_MODEL), lambda i: (0, 0)),
              pl.BlockSpec((D_MODEL, N_IN), lambda i: (0, 0))]
    proj_shape = jax.ShapeDtypeStruct((n, N_IN), F32)
    proj_spec = pl.BlockSpec((tm, N_IN), lambda i: (i, 0))
    if has_res:
        out_shape = (jax.ShapeDtypeStruct((n, D_MODEL), F32), proj_shape)
        out_specs = (row, proj_spec)
    else:
        out_shape, out_specs = proj_shape, proj_spec
    out = pl.pallas_call(
        functools.partial(_inproj_kernel, has_res=has_res),
        grid=(n // tm,),
        in_specs=specs,
        out_specs=out_specs,
        out_shape=out_shape,
        scratch_shapes=scratch,
        compiler_params=pltpu.CompilerParams(dimension_semantics=("arbitrary",), vmem_limit_bytes=VMEM_LIMIT),
        name="in_proj",
    )(*args)
    return out if has_res else (x, out)


def _group_scan(a, b, reverse):
    shape = a.shape
    grouped = (shape[0] // V7X_SUBLANES, V7X_SUBLANES, shape[1])
    a, b = a.reshape(grouped), b.reshape(grouped)
    ri = lax.broadcasted_iota(jnp.int32, grouped, 1)
    for d in (1, 2, 4):
        shift = V7X_SUBLANES - d if reverse else d
        keep = ri < V7X_SUBLANES - d if reverse else ri >= d
        ra, rb = pltpu.roll(a, shift, 1), pltpu.roll(b, shift, 1)
        b = a * jnp.where(keep, rb, 0.0) + b
        a = a * jnp.where(keep, ra, 1.0)
    return a.reshape(shape), b.reshape(shape)


def _mixer_kernel(s0, s1, s2, caw, cab, nag, nab, cbw, cbb, wg, bg, lam, h0, sng, snb, sw, sbias, cdw,
                  m_head, y_ref, st_ref, *scratch, seq, on_grid):
    def one_sequence(b, carry):
        _mix_sequence(s0.at[b], s1.at[b], s2.at[b], caw, cab, nag, nab, cbw, cbb, wg, bg, lam, h0.at[b],
                      sng, snb, sw, sbias, cdw, m_head, y_ref.at[b], st_ref.at[b], *scratch,
                      seq=seq, on_grid=on_grid)
        return carry

    lax.fori_loop(0, s0.shape[0], one_sequence, 0)


def _mix_sequence(s0, s1, s2, caw, cab, nag, nab, cbw, cbb, wg, bg, lam, h0, sng, snb, sw, sbias, cdw,
                  m_head, y_ref, st_ref, pad_s, af_s, bf_s, ab_s, bb_s, *, seq, on_grid):
    mixer = pl.program_id(1)
    n_conv = seq // CONV_ROWS
    win_rows = CONV_ROWS + 2 * CONV_PAD

    def rows_at(c, size):
        return pl.ds(pl.multiple_of(c * size, size), size)

    def fill_padded(fn):
        zeros = jnp.zeros((CONV_PAD, W_GRP), F32)
        pad_s[0:CONV_PAD, :] = zeros
        pad_s[CONV_PAD + seq:2 * CONV_PAD + seq, :] = zeros

        def body(c, carry):
            dst = pl.ds(pl.multiple_of(c * CONV_ROWS + CONV_PAD, V7X_SUBLANES), CONV_ROWS)
            pad_s[dst, :] = fn(rows_at(c, CONV_ROWS))
            return carry

        lax.fori_loop(0, n_conv, body, 0)

    def conv_window(win, w_ref, taps, pad_l):
        acc = jnp.zeros((CONV_ROWS, W_GRP), F32)
        for mis in range(V7X_SUBLANES):
            starts = [(k, CONV_PAD - pad_l + k) for k in range(taps)
                      if (CONV_PAD - pad_l + k) % V7X_SUBLANES == mis]
            if not starts:
                continue
            shifted = pltpu.roll(win, win_rows - mis, 0) if mis else win
            for k, start in starts:
                acc = acc + w_ref[k:k + 1, :] * shifted[start - mis:start - mis + CONV_ROWS, :]
        return acc

    def conv_rows(c, w_ref, taps, pad_l):
        win = pad_s[pl.ds(pl.multiple_of(c * CONV_ROWS, CONV_ROWS), win_rows), :]
        return conv_window(win, w_ref, taps, pad_l)

    @pl.when(mixer == 0)
    def _conformer():
        def glu(rows):
            return s0[rows, :] * jax.nn.sigmoid(s1[rows, :])

        if not on_grid:
            fill_padded(glu)

        def body(c, carry):
            if on_grid:
                edge = jnp.zeros((CONV_PAD, W_GRP), F32)
                win = jnp.concatenate([edge, glu(rows_at(c, CONV_ROWS)), edge], axis=0)
                z = conv_window(win, caw, CONV_A, CONV_A // 2)
            else:
                z = conv_rows(c, caw, CONV_A, CONV_A // 2)
            af_s[rows_at(c, CONV_ROWS), :] = z + cab[...]
            return carry

        lax.fori_loop(0, n_conv, body, 0)

        def norm(c, carry):
            rows = rows_at(c, NORM_ROWS)
            z = _layernorm(af_s[rows, :], nag[...], nab[...], m_head)
            y_ref[rows, :] = (z * jax.nn.sigmoid(z)).astype(BF16)
            return carry

        lax.fori_loop(0, seq // NORM_ROWS, norm, 0, unroll=2)

    @pl.when(mixer == 1)
    def _rglru():
        fill_padded(lambda rows: s0[rows, :])
        lam_v = lam[...]
        softplus_neg = jnp.maximum(-lam_v, 0.0) + jnp.log1p(jnp.exp(-jnp.abs(lam_v)))

        def gates(c, carry):
            rows = rows_at(c, CONV_ROWS)
            xc = conv_rows(c, cbw, CONV_B, 2) + cbb[...]
            gt = jax.nn.sigmoid(jnp.dot(xc.astype(BF16), wg[...], preferred_element_type=F32) + bg[...])
            for d, (a_s, b_s) in enumerate(((af_s, bf_s), (ab_s, bb_s))):
                r_gate = gt[:, (2 * d) * W_GRP:(2 * d + 1) * W_GRP]
                i_gate = gt[:, (2 * d + 1) * W_GRP:(2 * d + 2) * W_GRP]
                log_a = -LRU_C * r_gate * softplus_neg[d:d + 1, :]
                a = jnp.exp(log_a)
                b = jnp.sqrt(jnp.maximum(-jnp.tanh(log_a) * (a * a + 1.0), 0.0)) * (i_gate * xc)
                a, b = _group_scan(a, b, reverse=(d == 1))
                a_s[rows, :] = a
                b_s[rows, :] = b
            return carry

        lax.fori_loop(0, n_conv, gates, 0, unroll=2)

        n_grp = seq // V7X_SUBLANES

        def chain(g, carry):
            cf, cb = carry
            rf = rows_at(g, V7X_SUBLANES)
            rb = rows_at(n_grp - 1 - g, V7X_SUBLANES)
            hf = af_s[rf, :] * cf + bf_s[rf, :]
            bf_s[rf, :] = hf
            hb = ab_s[rb, :] * cb + bb_s[rb, :]
            bb_s[rb, :] = hb
            cf = jnp.broadcast_to(hf[V7X_SUBLANES - 1:V7X_SUBLANES, :], (V7X_SUBLANES, W_GRP))
            cb = jnp.broadcast_to(hb[0:1, :], (V7X_SUBLANES, W_GRP))
            return cf, cb

        init = (jnp.broadcast_to(h0[0:1, :], (V7X_SUBLANES, W_GRP)),
                jnp.broadcast_to(h0[1:2, :], (V7X_SUBLANES, W_GRP)))
        cf, cb = lax.fori_loop(0, n_grp, chain, init, unroll=4)
        st_ref[0:1, :] = cf[0:1, :]
        st_ref[1:2, :] = cb[0:1, :]

        def out(c, carry):
            rows = rows_at(c, CONV_ROWS)
            y_ref[rows, :] = (jax.nn.gelu(s1[rows, :]) * (bf_s[rows, :] + bb_s[rows, :])).astype(BF16)
            return carry

        lax.fori_loop(0, n_conv, out, 0)

    @pl.when(mixer == 2)
    def _sgu():
        lane = lax.broadcasted_iota(jnp.int32, (CHUNK, W_GRP), 1)

        def body(n, carry):
            rows = rows_at(n, CHUNK)
            v = s1[rows, :]
            vc = v - jnp.mean(v, axis=-1, keepdims=True)
            var = jnp.mean(vc * vc, axis=-1, keepdims=True)
            v = (vc * lax.rsqrt(var + EPS) * sng[...] + snb[...]).astype(BF16)
            s = sbias[...]
            for h in range(HEADS):
                sh = jnp.dot(sw[h], v, preferred_element_type=F32)
                s = s + jnp.where((lane >= h * HD) & (lane < (h + 1) * HD), sh, 0.0)
            y_ref[rows, :] = (s0[rows, :] * s).astype(BF16)
            return carry

        lax.fori_loop(0, seq // CHUNK, body, 0, unroll=4)

    @pl.when(mixer == 3)
    def _gated_conv():
        fill_padded(lambda rows: s1[rows, :] * s2[rows, :])

        def body(c, carry):
            rows = rows_at(c, CONV_ROWS)
            y_ref[rows, :] = (s0[rows, :] * conv_rows(c, cdw, CONV_D, CONV_D // 2)).astype(BF16)
            return carry

        lax.fori_loop(0, n_conv, body, 0)


def _mixers(proj, h0, lw, batch, seq, on_grid):
    proj3 = proj.reshape(batch, seq, N_IN)
    seqs = max(1, MIX_ROWS // seq)
    slab = (seqs, seq, W_GRP)
    state = pl.BlockSpec((seqs, 2, W_GRP), lambda b, m: (b, 0, 0))

    def const(shape):
        return pl.BlockSpec(shape, lambda b, m: (0,) * len(shape))

    in_specs = [
        pl.BlockSpec(slab, lambda b, m: (b, 0, 2 * m)),
        pl.BlockSpec(slab, lambda b, m: (b, 0, 2 * m + 1)),
        pl.BlockSpec(slab, lambda b, m: (b, 0, 8)),
        const((CONV_A, W_GRP)), const((1, W_GRP)), const((1, W_GRP)), const((1, W_GRP)),
        const((CONV_B, W_GRP)), const((1, W_GRP)),
        const((W_GRP, 4 * W_GRP)), const((1, 4 * W_GRP)), const((2, W_GRP)),
        state,
        const((1, W_GRP)), const((1, W_GRP)), const((HEADS, CHUNK, CHUNK)), const((CHUNK, W_GRP)),
        const((CONV_D, W_GRP)), const((W_GRP, W_GRP)),
    ]
    y, st = pl.pallas_call(
        functools.partial(_mixer_kernel, seq=seq, on_grid=on_grid),
        grid=(batch // seqs, N_MIXERS),
        in_specs=in_specs,
        out_specs=(pl.BlockSpec(slab, lambda b, m: (b, 0, m)), state),
        out_shape=(jax.ShapeDtypeStruct((batch, seq, D_MODEL), BF16),
                   jax.ShapeDtypeStruct((batch, 2, W_GRP), F32)),
        scratch_shapes=[pltpu.VMEM((seq + 2 * CONV_PAD, W_GRP), F32)] + [pltpu.VMEM((seq, W_GRP), F32)] * 4,
        compiler_params=pltpu.CompilerParams(dimension_semantics=("arbitrary", "arbitrary"),
                                             vmem_limit_bytes=VMEM_LIMIT),
        name="mixers",
    )(proj3, proj3, proj3, lw["conv_a_w"], lw["conv_a_b"], lw["norm_a_g"], lw["norm_a_b"],
      lw["conv_b_w"], lw["conv_b_b"], lw["gate_w"], lw["gate_b"], lw["lru_lam"], h0,
      lw["sgu_norm_g"], lw["sgu_norm_b"], lw["sgu_w"], lw["sgu_bias"], lw["conv_d_w"],
      lw["m_head"])
    return y.reshape(batch * seq, D_MODEL), st


def _outproj_kernel(y_ref, x_ref, mod_ref, g_ref, wo_ref, wr_ref, br_ref, tri_ref, cin_ref,
                    xm_ref, hn_ref, ei_ref, pr_ref, cnt_ref, seen_s):
    tm = x_ref.shape[0]
    e = N_EXPERTS

    @pl.when(pl.program_id(0) == 0)
    def _():
        seen_s[...] = cin_ref[...]

    y = jnp.dot(y_ref[...], wo_ref[...], preferred_element_type=F32)
    x = x_ref[...] + mod_ref[2:3, :] * y
    xm_ref[...] = x
    hn = _rmsnorm(x, g_ref[...]) * (1.0 + mod_ref[4:5, :]) + mod_ref[3:4, :]
    _store_token_tiles(hn_ref, hn)

    h1 = hn.astype(BF16)
    r1 = hn - h1.astype(F32)
    h2 = r1.astype(BF16)
    h3 = (r1 - h2.astype(F32)).astype(BF16)
    w_t = wr_ref[...]
    contract_features = (((1,), (1,)), ((), ()))
    p1 = lax.dot_general(w_t, h1, contract_features, preferred_element_type=F32)
    p2 = lax.dot_general(w_t, h2, contract_features, preferred_element_type=F32)
    p3 = lax.dot_general(w_t, h3, contract_features, preferred_element_type=F32)
    logits = (p1[2 * e:3 * e] + p2[e:2 * e] + p3[0:e] + p1[e:2 * e] + p2[0:e] + p1[0:e]) + br_ref[...]

    expert = lax.broadcasted_iota(jnp.int32, (e, tm), 0)
    beats = jnp.zeros((e, tm), F32)
    for other in range(e):
        lo = logits[other:other + 1, :]
        ahead = (lo > logits) | ((lo == logits) & (expert > other))
        beats = beats + jnp.where(ahead, 1.0, 0.0)

    chosen = jnp.where(beats < float(TOP_K), 1.0, 0.0)
    earlier = jnp.dot(chosen.astype(BF16), tri_ref[...], preferred_element_type=F32) + seen_s[...]
    seen_s[...] = earlier[:, tm - 1:tm] + chosen[:, tm - 1:tm]
    cnt_ref[...] = seen_s[...]

    def pick(k, values):
        return jnp.sum(jnp.where(beats == float(k), values, 0.0), axis=0, keepdims=True)

    expert_f = expert.astype(F32)
    vals = [pick(k, logits) for k in range(TOP_K)]
    exps = [jnp.exp(v - vals[0]) for v in vals]
    den = exps[0] + exps[1] + exps[2] + exps[3]
    out_row = lax.broadcasted_iota(jnp.int32, (2 * TOP_K, tm), 0)
    ei = jnp.zeros((2 * TOP_K, tm), F32)
    pr = jnp.zeros((2 * TOP_K, tm), F32)
    for k in range(TOP_K):
        ei = jnp.where(out_row == k, pick(k, expert_f), ei)
        ei = jnp.where(out_row == TOP_K + k, pick(k, earlier), ei)
        pr = jnp.where(out_row == k, exps[k] / den, pr)
    ei_ref[...] = ei.astype(jnp.int32)
    pr_ref[...] = pr


def _out_proj(y, x, mod, g, wo_bf16, w_router, b_router, seen, seq):
    n = x.shape[0]
    tm = _row_tile(mod, seq)
    row = pl.BlockSpec((tm, D_MODEL), lambda i: (i, 0))
    route = pl.BlockSpec((2 * TOP_K, tm), lambda i: (0, i))
    cnt = pl.BlockSpec((N_EXPERTS, 1), lambda i: (0, 0))
    before = jnp.tri(tm, k=-1, dtype=BF16).T
    w1 = w_router.astype(BF16)
    w2 = (w_router - w1.astype(F32)).astype(BF16)
    w3 = (w_router - w1.astype(F32) - w2.astype(F32)).astype(BF16)
    w_t = jnp.concatenate([w1.T, w2.T, w3.T, jnp.zeros_like(w1.T)], axis=0)
    return pl.pallas_call(
        _outproj_kernel,
        grid=(n // tm,),
        in_specs=[row, row, _mod_spec(mod, seq, tm),
                  pl.BlockSpec((1, D_MODEL), lambda i: (0, 0)),
                  pl.BlockSpec((D_MODEL, D_MODEL), lambda i: (0, 0)),
                  pl.BlockSpec((4 * N_EXPERTS, D_MODEL), lambda i: (0, 0)),
                  cnt, pl.BlockSpec((tm, tm), lambda i: (0, 0)), cnt],
        out_specs=(row, pl.BlockSpec((tm * TOKEN_ROWS, V7X_LANES), lambda i: (i, 0)), route, route, cnt),
        out_shape=(jax.ShapeDtypeStruct((n, D_MODEL), F32),
                   jax.ShapeDtypeStruct((n * TOKEN_ROWS, V7X_LANES), F32),
                   jax.ShapeDtypeStruct((2 * TOP_K, n), jnp.int32),
                   jax.ShapeDtypeStruct((2 * TOP_K, n), F32),
                   jax.ShapeDtypeStruct((N_EXPERTS, 1), F32)),
        scratch_shapes=[pltpu.VMEM((N_EXPERTS, 1), F32)],
        compiler_params=pltpu.CompilerParams(dimension_semantics=("arbitrary",)),
        name="out_proj",
    )(y, x, mod, g.reshape(1, D_MODEL), wo_bf16, w_t, b_router.reshape(N_EXPERTS, 1), before, seen)


def _dispatch_kernel(zero_blk, zero_on, dest_ref, hn_p, hn_s, xbuf, zeros_s, sem, zsem, *, tiles_p):
    i = pl.program_id(0)

    @pl.when(i == 0)
    def _():
        zeros_s[...] = jnp.zeros_like(zeros_s)

        def fill(j):
            return pltpu.make_async_copy(zeros_s, xbuf.at[_token_rows(zero_blk[j] * MOE_TM, MOE_TM)], zsem)

        for j in range(zero_blk.shape[0]):
            @pl.when(zero_on[j] == 1)
            def _():
                fill(j).start()

        for j in range(zero_blk.shape[0]):
            @pl.when(zero_on[j] == 1)
            def _():
                fill(j).wait()

    def issue(src):
        def body(j, carry):
            row = src.at[_token_rows(j)]
            for k in range(TOP_K):
                pltpu.make_async_copy(row, xbuf.at[_token_rows(dest_ref[0, k * DISPATCH_TM + j])],
                                      sem).start(priority=k % DMA_PRIORITIES)
            return carry

        lax.fori_loop(0, DISPATCH_TM, body, 0, unroll=4)

    @pl.when(i < tiles_p)
    def _():
        issue(hn_p)

    @pl.when(i >= tiles_p)
    def _():
        issue(hn_s)

    for _ in range(TOP_K):
        pltpu.make_async_copy(hn_s, xbuf.at[_token_rows(0, DISPATCH_TM)], sem).wait()


def _dispatch(dest, zero_blk, zero_on, hn_p, hn_s, n_pad):
    n_p, n_s = hn_p.shape[0] // TOKEN_ROWS, hn_s.shape[0] // TOKEN_ROWS
    tiles_p = n_p // DISPATCH_TM
    n_steps = (n_p + n_s) // DISPATCH_TM
    tile = (DISPATCH_TM * TOKEN_ROWS, V7X_LANES)
    grid_spec = pltpu.PrefetchScalarGridSpec(
        num_scalar_prefetch=2,
        grid=(n_steps,),
        in_specs=[pl.BlockSpec((None, 1, DISPATCH_TM * TOP_K), lambda i, zb, zo: (i, 0, 0),
                               memory_space=pltpu.SMEM),
                  pl.BlockSpec(tile, lambda i, zb, zo: (jnp.minimum(i, tiles_p - 1), 0)),
                  pl.BlockSpec(tile, lambda i, zb, zo: (jnp.maximum(i - tiles_p, 0), 0))],
        out_specs=pl.BlockSpec(memory_space=pl.ANY),
        scratch_shapes=[pltpu.VMEM((MOE_TM * TOKEN_ROWS, V7X_LANES), F32), pltpu.SemaphoreType.DMA(()),
                        pltpu.SemaphoreType.DMA(())],
    )
    return pl.pallas_call(
        functools.partial(_dispatch_kernel, tiles_p=tiles_p),
        grid_spec=grid_spec,
        out_shape=jax.ShapeDtypeStruct((n_pad * TOKEN_ROWS, V7X_LANES), F32),
        compiler_params=pltpu.CompilerParams(dimension_semantics=("arbitrary",)),
        name="moe_dispatch",
    )(zero_blk, zero_on, _per_tile(dest, DISPATCH_TM), hn_p, hn_s)


def _moe_kernel(blk_e, blk_new, blk_on, blk_slot, blk_next, x_ref, wgu_hbm, bgu_ref, wdn_hbm, bdn_ref, o_ref,
                wgu_f, wdn_f, wgu_s, wdn_s, sems, *, layer):
    i = pl.program_id(0)

    def fetch(expert, slot):
        return (pltpu.make_async_copy(wgu_hbm.at[layer, expert], wgu_f.at[slot], sems.at[0, slot]),
                pltpu.make_async_copy(wdn_hbm.at[layer, expert], wdn_f.at[slot], sems.at[1, slot]))

    @pl.when(i == 0)
    def _():
        for copy in fetch(blk_e[0], 0):
            copy.start()

    @pl.when(blk_on[i] == BLK_UNUSED)
    def _():
        o_ref[...] = jnp.zeros_like(o_ref)

    @pl.when(blk_on[i] != BLK_UNUSED)
    def _():
        @pl.when(blk_new[i] == 1)
        def _():
            slot = blk_slot[i]
            for copy in fetch(blk_e[i], slot):
                copy.wait()

            @pl.when(blk_next[i] >= 0)
            def _():
                for copy in fetch(blk_next[i], 1 - slot):
                    copy.start()

            wgu_s[...] = wgu_f[slot].astype(BF16)
            wdn_s[...] = wdn_f[slot].astype(BF16)

        def expert_rows(n):
            x = jnp.concatenate([c.astype(BF16) for c in _load_token_tiles(x_ref, n)], axis=1)
            gu = jnp.dot(x, wgu_s[...], preferred_element_type=F32) + bgu_ref[...]
            g = jnp.minimum(gu[:, :D_FF], SWIGLU_LIMIT)
            u = jnp.clip(gu[:, D_FF:], -SWIGLU_LIMIT, SWIGLU_LIMIT)
            act = (u + 1.0) * (g * jax.nn.sigmoid(SWIGLU_ALPHA * g))
            _store_token_tiles(o_ref, jnp.dot(act.astype(BF16), wdn_s[...], preferred_element_type=F32)
                               + bdn_ref[...])

        @pl.when(blk_on[i] == BLK_FULL)
        def _():
            expert_rows(MOE_TM)

        @pl.when(blk_on[i] == BLK_HALF)
        def _():
            expert_rows(MOE_TM // 2)
            o_ref[_token_rows(MOE_TM // 2, MOE_TM // 2), :] = jnp.zeros(
                (MOE_TM // 2 * TOKEN_ROWS, V7X_LANES), F32)


def _moe_blocks(x_buf, blk, layer, w_gu, b_gu, w_dn, b_dn):
    n_pad = x_buf.shape[0] // TOKEN_ROWS
    n_blk = n_pad // MOE_TM
    tile = pl.BlockSpec((MOE_TM * TOKEN_ROWS, V7X_LANES), lambda i, e, *_: (i, 0))
    grid_spec = pltpu.PrefetchScalarGridSpec(
        num_scalar_prefetch=len(blk),
        grid=(n_blk,),
        in_specs=[
            tile,
            pl.BlockSpec(memory_space=pl.ANY),
            pl.BlockSpec((None, 1, 2 * D_FF), lambda i, e, *_: (e[i], 0, 0)),
            pl.BlockSpec(memory_space=pl.ANY),
            pl.BlockSpec((None, 1, D_MODEL), lambda i, e, *_: (e[i], 0, 0)),
        ],
        out_specs=tile,
        scratch_shapes=[pltpu.VMEM((2, D_MODEL, 2 * D_FF), F32), pltpu.VMEM((2, D_FF, D_MODEL), F32),
                        pltpu.VMEM((D_MODEL, 2 * D_FF), BF16), pltpu.VMEM((D_FF, D_MODEL), BF16),
                        pltpu.SemaphoreType.DMA((2, 2))],
    )
    return pl.pallas_call(
        functools.partial(_moe_kernel, layer=layer),
        grid_spec=grid_spec,
        out_shape=jax.ShapeDtypeStruct((n_pad * TOKEN_ROWS, V7X_LANES), F32),
        compiler_params=pltpu.CompilerParams(dimension_semantics=("arbitrary",),
                                             vmem_limit_bytes=VMEM_LIMIT),
        name="moe_experts",
    )(*blk, x_buf, w_gu, b_gu.reshape(N_EXPERTS, 1, 2 * D_FF), w_dn, b_dn.reshape(N_EXPERTS, 1, D_MODEL))


def _slot_kernel(start_ref, ei_ref, dest_ref):
    expert, rank = ei_ref[0:TOP_K, :], ei_ref[TOP_K:2 * TOP_K, :]
    start = jnp.zeros_like(expert)
    for e in range(N_EXPERTS):
        start = jnp.where(expert == e, start_ref[e], start)
    dest_ref[...] = start + rank


def _slots(ei, seg_start):
    n = ei.shape[1]
    grid_spec = pltpu.PrefetchScalarGridSpec(
        num_scalar_prefetch=1,
        grid=(n // SLOT_TN,),
        in_specs=[pl.BlockSpec((2 * TOP_K, SLOT_TN), lambda i, s: (0, i))],
        out_specs=pl.BlockSpec((TOP_K, SLOT_TN), lambda i, s: (0, i)),
    )
    return pl.pallas_call(
        _slot_kernel,
        grid_spec=grid_spec,
        out_shape=jax.ShapeDtypeStruct((TOP_K, n), jnp.int32),
        name="moe_slots",
    )(seg_start, ei)


PLAN_ROWS = ("blk_e", "blk_new", "blk_on", "blk_slot", "blk_next", "zero_blk", "zero_on", "seg_start")
PLAN_LANES = 256


def _plan_kernel(cnt_ref, plan_ref, *, n_blk, tail):
    e, lanes = N_EXPERTS, PLAN_LANES
    cnt = cnt_ref[...]
    expert = lax.broadcasted_iota(jnp.int32, (e, lanes), 0).astype(F32)
    lane = lax.broadcasted_iota(jnp.int32, (e, lanes), 1).astype(F32)
    lane_row = lane[0:1, :]

    def as_row(col):
        return jnp.sum(jnp.where(expert == lane, col, 0.0), axis=0, keepdims=True)

    nblk = jnp.floor((cnt + (MOE_TM - 1)) / MOE_TM)
    nblk_row = as_row(nblk)
    end = jnp.sum(jnp.where(lane <= expert, nblk_row, 0.0), axis=1, keepdims=True)
    start = end - nblk
    end_row = jnp.sum(jnp.where(expert <= lane, nblk, 0.0), axis=0, keepdims=True)
    total = end_row[:, e - 1:e]

    blk_e = jnp.minimum(jnp.sum(jnp.where(lane_row >= end, 1.0, 0.0), axis=0, keepdims=True), e - 1.0)
    mine = expert == blk_e

    def per_block(col):
        return jnp.sum(jnp.where(mine, col, 0.0), axis=0, keepdims=True)

    blk_start = per_block(start)
    tokens = jnp.clip(per_block(cnt) - (lane_row - blk_start) * MOE_TM, 0.0, float(MOE_TM))
    blk_on = jnp.where(lane_row >= total, float(BLK_UNUSED),
                       jnp.where(tokens <= MOE_TM // 2, float(BLK_HALF), float(BLK_FULL)))
    blk_new = jnp.where(lane_row == blk_start, 1.0, 0.0)

    used_row = jnp.where(as_row(cnt) > 0.0, 1.0, 0.0)
    runs_before = jnp.sum(jnp.where(lane < expert, used_row, 0.0), axis=1, keepdims=True)
    slot = runs_before - 2.0 * jnp.floor(runs_before / 2.0)
    after = jnp.min(jnp.where((lane > expert) & (used_row > 0.0), lane, float(lanes)), axis=1, keepdims=True)
    after = jnp.where(after >= e, -1.0, after)

    cnt_row = as_row(cnt)
    partly = jnp.where((cnt_row - MOE_TM * jnp.floor(cnt_row / MOE_TM) != 0.0) & (lane_row < e), 1.0, 0.0)
    tail_blk = total + (lane_row - e)
    in_tail = (lane_row >= e) & (lane_row < e + tail) & (tail_blk < n_blk)
    zero_blk = jnp.where(in_tail, tail_blk, (end_row - 1.0) * partly)
    zero_on = jnp.where(in_tail, 1.0, partly)
    seg_start = as_row(start) * MOE_TM

    rows = dict(blk_e=blk_e, blk_new=blk_new, blk_on=blk_on, blk_slot=per_block(slot), blk_next=per_block(after),
                zero_blk=zero_blk, zero_on=zero_on, seg_start=seg_start)
    out_row = lax.broadcasted_iota(jnp.int32, (len(PLAN_ROWS), lanes), 0)
    plan = jnp.zeros((len(PLAN_ROWS), lanes), F32)
    for r, name in enumerate(PLAN_ROWS):
        plan = jnp.where(out_row == r, rows[name], plan)
    plan_ref[...] = plan.astype(jnp.int32)


def _moe(hn_p, hn_s, ei_p, ei_s, counts, layer, w_gu, b_gu, w_dn, b_dn):
    n_asg = (ei_p.shape[1] + ei_s.shape[1]) * TOP_K
    n_pad = (n_asg + N_EXPERTS * (MOE_TM - 1) + MOE_TM - 1) // MOE_TM * MOE_TM
    n_blk = n_pad // MOE_TM
    tail = n_blk - n_asg // MOE_TM
    assert n_blk <= PLAN_LANES and N_EXPERTS + tail <= PLAN_LANES
    plan = pl.pallas_call(
        functools.partial(_plan_kernel, n_blk=n_blk, tail=tail),
        out_shape=jax.ShapeDtypeStruct((len(PLAN_ROWS), PLAN_LANES), jnp.int32),
        name="moe_plan",
    )(counts)
    row = {name: plan[r] for r, name in enumerate(PLAN_ROWS)}
    blk = tuple(row[name][:n_blk] for name in ("blk_e", "blk_new", "blk_on", "blk_slot", "blk_next"))
    n_zero = N_EXPERTS + tail

    dest = _slots(jnp.concatenate([ei_p, ei_s], axis=1), row["seg_start"][:N_EXPERTS])
    dest_p, dest_s = dest[:, :ei_p.shape[1]], dest[:, ei_p.shape[1]:]
    x_buf = _dispatch(dest, row["zero_blk"][:n_zero], row["zero_on"][:n_zero], hn_p, hn_s, n_pad)
    y_buf = _moe_blocks(x_buf, blk, layer, w_gu, b_gu, w_dn, b_dn)
    return dest_p, dest_s, y_buf


def _final_kernel(dest_cur, dest_nxt, x_ref, pr_ref, ybuf, mod_ref, g_ref, o_ref, rows, moe_s, sems):
    x = x_ref[...] + mod_ref[5:6, :] * _combine_experts(dest_cur, dest_nxt, pr_ref, ybuf, rows, moe_s, sems)
    o_ref[...] = _rmsnorm(x, g_ref[...])


def _final_norm(x, route, mod, g, seq):
    n = x.shape[0]
    tm = min(ROW_TILE, seq)
    row = pl.BlockSpec((tm, D_MODEL), lambda i: (i, 0))
    args, specs, scratch = _combine_operands(*route, n, tm)
    return pl.pallas_call(
        _final_kernel,
        grid=(n // tm,),
        in_specs=specs[:2] + [row] + specs[2:] + [_mod_spec(mod, seq, tm),
                                                  pl.BlockSpec((1, D_MODEL), lambda i: (0, 0))],
        out_specs=row,
        out_shape=jax.ShapeDtypeStruct((n, D_MODEL), F32),
        scratch_shapes=scratch,
        compiler_params=pltpu.CompilerParams(dimension_semantics=("arbitrary",), vmem_limit_bytes=VMEM_LIMIT),
        name="final_norm",
    )(*args[:2], x, *args[2:], mod, g.reshape(1, D_MODEL))


def _block_diag(w):
    eye = jnp.eye(HEADS, dtype=w.dtype)
    return (eye[:, None, :, None] * w[:, :, None, :]).reshape(W_GRP, W_GRP)


def _layer_weights(l, p):
    gate_w = jnp.concatenate([_block_diag(p["lru_wr"][l, 0]), _block_diag(p["lru_wi"][l, 0]),
                              _block_diag(p["lru_wr"][l, 1]), _block_diag(p["lru_wi"][l, 1])], axis=1)
    gate_b = jnp.concatenate([p["lru_br"][l, 0], p["lru_bi"][l, 0], p["lru_br"][l, 1], p["lru_bi"][l, 1]])
    head_of = jnp.arange(W_GRP) // HD
    row = lambda v: v.reshape(1, W_GRP)
    return dict(
        conv_a_w=p["conv_a_w"][l], conv_a_b=row(p["conv_a_b"][l]),
        norm_a_g=row(p["norm_a_g"][l]), norm_a_b=row(p["norm_a_b"][l]),
        conv_b_w=p["conv_b_w"][l], conv_b_b=row(p["conv_b_b"][l]),
        gate_w=gate_w.astype(BF16), gate_b=gate_b.reshape(1, 4 * W_GRP), lru_lam=p["lru_lam"][l],
        sgu_norm_g=row(p["sgu_norm_g"][l]), sgu_norm_b=row(p["sgu_norm_b"][l]),
        sgu_w=p["sgu_w"][l].astype(BF16), sgu_bias=jnp.repeat(p["sgu_b"][l].T, HD, axis=1),
        conv_d_w=p["conv_d_w"][l],
        m_head=((head_of[:, None] == head_of[None, :]).astype(F32) / HD).astype(BF16),
    )


def kernel(x_prompt, x_sample, state_rglru, c, c_ctx, w_ada, b_ada, norm1_g, norm2_g, w_in, conv_a_w,
           conv_a_b, norm_a_g, norm_a_b, conv_b_w, conv_b_b, lru_wr, lru_br, lru_wi, lru_bi, lru_lam,
           sgu_norm_g, sgu_norm_b, sgu_w, sgu_b, conv_d_w, w_out, w_router, b_router, w_gu, b_gu, w_dn,
           b_dn, final_g):
    p = dict(conv_a_w=conv_a_w, conv_a_b=conv_a_b, norm_a_g=norm_a_g, norm_a_b=norm_a_b,
             conv_b_w=conv_b_w, conv_b_b=conv_b_b, lru_wr=lru_wr, lru_br=lru_br, lru_wi=lru_wi,
             lru_bi=lru_bi, lru_lam=lru_lam, sgu_norm_g=sgu_norm_g, sgu_norm_b=sgu_norm_b,
             sgu_w=sgu_w, sgu_b=sgu_b, conv_d_w=conv_d_w)
    bp, tp, _ = x_prompt.shape
    bs, ts, _ = x_sample.shape
    n_p, n_s = bp * tp, bs * ts

    cond_rows = jnp.zeros((COND_ROWS, D_MODEL), F32).at[0].set(c_ctx).at[1:1 + bs].set(c)
    mod = _ada_mod(cond_rows, w_ada, b_ada).reshape(DEPTH, COND_ROWS, 6, D_MODEL)

    xp = x_prompt.reshape(n_p, D_MODEL)
    xs = x_sample.reshape(n_s, D_MODEL)
    h0_ctx = jnp.zeros((bp, 2, W_GRP), F32)
    route_p = route_s = mod_p_prev = mod_s_prev = None
    no_tokens_seen = jnp.zeros((N_EXPERTS, 1), F32)
    states = []
    for l in range(DEPTH):
        lw = _layer_weights(l, p)
        mod_p, mod_s = mod[l, 0:1], mod[l, 1:1 + bs]
        w_in_l = w_in[l].astype(BF16)
        w_out_l = w_out[l].astype(BF16)
        xp, proj_p = _in_proj(xp, route_p, mod_p_prev, mod_p, norm1_g[l], w_in_l, tp)
        xs, proj_s = _in_proj(xs, route_s, mod_s_prev, mod_s, norm1_g[l], w_in_l, ts)
        y_p, st = _mixers(proj_p, h0_ctx, lw, bp, tp, False)
        y_s, _ = _mixers(proj_s, state_rglru[:, l], lw, bs, ts, True)
        states.append(st)
        xp, hn_p, ei_p, pr_p, seen = _out_proj(y_p, xp, mod_p, norm2_g[l], w_out_l, w_router[l],
                                               b_router[l], no_tokens_seen, tp)
        xs, hn_s, ei_s, pr_s, counts = _out_proj(y_s, xs, mod_s, norm2_g[l], w_out_l, w_router[l],
                                                 b_router[l], seen, ts)
        dest_p, dest_s, y_buf = _moe(hn_p, hn_s, ei_p, ei_s, counts, l, w_gu, b_gu[l], w_dn, b_dn[l])
        route_p, route_s = (dest_p, pr_p, y_buf), (dest_s, pr_s, y_buf)
        mod_p_prev, mod_s_prev = mod_p, mod_s
    y_prompt = _final_norm(xp, route_p, mod_p_prev, final_g, tp).reshape(bp, tp, D_MODEL)
    y_sample = _final_norm(xs, route_s, mod_s_prev, final_g, ts).reshape(bs, ts, D_MODEL)
    return y_prompt, y_sample, jnp.stack(states, axis=1)
```

```python
import functools

import jax
import jax.numpy as jnp
from jax import lax
from jax.experimental import pallas as pl
from jax.experimental.pallas import tpu as pltpu

F32 = jnp.float32
BF16 = jnp.bfloat16

D_MODEL = 1024
DEPTH = 2
GRID_W = 64
N_MIXERS = 4
W_GRP = D_MODEL // N_MIXERS
HEADS = 4
HD = W_GRP // HEADS
N_IN = 9 * W_GRP
CONV_A = 31
CONV_B = 4
CONV_D = 3
CHUNK = 128
LRU_C = 8.0
N_EXPERTS = 32
TOP_K = 4
D_FF = D_MODEL
SWIGLU_LIMIT = 7.0
SWIGLU_ALPHA = 1.702
EPS = 1e-6

V7X_SUBLANES = 8
V7X_LANES = 128
TOKEN_ROWS = D_MODEL // V7X_LANES
DMA_PRIORITIES = 2
V7X_VMEM_BYTES = 64 * 1024 * 1024
VMEM_LIMIT = V7X_VMEM_BYTES * 7 // 8

COND_ROWS = 16
ADA_TN = 1536
ROW_TILE = 512
CONV_ROWS = GRID_W
CONV_PAD = 16
NORM_ROWS = 256
MIX_ROWS = 1024
MOE_TM = 512
BLK_UNUSED, BLK_HALF, BLK_FULL = 0, 1, 2
DISPATCH_TM = 512
SLOT_TN = 2048
COMBINE_ROWS = 32


def _rmsnorm(x, g):
    return x * lax.rsqrt(jnp.mean(x * x, axis=-1, keepdims=True) + EPS) * g


def _group_mean(x, m_ref):
    hi = x.astype(BF16)
    lo = (x - hi.astype(F32)).astype(BF16)
    m = m_ref[...]
    return (jnp.dot(hi, m, preferred_element_type=F32) + jnp.dot(lo, m, preferred_element_type=F32))


def _layernorm(x, g, b, m_ref):
    xc = x - _group_mean(x, m_ref)
    var = _group_mean(xc * xc, m_ref)
    return xc * lax.rsqrt(var + EPS) * g + b


def _ada_kernel(c_ref, w_ref, b_ref, o_ref):
    c = c_ref[...]
    cond = (c * jax.nn.sigmoid(c)).astype(BF16)
    o_ref[...] = jnp.dot(cond, w_ref[...].astype(BF16), preferred_element_type=F32) + b_ref[...]


def _ada_mod(cond_rows, w_ada, b_ada):
    n_col = w_ada.shape[-1]
    return pl.pallas_call(
        _ada_kernel,
        grid=(DEPTH, n_col // ADA_TN),
        in_specs=[
            pl.BlockSpec((COND_ROWS, D_MODEL), lambda l, j: (0, 0)),
            pl.BlockSpec((None, D_MODEL, ADA_TN), lambda l, j: (l, 0, j)),
            pl.BlockSpec((None, 1, ADA_TN), lambda l, j: (l, 0, j)),
        ],
        out_specs=pl.BlockSpec((None, COND_ROWS, ADA_TN), lambda l, j: (l, 0, j)),
        out_shape=jax.ShapeDtypeStruct((DEPTH, COND_ROWS, n_col), F32),
        name="ada_mod",
    )(cond_rows, w_ada, b_ada.reshape(DEPTH, 1, n_col))


def _token_rows(t, count=1):
    return pl.ds(pl.multiple_of(t * TOKEN_ROWS, TOKEN_ROWS), count * TOKEN_ROWS)


def _store_token_tiles(ref, x):
    n = x.shape[0]
    for c in range(TOKEN_ROWS):
        ref[pl.ds(c, n, stride=TOKEN_ROWS), :] = x[:, c * V7X_LANES:(c + 1) * V7X_LANES]


def _load_token_tiles(ref, n):
    return [ref[pl.ds(c, n, stride=TOKEN_ROWS), :] for c in range(TOKEN_ROWS)]


def _combine_experts(dest_cur, dest_nxt, pr_ref, ybuf, rows, moe_s, sems):
    i = pl.program_id(0)
    n_steps = pl.num_programs(0)
    tm = rows.shape[2] // TOKEN_ROWS
    slot = i % 2

    def row_copy(dref, j, k, slot_):
        return pltpu.make_async_copy(ybuf.at[_token_rows(dref[0, k * tm + j])],
                                     rows.at[slot_, k, _token_rows(j)], sems.at[slot_])

    def issue(dref, slot_):
        def body(j, carry):
            for k in range(TOP_K):
                row_copy(dref, j, k, slot_).start(priority=k % DMA_PRIORITIES)
            return carry

        lax.fori_loop(0, tm, body, 0, unroll=4)

    @pl.when(i == 0)
    def _():
        issue(dest_cur, 0)

    for nxt in (0, 1):
        @pl.when((i + 1 < n_steps) & (slot != nxt))
        def _():
            issue(dest_nxt, nxt)

    for k in range(TOP_K):
        pltpu.make_async_copy(ybuf.at[_token_rows(0, tm)], rows.at[slot, k], sems.at[slot]).wait()
    pad = jnp.zeros((V7X_LANES - pr_ref.shape[0], tm), F32)
    probs = jnp.concatenate([pr_ref[...], pad], axis=0).T
    for t0 in range(0, tm, COMBINE_ROWS):
        weight = [probs[t0:t0 + COMBINE_ROWS, k:k + 1] for k in range(TOP_K)]
        for c in range(TOKEN_ROWS):
            chunk = pl.ds(t0 * TOKEN_ROWS + c, COMBINE_ROWS, stride=TOKEN_ROWS)
            acc = weight[0] * rows.at[slot, 0][chunk, :]
            for k in range(1, TOP_K):
                acc = acc + weight[k] * rows.at[slot, k][chunk, :]
            moe_s[t0:t0 + COMBINE_ROWS, c * V7X_LANES:(c + 1) * V7X_LANES] = acc
    return moe_s[...]


def _inproj_kernel(*refs, has_res):
    if has_res:
        (dest_cur, dest_nxt, x_ref, pr_ref, ybuf, modp_ref, mod_ref, g_ref, w_ref,
         xo_ref, p_ref, rows, moe_s, sems) = refs
        x = x_ref[...] + modp_ref[5:6, :] * _combine_experts(dest_cur, dest_nxt, pr_ref, ybuf, rows, moe_s, sems)
        xo_ref[...] = x
    else:
        x_ref, mod_ref, g_ref, w_ref, p_ref = refs
        x = x_ref[...]
    hn = _rmsnorm(x, g_ref[...]) * (1.0 + mod_ref[1:2, :]) + mod_ref[0:1, :]
    p_ref[...] = jnp.dot(hn.astype(BF16), w_ref[...], preferred_element_type=F32)


def _row_tile(mod, seq):
    return ROW_TILE if mod.shape[0] == 1 else min(ROW_TILE, seq)


def _mod_spec(mod, seq, tm):
    if mod.shape[0] == 1:
        return pl.BlockSpec((None, 6, D_MODEL), lambda i: (0, 0, 0))
    return pl.BlockSpec((None, 6, D_MODEL), lambda i: ((i * tm) // seq, 0, 0))


def _per_tile(dest, tm):
    n_steps = dest.shape[1] // tm
    return dest.reshape(TOP_K, n_steps, tm).transpose(1, 0, 2).reshape(n_steps, 1, TOP_K * tm)


def _combine_operands(dest, probs, y_buf, n, tm):
    n_steps = n // tm
    dest3 = _per_tile(dest, tm)
    smem = functools.partial(pl.BlockSpec, (None, 1, tm * TOP_K), memory_space=pltpu.SMEM)
    args = [dest3, dest3, probs, y_buf]
    specs = [smem(lambda i: (i, 0, 0)),
             smem(lambda i: (jnp.minimum(i + 1, n_steps - 1), 0, 0)),
             pl.BlockSpec((2 * TOP_K, tm), lambda i: (0, i)),
             pl.BlockSpec(memory_space=pl.ANY)]
    scratch = [pltpu.VMEM((2, TOP_K, tm * TOKEN_ROWS, V7X_LANES), F32), pltpu.VMEM((tm, D_MODEL), F32),
               pltpu.SemaphoreType.DMA((2,))]
    return args, specs, scratch


def _in_proj(x, route, mod_prev, mod, g, w_bf16, seq):
    n = x.shape[0]
    tm = _row_tile(mod, seq)
    row = pl.BlockSpec((tm, D_MODEL), lambda i: (i, 0))
    has_res = route is not None
    args, specs, scratch = [], [], []
    if has_res:
        args, specs, scratch = _combine_operands(*route, n, tm)
        args = args[:2] + [x] + args[2:] + [mod_prev]
        specs = specs[:2] + [row] + specs[2:] + [_mod_spec(mod_prev, seq, tm)]
    else:
        args, specs = [x], [row]
    args += [mod, g.reshape(1, D_MODEL), w_bf16]
    specs += [_mod_spec(mod, seq, tm),
              pl.BlockSpec((1, D_MODEL), lambda i: (0, 0)),
              pl.BlockSpec((D_MODEL, N_IN), lambda i: (0, 0))]
    proj_shape = jax.ShapeDtypeStruct((n, N_IN), F32)
    proj_spec = pl.BlockSpec((tm, N_IN), lambda i: (i, 0))
    if has_res:
        out_shape = (jax.ShapeDtypeStruct((n, D_MODEL), F32), proj_shape)
        out_specs = (row, proj_spec)
    else:
        out_shape, out_specs = proj_shape, proj_spec
    out = pl.pallas_call(
        functools.partial(_inproj_kernel, has_res=has_res),
        grid=(n // tm,),
        in_specs=specs,
        out_specs=out_specs,
        out_shape=out_shape,
        scratch_shapes=scratch,
        compiler_params=pltpu.CompilerParams(dimension_semantics=("arbitrary",), vmem_limit_bytes=VMEM_LIMIT),
        name="in_proj",
    )(*args)
    return out if has_res else (x, out)


def _group_scan(a, b, reverse):
    shape = a.shape
    grouped = (shape[0] // V7X_SUBLANES, V7X_SUBLANES, shape[1])
    a, b = a.reshape(grouped), b.reshape(grouped)
    ri = lax.broadcasted_iota(jnp.int32, grouped, 1)
    for d in (1, 2, 4):
        shift = V7X_SUBLANES - d if reverse else d
        keep = ri < V7X_SUBLANES - d if reverse else ri >= d
        ra, rb = pltpu.roll(a, shift, 1), pltpu.roll(b, shift, 1)
        b = a * jnp.where(keep, rb, 0.0) + b
        a = a * jnp.where(keep, ra, 1.0)
    return a.reshape(shape), b.reshape(shape)


def _mixer_kernel(s0, s1, s2, caw, cab, nag, nab, cbw, cbb, wg, bg, lam, h0, sng, snb, sw, sbias, cdw,
                  m_head, y_ref, st_ref, *scratch, seq, on_grid):
    def one_sequence(b, carry):
        _mix_sequence(s0.at[b], s1.at[b], s2.at[b], caw, cab, nag, nab, cbw, cbb, wg, bg, lam, h0.at[b],
                      sng, snb, sw, sbias, cdw, m_head, y_ref.at[b], st_ref.at[b], *scratch,
                      seq=seq, on_grid=on_grid)
        return carry

    lax.fori_loop(0, s0.shape[0], one_sequence, 0)


def _mix_sequence(s0, s1, s2, caw, cab, nag, nab, cbw, cbb, wg, bg, lam, h0, sng, snb, sw, sbias, cdw,
                  m_head, y_ref, st_ref, pad_s, af_s, bf_s, ab_s, bb_s, *, seq, on_grid):
    mixer = pl.program_id(1)
    n_conv = seq // CONV_ROWS
    win_rows = CONV_ROWS + 2 * CONV_PAD

    def rows_at(c, size):
        return pl.ds(pl.multiple_of(c * size, size), size)

    def fill_padded(fn):
        zeros = jnp.zeros((CONV_PAD, W_GRP), F32)
        pad_s[0:CONV_PAD, :] = zeros
        pad_s[CONV_PAD + seq:2 * CONV_PAD + seq, :] = zeros

        def body(c, carry):
            dst = pl.ds(pl.multiple_of(c * CONV_ROWS + CONV_PAD, V7X_SUBLANES), CONV_ROWS)
            pad_s[dst, :] = fn(rows_at(c, CONV_ROWS))
            return carry

        lax.fori_loop(0, n_conv, body, 0)

    def conv_window(win, w_ref, taps, pad_l):
        acc = jnp.zeros((CONV_ROWS, W_GRP), F32)
        for mis in range(V7X_SUBLANES):
            starts = [(k, CONV_PAD - pad_l + k) for k in range(taps)
                      if (CONV_PAD - pad_l + k) % V7X_SUBLANES == mis]
            if not starts:
                continue
            shifted = pltpu.roll(win, win_rows - mis, 0) if mis else win
            for k, start in starts:
                acc = acc + w_ref[k:k + 1, :] * shifted[start - mis:start - mis + CONV_ROWS, :]
        return acc

    def conv_rows(c, w_ref, taps, pad_l):
        win = pad_s[pl.ds(pl.multiple_of(c * CONV_ROWS, CONV_ROWS), win_rows), :]
        return conv_window(win, w_ref, taps, pad_l)

    @pl.when(mixer == 0)
    def _conformer():
        def glu(rows):
            return s0[rows, :] * jax.nn.sigmoid(s1[rows, :])

        if not on_grid:
            fill_padded(glu)

        def body(c, carry):
            if on_grid:
                edge = jnp.zeros((CONV_PAD, W_GRP), F32)
                win = jnp.concatenate([edge, glu(rows_at(c, CONV_ROWS)), edge], axis=0)
                z = conv_window(win, caw, CONV_A, CONV_A // 2)
            else:
                z = conv_rows(c, caw, CONV_A, CONV_A // 2)
            af_s[rows_at(c, CONV_ROWS), :] = z + cab[...]
            return carry

        lax.fori_loop(0, n_conv, body, 0)

        def norm(c, carry):
            rows = rows_at(c, NORM_ROWS)
            z = _layernorm(af_s[rows, :], nag[...], nab[...], m_head)
            y_ref[rows, :] = (z * jax.nn.sigmoid(z)).astype(BF16)
            return carry

        lax.fori_loop(0, seq // NORM_ROWS, norm, 0, unroll=4)

    @pl.when(mixer == 1)
    def _rglru():
        fill_padded(lambda rows: s0[rows, :])
        lam_v = lam[...]
        softplus_neg = jnp.maximum(-lam_v, 0.0) + jnp.log1p(jnp.exp(-jnp.abs(lam_v)))

        def gates(c, carry):
            rows = rows_at(c, CONV_ROWS)
            xc = conv_rows(c, cbw, CONV_B, 2) + cbb[...]
            gt = jax.nn.sigmoid(jnp.dot(xc.astype(BF16), wg[...], preferred_element_type=F32) + bg[...])
            for d, (a_s, b_s) in enumerate(((af_s, bf_s), (ab_s, bb_s))):
                r_gate = gt[:, (2 * d) * W_GRP:(2 * d + 1) * W_GRP]
                i_gate = gt[:, (2 * d + 1) * W_GRP:(2 * d + 2) * W_GRP]
                log_a = -LRU_C * r_gate * softplus_neg[d:d + 1, :]
                a = jnp.exp(log_a)
                b = jnp.sqrt(jnp.maximum(-jnp.tanh(log_a) * (a * a + 1.0), 0.0)) * (i_gate * xc)
                a, b = _group_scan(a, b, reverse=(d == 1))
                a_s[rows, :] = a
                b_s[rows, :] = b
            return carry

        lax.fori_loop(0, n_conv, gates, 0, unroll=4)

        n_grp = seq // V7X_SUBLANES

        def chain(g, carry):
            cf, cb = carry
            rf = rows_at(g, V7X_SUBLANES)
            rb = rows_at(n_grp - 1 - g, V7X_SUBLANES)
            hf = af_s[rf, :] * cf + bf_s[rf, :]
            bf_s[rf, :] = hf
            hb = ab_s[rb, :] * cb + bb_s[rb, :]
            bb_s[rb, :] = hb
            cf = jnp.broadcast_to(hf[V7X_SUBLANES - 1:V7X_SUBLANES, :], (V7X_SUBLANES, W_GRP))
            cb = jnp.broadcast_to(hb[0:1, :], (V7X_SUBLANES, W_GRP))
            return cf, cb

        init = (jnp.broadcast_to(h0[0:1, :], (V7X_SUBLANES, W_GRP)),
                jnp.broadcast_to(h0[1:2, :], (V7X_SUBLANES, W_GRP)))
        cf, cb = lax.fori_loop(0, n_grp, chain, init, unroll=8)
        st_ref[0:1, :] = cf[0:1, :]
        st_ref[1:2, :] = cb[0:1, :]

        def out(c, carry):
            rows = rows_at(c, CONV_ROWS)
            y_ref[rows, :] = (jax.nn.gelu(s1[rows, :]) * (bf_s[rows, :] + bb_s[rows, :])).astype(BF16)
            return carry

        lax.fori_loop(0, n_conv, out, 0)

    @pl.when(mixer == 2)
    def _sgu():
        lane = lax.broadcasted_iota(jnp.int32, (CHUNK, W_GRP), 1)

        def body(n, carry):
            rows = rows_at(n, CHUNK)
            v = s1[rows, :]
            vc = v - jnp.mean(v, axis=-1, keepdims=True)
            var = jnp.mean(vc * vc, axis=-1, keepdims=True)
            v = (vc * lax.rsqrt(var + EPS) * sng[...] + snb[...]).astype(BF16)
            s = sbias[...]
            for h in range(HEADS):
                sh = jnp.dot(sw[h], v, preferred_element_type=F32)
                s = s + jnp.where((lane >= h * HD) & (lane < (h + 1) * HD), sh, 0.0)
            y_ref[rows, :] = (s0[rows, :] * s).astype(BF16)
            return carry

        lax.fori_loop(0, seq // CHUNK, body, 0, unroll=4)

    @pl.when(mixer == 3)
    def _gated_conv():
        fill_padded(lambda rows: s1[rows, :] * s2[rows, :])

        def body(c, carry):
            rows = rows_at(c, CONV_ROWS)
            y_ref[rows, :] = (s0[rows, :] * conv_rows(c, cdw, CONV_D, CONV_D // 2)).astype(BF16)
            return carry

        lax.fori_loop(0, n_conv, body, 0)


def _mixers(proj, h0, lw, batch, seq, on_grid):
    proj3 = proj.reshape(batch, seq, N_IN)
    seqs = max(1, MIX_ROWS // seq)
    slab = (seqs, seq, W_GRP)
    state = pl.BlockSpec((seqs, 2, W_GRP), lambda b, m: (b, 0, 0))

    def const(shape):
        return pl.BlockSpec(shape, lambda b, m: (0,) * len(shape))

    in_specs = [
        pl.BlockSpec(slab, lambda b, m: (b, 0, 2 * m)),
        pl.BlockSpec(slab, lambda b, m: (b, 0, 2 * m + 1)),
        pl.BlockSpec(slab, lambda b, m: (b, 0, 8)),
        const((CONV_A, W_GRP)), const((1, W_GRP)), const((1, W_GRP)), const((1, W_GRP)),
        const((CONV_B, W_GRP)), const((1, W_GRP)),
        const((W_GRP, 4 * W_GRP)), const((1, 4 * W_GRP)), const((2, W_GRP)),
        state,
        const((1, W_GRP)), const((1, W_GRP)), const((HEADS, CHUNK, CHUNK)), const((CHUNK, W_GRP)),
        const((CONV_D, W_GRP)), const((W_GRP, W_GRP)),
    ]
    y, st = pl.pallas_call(
        functools.partial(_mixer_kernel, seq=seq, on_grid=on_grid),
        grid=(batch // seqs, N_MIXERS),
        in_specs=in_specs,
        out_specs=(pl.BlockSpec(slab, lambda b, m: (b, 0, m)), state),
        out_shape=(jax.ShapeDtypeStruct((batch, seq, D_MODEL), BF16),
                   jax.ShapeDtypeStruct((batch, 2, W_GRP), F32)),
        scratch_shapes=[pltpu.VMEM((seq + 2 * CONV_PAD, W_GRP), F32)] + [pltpu.VMEM((seq, W_GRP), F32)] * 4,
        compiler_params=pltpu.CompilerParams(dimension_semantics=("arbitrary", "arbitrary"),
                                             vmem_limit_bytes=VMEM_LIMIT),
        name="mixers",
    )(proj3, proj3, proj3, lw["conv_a_w"], lw["conv_a_b"], lw["norm_a_g"], lw["norm_a_b"],
      lw["conv_b_w"], lw["conv_b_b"], lw["gate_w"], lw["gate_b"], lw["lru_lam"], h0,
      lw["sgu_norm_g"], lw["sgu_norm_b"], lw["sgu_w"], lw["sgu_bias"], lw["conv_d_w"],
      lw["m_head"])
    return y.reshape(batch * seq, D_MODEL), st


def _outproj_kernel(y_ref, x_ref, mod_ref, g_ref, wo_ref, wr_ref, br_ref, tri_ref, cin_ref,
                    xm_ref, hn_ref, ei_ref, pr_ref, cnt_ref, seen_s):
    tm = x_ref.shape[0]
    e = N_EXPERTS

    @pl.when(pl.program_id(0) == 0)
    def _():
        seen_s[...] = cin_ref[...]

    y = jnp.dot(y_ref[...], wo_ref[...], preferred_element_type=F32)
    x = x_ref[...] + mod_ref[2:3, :] * y
    xm_ref[...] = x
    hn = _rmsnorm(x, g_ref[...]) * (1.0 + mod_ref[4:5, :]) + mod_ref[3:4, :]
    _store_token_tiles(hn_ref, hn)

    h1 = hn.astype(BF16)
    r1 = hn - h1.astype(F32)
    h2 = r1.astype(BF16)
    h3 = (r1 - h2.astype(F32)).astype(BF16)
    w_t = wr_ref[...]
    contract_features = (((1,), (1,)), ((), ()))
    p1 = lax.dot_general(w_t, h1, contract_features, preferred_element_type=F32)
    p2 = lax.dot_general(w_t, h2, contract_features, preferred_element_type=F32)
    p3 = lax.dot_general(w_t, h3, contract_features, preferred_element_type=F32)
    logits = (p1[2 * e:3 * e] + p2[e:2 * e] + p3[0:e] + p1[e:2 * e] + p2[0:e] + p1[0:e]) + br_ref[...]

    expert = lax.broadcasted_iota(jnp.int32, (e, tm), 0)
    beats = jnp.zeros((e, tm), F32)
    for other in range(e):
        lo = logits[other:other + 1, :]
        ahead = (lo > logits) | ((lo == logits) & (expert > other))
        beats = beats + jnp.where(ahead, 1.0, 0.0)

    chosen = jnp.where(beats < float(TOP_K), 1.0, 0.0)
    earlier = jnp.dot(chosen.astype(BF16), tri_ref[...], preferred_element_type=F32) + seen_s[...]
    seen_s[...] = earlier[:, tm - 1:tm] + chosen[:, tm - 1:tm]
    cnt_ref[...] = seen_s[...]

    def pick(k, values):
        return jnp.sum(jnp.where(beats == float(k), values, 0.0), axis=0, keepdims=True)

    expert_f = expert.astype(F32)
    vals = [pick(k, logits) for k in range(TOP_K)]
    exps = [jnp.exp(v - vals[0]) for v in vals]
    den = exps[0] + exps[1] + exps[2] + exps[3]
    out_row = lax.broadcasted_iota(jnp.int32, (2 * TOP_K, tm), 0)
    ei = jnp.zeros((2 * TOP_K, tm), F32)
    pr = jnp.zeros((2 * TOP_K, tm), F32)
    for k in range(TOP_K):
        ei = jnp.where(out_row == k, pick(k, expert_f), ei)
        ei = jnp.where(out_row == TOP_K + k, pick(k, earlier), ei)
        pr = jnp.where(out_row == k, exps[k] / den, pr)
    ei_ref[...] = ei.astype(jnp.int32)
    pr_ref[...] = pr


def _out_proj(y, x, mod, g, wo_bf16, w_router, b_router, seen, seq):
    n = x.shape[0]
    tm = _row_tile(mod, seq)
    row = pl.BlockSpec((tm, D_MODEL), lambda i: (i, 0))
    route = pl.BlockSpec((2 * TOP_K, tm), lambda i: (0, i))
    cnt = pl.BlockSpec((N_EXPERTS, 1), lambda i: (0, 0))
    before = jnp.tri(tm, k=-1, dtype=BF16).T
    w1 = w_router.astype(BF16)
    w2 = (w_router - w1.astype(F32)).astype(BF16)
    w3 = (w_router - w1.astype(F32) - w2.astype(F32)).astype(BF16)
    w_t = jnp.concatenate([w1.T, w2.T, w3.T, jnp.zeros_like(w1.T)], axis=0)
    return pl.pallas_call(
        _outproj_kernel,
        grid=(n // tm,),
        in_specs=[row, row, _mod_spec(mod, seq, tm),
                  pl.BlockSpec((1, D_MODEL), lambda i: (0, 0)),
                  pl.BlockSpec((D_MODEL, D_MODEL), lambda i: (0, 0)),
                  pl.BlockSpec((4 * N_EXPERTS, D_MODEL), lambda i: (0, 0)),
                  cnt, pl.BlockSpec((tm, tm), lambda i: (0, 0)), cnt],
        out_specs=(row, pl.BlockSpec((tm * TOKEN_ROWS, V7X_LANES), lambda i: (i, 0)), route, route, cnt),
        out_shape=(jax.ShapeDtypeStruct((n, D_MODEL), F32),
                   jax.ShapeDtypeStruct((n * TOKEN_ROWS, V7X_LANES), F32),
                   jax.ShapeDtypeStruct((2 * TOP_K, n), jnp.int32),
                   jax.ShapeDtypeStruct((2 * TOP_K, n), F32),
                   jax.ShapeDtypeStruct((N_EXPERTS, 1), F32)),
        scratch_shapes=[pltpu.VMEM((N_EXPERTS, 1), F32)],
        compiler_params=pltpu.CompilerParams(dimension_semantics=("arbitrary",)),
        name="out_proj",
    )(y, x, mod, g.reshape(1, D_MODEL), wo_bf16, w_t, b_router.reshape(N_EXPERTS, 1), before, seen)


def _dispatch_kernel(zero_blk, zero_on, dest_ref, hn_p, hn_s, xbuf, zeros_s, sem, zsem, *, tiles_p):
    i = pl.program_id(0)

    @pl.when(i == 0)
    def _():
        zeros_s[...] = jnp.zeros_like(zeros_s)

        def fill(j):
            return pltpu.make_async_copy(zeros_s, xbuf.at[_token_rows(zero_blk[j] * MOE_TM, MOE_TM)], zsem)

        for j in range(zero_blk.shape[0]):
            @pl.when(zero_on[j] == 1)
            def _():
                fill(j).start()

        for j in range(zero_blk.shape[0]):
            @pl.when(zero_on[j] == 1)
            def _():
                fill(j).wait()

    def issue(src):
        def body(j, carry):
            row = src.at[_token_rows(j)]
            for k in range(TOP_K):
                pltpu.make_async_copy(row, xbuf.at[_token_rows(dest_ref[0, k * DISPATCH_TM + j])],
                                      sem).start(priority=k % DMA_PRIORITIES)
            return carry

        lax.fori_loop(0, DISPATCH_TM, body, 0, unroll=4)

    @pl.when(i < tiles_p)
    def _():
        issue(hn_p)

    @pl.when(i >= tiles_p)
    def _():
        issue(hn_s)

    for _ in range(TOP_K):
        pltpu.make_async_copy(hn_s, xbuf.at[_token_rows(0, DISPATCH_TM)], sem).wait()


def _dispatch(dest, zero_blk, zero_on, hn_p, hn_s, n_pad):
    n_p, n_s = hn_p.shape[0] // TOKEN_ROWS, hn_s.shape[0] // TOKEN_ROWS
    tiles_p = n_p // DISPATCH_TM
    n_steps = (n_p + n_s) // DISPATCH_TM
    tile = (DISPATCH_TM * TOKEN_ROWS, V7X_LANES)
    grid_spec = pltpu.PrefetchScalarGridSpec(
        num_scalar_prefetch=2,
        grid=(n_steps,),
        in_specs=[pl.BlockSpec((None, 1, DISPATCH_TM * TOP_K), lambda i, zb, zo: (i, 0, 0),
                               memory_space=pltpu.SMEM),
                  pl.BlockSpec(tile, lambda i, zb, zo: (jnp.minimum(i, tiles_p - 1), 0)),
                  pl.BlockSpec(tile, lambda i, zb, zo: (jnp.maximum(i - tiles_p, 0), 0))],
        out_specs=pl.BlockSpec(memory_space=pl.ANY),
        scratch_shapes=[pltpu.VMEM((MOE_TM * TOKEN_ROWS, V7X_LANES), F32), pltpu.SemaphoreType.DMA(()),
                        pltpu.SemaphoreType.DMA(())],
    )
    return pl.pallas_call(
        functools.partial(_dispatch_kernel, tiles_p=tiles_p),
        grid_spec=grid_spec,
        out_shape=jax.ShapeDtypeStruct((n_pad * TOKEN_ROWS, V7X_LANES), F32),
        compiler_params=pltpu.CompilerParams(dimension_semantics=("arbitrary",)),
        name="moe_dispatch",
    )(zero_blk, zero_on, _per_tile(dest, DISPATCH_TM), hn_p, hn_s)


def _moe_kernel(blk_e, blk_new, blk_on, blk_slot, blk_next, x_ref, wgu_hbm, bgu_ref, wdn_hbm, bdn_ref, o_ref,
                wgu_f, wdn_f, wgu_s, wdn_s, sems, *, layer):
    i = pl.program_id(0)

    def fetch(expert, slot):
        return (pltpu.make_async_copy(wgu_hbm.at[layer, expert], wgu_f.at[slot], sems.at[0, slot]),
                pltpu.make_async_copy(wdn_hbm.at[layer, expert], wdn_f.at[slot], sems.at[1, slot]))

    @pl.when(i == 0)
    def _():
        for copy in fetch(blk_e[0], 0):
            copy.start()

    @pl.when(blk_on[i] == BLK_UNUSED)
    def _():
        o_ref[...] = jnp.zeros_like(o_ref)

    @pl.when(blk_on[i] != BLK_UNUSED)
    def _():
        @pl.when(blk_new[i] == 1)
        def _():
            slot = blk_slot[i]
            for copy in fetch(blk_e[i], slot):
                copy.wait()

            @pl.when(blk_next[i] >= 0)
            def _():
                for copy in fetch(blk_next[i], 1 - slot):
                    copy.start()

            wgu_s[...] = wgu_f[slot].astype(BF16)
            wdn_s[...] = wdn_f[slot].astype(BF16)

        def expert_rows(n):
            x = jnp.concatenate([c.astype(BF16) for c in _load_token_tiles(x_ref, n)], axis=1)
            gu = jnp.dot(x, wgu_s[...], preferred_element_type=F32) + bgu_ref[...]
            g = jnp.minimum(gu[:, :D_FF], SWIGLU_LIMIT)
            u = jnp.clip(gu[:, D_FF:], -SWIGLU_LIMIT, SWIGLU_LIMIT)
            act = (u + 1.0) * (g * jax.nn.sigmoid(SWIGLU_ALPHA * g))
            _store_token_tiles(o_ref, jnp.dot(act.astype(BF16), wdn_s[...], preferred_element_type=F32)
                               + bdn_ref[...])

        @pl.when(blk_on[i] == BLK_FULL)
        def _():
            expert_rows(MOE_TM)

        @pl.when(blk_on[i] == BLK_HALF)
        def _():
            expert_rows(MOE_TM // 2)
            o_ref[_token_rows(MOE_TM // 2, MOE_TM // 2), :] = jnp.zeros(
                (MOE_TM // 2 * TOKEN_ROWS, V7X_LANES), F32)


def _moe_blocks(x_buf, blk, layer, w_gu, b_gu, w_dn, b_dn):
    n_pad = x_buf.shape[0] // TOKEN_ROWS
    n_blk = n_pad // MOE_TM
    tile = pl.BlockSpec((MOE_TM * TOKEN_ROWS, V7X_LANES), lambda i, e, *_: (i, 0))
    grid_spec = pltpu.PrefetchScalarGridSpec(
        num_scalar_prefetch=len(blk),
        grid=(n_blk,),
        in_specs=[
            tile,
            pl.BlockSpec(memory_space=pl.ANY),
            pl.BlockSpec((None, 1, 2 * D_FF), lambda i, e, *_: (e[i], 0, 0)),
            pl.BlockSpec(memory_space=pl.ANY),
            pl.BlockSpec((None, 1, D_MODEL), lambda i, e, *_: (e[i], 0, 0)),
        ],
        out_specs=tile,
        scratch_shapes=[pltpu.VMEM((2, D_MODEL, 2 * D_FF), F32), pltpu.VMEM((2, D_FF, D_MODEL), F32),
                        pltpu.VMEM((D_MODEL, 2 * D_FF), BF16), pltpu.VMEM((D_FF, D_MODEL), BF16),
                        pltpu.SemaphoreType.DMA((2, 2))],
    )
    return pl.pallas_call(
        functools.partial(_moe_kernel, layer=layer),
        grid_spec=grid_spec,
        out_shape=jax.ShapeDtypeStruct((n_pad * TOKEN_ROWS, V7X_LANES), F32),
        compiler_params=pltpu.CompilerParams(dimension_semantics=("arbitrary",),
                                             vmem_limit_bytes=VMEM_LIMIT),
        name="moe_experts",
    )(*blk, x_buf, w_gu, b_gu.reshape(N_EXPERTS, 1, 2 * D_FF), w_dn, b_dn.reshape(N_EXPERTS, 1, D_MODEL))


def _slot_kernel(start_ref, ei_ref, dest_ref):
    expert, rank = ei_ref[0:TOP_K, :], ei_ref[TOP_K:2 * TOP_K, :]
    start = jnp.zeros_like(expert)
    for e in range(N_EXPERTS):
        start = jnp.where(expert == e, start_ref[e], start)
    dest_ref[...] = start + rank


def _slots(ei, seg_start):
    n = ei.shape[1]
    grid_spec = pltpu.PrefetchScalarGridSpec(
        num_scalar_prefetch=1,
        grid=(n // SLOT_TN,),
        in_specs=[pl.BlockSpec((2 * TOP_K, SLOT_TN), lambda i, s: (0, i))],
        out_specs=pl.BlockSpec((TOP_K, SLOT_TN), lambda i, s: (0, i)),
    )
    return pl.pallas_call(
        _slot_kernel,
        grid_spec=grid_spec,
        out_shape=jax.ShapeDtypeStruct((TOP_K, n), jnp.int32),
        name="moe_slots",
    )(seg_start, ei)


PLAN_ROWS = ("blk_e", "blk_new", "blk_on", "blk_slot", "blk_next", "zero_blk", "zero_on", "seg_start")
PLAN_LANES = 256


def _plan_kernel(cnt_ref, plan_ref, *, n_blk, tail):
    e, lanes = N_EXPERTS, PLAN_LANES
    cnt = cnt_ref[...]
    expert = lax.broadcasted_iota(jnp.int32, (e, lanes), 0).astype(F32)
    lane = lax.broadcasted_iota(jnp.int32, (e, lanes), 1).astype(F32)
    lane_row = lane[0:1, :]

    def as_row(col):
        return jnp.sum(jnp.where(expert == lane, col, 0.0), axis=0, keepdims=True)

    nblk = jnp.floor((cnt + (MOE_TM - 1)) / MOE_TM)
    nblk_row = as_row(nblk)
    end = jnp.sum(jnp.where(lane <= expert, nblk_row, 0.0), axis=1, keepdims=True)
    start = end - nblk
    end_row = jnp.sum(jnp.where(expert <= lane, nblk, 0.0), axis=0, keepdims=True)
    total = end_row[:, e - 1:e]

    blk_e = jnp.minimum(jnp.sum(jnp.where(lane_row >= end, 1.0, 0.0), axis=0, keepdims=True), e - 1.0)
    mine = expert == blk_e

    def per_block(col):
        return jnp.sum(jnp.where(mine, col, 0.0), axis=0, keepdims=True)

    blk_start = per_block(start)
    tokens = jnp.clip(per_block(cnt) - (lane_row - blk_start) * MOE_TM, 0.0, float(MOE_TM))
    blk_on = jnp.where(lane_row >= total, float(BLK_UNUSED),
                       jnp.where(tokens <= MOE_TM // 2, float(BLK_HALF), float(BLK_FULL)))
    blk_new = jnp.where(lane_row == blk_start, 1.0, 0.0)

    used_row = jnp.where(as_row(cnt) > 0.0, 1.0, 0.0)
    runs_before = jnp.sum(jnp.where(lane < expert, used_row, 0.0), axis=1, keepdims=True)
    slot = runs_before - 2.0 * jnp.floor(runs_before / 2.0)
    after = jnp.min(jnp.where((lane > expert) & (used_row > 0.0), lane, float(lanes)), axis=1, keepdims=True)
    after = jnp.where(after >= e, -1.0, after)

    cnt_row = as_row(cnt)
    partly = jnp.where((cnt_row - MOE_TM * jnp.floor(cnt_row / MOE_TM) != 0.0) & (lane_row < e), 1.0, 0.0)
    tail_blk = total + (lane_row - e)
    in_tail = (lane_row >= e) & (lane_row < e + tail) & (tail_blk < n_blk)
    zero_blk = jnp.where(in_tail, tail_blk, (end_row - 1.0) * partly)
    zero_on = jnp.where(in_tail, 1.0, partly)
    seg_start = as_row(start) * MOE_TM

    rows = dict(blk_e=blk_e, blk_new=blk_new, blk_on=blk_on, blk_slot=per_block(slot), blk_next=per_block(after),
                zero_blk=zero_blk, zero_on=zero_on, seg_start=seg_start)
    out_row = lax.broadcasted_iota(jnp.int32, (len(PLAN_ROWS), lanes), 0)
    plan = jnp.zeros((len(PLAN_ROWS), lanes), F32)
    for r, name in enumerate(PLAN_ROWS):
        plan = jnp.where(out_row == r, rows[name], plan)
    plan_ref[...] = plan.astype(jnp.int32)


def _moe(hn_p, hn_s, ei_p, ei_s, counts, layer, w_gu, b_gu, w_dn, b_dn):
    n_asg = (ei_p.shape[1] + ei_s.shape[1]) * TOP_K
    n_pad = (n_asg + N_EXPERTS * (MOE_TM - 1) + MOE_TM - 1) // MOE_TM * MOE_TM
    n_blk = n_pad // MOE_TM
    tail = n_blk - n_asg // MOE_TM
    assert n_blk <= PLAN_LANES and N_EXPERTS + tail <= PLAN_LANES
    plan = pl.pallas_call(
        functools.partial(_plan_kernel, n_blk=n_blk, tail=tail),
        out_shape=jax.ShapeDtypeStruct((len(PLAN_ROWS), PLAN_LANES), jnp.int32),
        name="moe_plan",
    )(counts)
    row = {name: plan[r] for r, name in enumerate(PLAN_ROWS)}
    blk = tuple(row[name][:n_blk] for name in ("blk_e", "blk_new", "blk_on", "blk_slot", "blk_next"))
    n_zero = N_EXPERTS + tail

    dest = _slots(jnp.concatenate([ei_p, ei_s], axis=1), row["seg_start"][:N_EXPERTS])
    dest_p, dest_s = dest[:, :ei_p.shape[1]], dest[:, ei_p.shape[1]:]
    x_buf = _dispatch(dest, row["zero_blk"][:n_zero], row["zero_on"][:n_zero], hn_p, hn_s, n_pad)
    y_buf = _moe_blocks(x_buf, blk, layer, w_gu, b_gu, w_dn, b_dn)
    return dest_p, dest_s, y_buf


def _final_kernel(dest_cur, dest_nxt, x_ref, pr_ref, ybuf, mod_ref, g_ref, o_ref, rows, moe_s, sems):
    x = x_ref[...] + mod_ref[5:6, :] * _combine_experts(dest_cur, dest_nxt, pr_ref, ybuf, rows, moe_s, sems)
    o_ref[...] = _rmsnorm(x, g_ref[...])


def _final_norm(x, route, mod, g, seq):
    n = x.shape[0]
    tm = min(ROW_TILE, seq)
    row = pl.BlockSpec((tm, D_MODEL), lambda i: (i, 0))
    args, specs, scratch = _combine_operands(*route, n, tm)
    return pl.pallas_call(
        _final_kernel,
        grid=(n // tm,),
        in_specs=specs[:2] + [row] + specs[2:] + [_mod_spec(mod, seq, tm),
                                                  pl.BlockSpec((1, D_MODEL), lambda i: (0, 0))],
        out_specs=row,
        out_shape=jax.ShapeDtypeStruct((n, D_MODEL), F32),
        scratch_shapes=scratch,
        compiler_params=pltpu.CompilerParams(dimension_semantics=("arbitrary",), vmem_limit_bytes=VMEM_LIMIT),
        name="final_norm",
    )(*args[:2], x, *args[2:], mod, g.reshape(1, D_MODEL))


def _block_diag(w):
    eye = jnp.eye(HEADS, dtype=w.dtype)
    return (eye[:, None, :, None] * w[:, :, None, :]).reshape(W_GRP, W_GRP)


def _layer_weights(l, p):
    gate_w = jnp.concatenate([_block_diag(p["lru_wr"][l, 0]), _block_diag(p["lru_wi"][l, 0]),
                              _block_diag(p["lru_wr"][l, 1]), _block_diag(p["lru_wi"][l, 1])], axis=1)
    gate_b = jnp.concatenate([p["lru_br"][l, 0], p["lru_bi"][l, 0], p["lru_br"][l, 1], p["lru_bi"][l, 1]])
    head_of = jnp.arange(W_GRP) // HD
    row = lambda v: v.reshape(1, W_GRP)
    return dict(
        conv_a_w=p["conv_a_w"][l], conv_a_b=row(p["conv_a_b"][l]),
        norm_a_g=row(p["norm_a_g"][l]), norm_a_b=row(p["norm_a_b"][l]),
        conv_b_w=p["conv_b_w"][l], conv_b_b=row(p["conv_b_b"][l]),
        gate_w=gate_w.astype(BF16), gate_b=gate_b.reshape(1, 4 * W_GRP), lru_lam=p["lru_lam"][l],
        sgu_norm_g=row(p["sgu_norm_g"][l]), sgu_norm_b=row(p["sgu_norm_b"][l]),
        sgu_w=p["sgu_w"][l].astype(BF16), sgu_bias=jnp.repeat(p["sgu_b"][l].T, HD, axis=1),
        conv_d_w=p["conv_d_w"][l],
        m_head=((head_of[:, None] == head_of[None, :]).astype(F32) / HD).astype(BF16),
    )


def kernel(x_prompt, x_sample, state_rglru, c, c_ctx, w_ada, b_ada, norm1_g, norm2_g, w_in, conv_a_w,
           conv_a_b, norm_a_g, norm_a_b, conv_b_w, conv_b_b, lru_wr, lru_br, lru_wi, lru_bi, lru_lam,
           sgu_norm_g, sgu_norm_b, sgu_w, sgu_b, conv_d_w, w_out, w_router, b_router, w_gu, b_gu, w_dn,
           b_dn, final_g):
    p = dict(conv_a_w=conv_a_w, conv_a_b=conv_a_b, norm_a_g=norm_a_g, norm_a_b=norm_a_b,
             conv_b_w=conv_b_w, conv_b_b=conv_b_b, lru_wr=lru_wr, lru_br=lru_br, lru_wi=lru_wi,
             lru_bi=lru_bi, lru_lam=lru_lam, sgu_norm_g=sgu_norm_g, sgu_norm_b=sgu_norm_b,
             sgu_w=sgu_w, sgu_b=sgu_b, conv_d_w=conv_d_w)
    bp, tp, _ = x_prompt.shape
    bs, ts, _ = x_sample.shape
    n_p, n_s = bp * tp, bs * ts

    cond_rows = jnp.zeros((COND_ROWS, D_MODEL), F32).at[0].set(c_ctx).at[1:1 + bs].set(c)
    mod = _ada_mod(cond_rows, w_ada, b_ada).reshape(DEPTH, COND_ROWS, 6, D_MODEL)

    xp = x_prompt.reshape(n_p, D_MODEL)
    xs = x_sample.reshape(n_s, D_MODEL)
    h0_ctx = jnp.zeros((bp, 2, W_GRP), F32)
    route_p = route_s = mod_p_prev = mod_s_prev = None
    no_tokens_seen = jnp.zeros((N_EXPERTS, 1), F32)
    states = []
    for l in range(DEPTH):
        lw = _layer_weights(l, p)
        mod_p, mod_s = mod[l, 0:1], mod[l, 1:1 + bs]
        w_in_l = w_in[l].astype(BF16)
        w_out_l = w_out[l].astype(BF16)
        xp, proj_p = _in_proj(xp, route_p, mod_p_prev, mod_p, norm1_g[l], w_in_l, tp)
        xs, proj_s = _in_proj(xs, route_s, mod_s_prev, mod_s, norm1_g[l], w_in_l, ts)
        y_p, st = _mixers(proj_p, h0_ctx, lw, bp, tp, False)
        y_s, _ = _mixers(proj_s, state_rglru[:, l], lw, bs, ts, True)
        states.append(st)
        xp, hn_p, ei_p, pr_p, seen = _out_proj(y_p, xp, mod_p, norm2_g[l], w_out_l, w_router[l],
                                               b_router[l], no_tokens_seen, tp)
        xs, hn_s, ei_s, pr_s, counts = _out_proj(y_s, xs, mod_s, norm2_g[l], w_out_l, w_router[l],
                                                 b_router[l], seen, ts)
        dest_p, dest_s, y_buf = _moe(hn_p, hn_s, ei_p, ei_s, counts, l, w_gu, b_gu[l], w_dn, b_dn[l])
        route_p, route_s = (dest_p, pr_p, y_buf), (dest_s, pr_s, y_buf)
        mod_p_prev, mod_s_prev = mod_p, mod_s
    y_prompt = _final_norm(xp, route_p, mod_p_prev, final_g, tp).reshape(bp, tp, D_MODEL)
    y_sample = _final_norm(xs, route_s, mod_s_prev, final_g, ts).reshape(bs, ts, D_MODEL)
    return y_prompt, y_sample, jnp.stack(states, axis=1)
```

```python
import functools

import jax
import jax.numpy as jnp
from jax import lax
from jax.experimental import pallas as pl
from jax.experimental.pallas import tpu as pltpu

F32 = jnp.float32
BF16 = jnp.bfloat16

D_MODEL = 1024
DEPTH = 2
GRID_W = 64
N_MIXERS = 4
W_GRP = D_MODEL // N_MIXERS
HEADS = 4
HD = W_GRP // HEADS
N_IN = 9 * W_GRP
CONV_A = 31
CONV_B = 4
CONV_D = 3
CHUNK = 128
LRU_C = 8.0
N_EXPERTS = 32
TOP_K = 4
D_FF = D_MODEL
SWIGLU_LIMIT = 7.0
SWIGLU_ALPHA = 1.702
EPS = 1e-6

V7X_SUBLANES = 8
V7X_LANES = 128
TOKEN_ROWS = D_MODEL // V7X_LANES
DMA_PRIORITIES = 2
V7X_VMEM_BYTES = 64 * 1024 * 1024
VMEM_LIMIT = V7X_VMEM_BYTES * 7 // 8

COND_ROWS = 16
ADA_TN = 1536
ROW_TILE = 512
CONV_ROWS = GRID_W
CONV_PAD = 16
NORM_ROWS = 256
MIX_ROWS = 1024
MOE_TM = 512
BLK_UNUSED, BLK_HALF, BLK_FULL = 0, 1, 2
DISPATCH_TM = 1024
SLOT_TN = 2048
COMBINE_ROWS = 32


def _rmsnorm(x, g):
    return x * lax.rsqrt(jnp.mean(x * x, axis=-1, keepdims=True) + EPS) * g


def _group_mean(x, m_ref):
    hi = x.astype(BF16)
    lo = (x - hi.astype(F32)).astype(BF16)
    m = m_ref[...]
    return (jnp.dot(hi, m, preferred_element_type=F32) + jnp.dot(lo, m, preferred_element_type=F32))


def _layernorm(x, g, b, m_ref):
    xc = x - _group_mean(x, m_ref)
    var = _group_mean(xc * xc, m_ref)
    return xc * lax.rsqrt(var + EPS) * g + b


def _ada_kernel(c_ref, w_ref, b_ref, o_ref):
    c = c_ref[...]
    cond = (c * jax.nn.sigmoid(c)).astype(BF16)
    o_ref[...] = jnp.dot(cond, w_ref[...].astype(BF16), preferred_element_type=F32) + b_ref[...]


def _ada_mod(cond_rows, w_ada, b_ada):
    n_col = w_ada.shape[-1]
    return pl.pallas_call(
        _ada_kernel,
        grid=(DEPTH, n_col // ADA_TN),
        in_specs=[
            pl.BlockSpec((COND_ROWS, D_MODEL), lambda l, j: (0, 0)),
            pl.BlockSpec((None, D_MODEL, ADA_TN), lambda l, j: (l, 0, j)),
            pl.BlockSpec((None, 1, ADA_TN), lambda l, j: (l, 0, j)),
        ],
        out_specs=pl.BlockSpec((None, COND_ROWS, ADA_TN), lambda l, j: (l, 0, j)),
        out_shape=jax.ShapeDtypeStruct((DEPTH, COND_ROWS, n_col), F32),
        name="ada_mod",
    )(cond_rows, w_ada, b_ada.reshape(DEPTH, 1, n_col))


def _token_rows(t, count=1):
    return pl.ds(pl.multiple_of(t * TOKEN_ROWS, TOKEN_ROWS), count * TOKEN_ROWS)


def _store_token_tiles(ref, x):
    n = x.shape[0]
    for c in range(TOKEN_ROWS):
        ref[pl.ds(c, n, stride=TOKEN_ROWS), :] = x[:, c * V7X_LANES:(c + 1) * V7X_LANES]


def _load_token_tiles(ref, n):
    return [ref[pl.ds(c, n, stride=TOKEN_ROWS), :] for c in range(TOKEN_ROWS)]


def _combine_experts(dest_cur, dest_nxt, pr_ref, ybuf, rows, moe_s, sems):
    i = pl.program_id(0)
    n_steps = pl.num_programs(0)
    tm = rows.shape[2] // TOKEN_ROWS
    slot = i % 2

    def row_copy(dref, j, k, slot_):
        return pltpu.make_async_copy(ybuf.at[_token_rows(dref[0, k * tm + j])],
                                     rows.at[slot_, k, _token_rows(j)], sems.at[slot_])

    def issue(dref, slot_):
        def body(j, carry):
            for k in range(TOP_K):
                row_copy(dref, j, k, slot_).start(priority=k % DMA_PRIORITIES)
            return carry

        lax.fori_loop(0, tm, body, 0, unroll=4)

    @pl.when(i == 0)
    def _():
        issue(dest_cur, 0)

    for nxt in (0, 1):
        @pl.when((i + 1 < n_steps) & (slot != nxt))
        def _():
            issue(dest_nxt, nxt)

    for k in range(TOP_K):
        pltpu.make_async_copy(ybuf.at[_token_rows(0, tm)], rows.at[slot, k], sems.at[slot]).wait()
    pad = jnp.zeros((V7X_LANES - pr_ref.shape[0], tm), F32)
    probs = jnp.concatenate([pr_ref[...], pad], axis=0).T
    for t0 in range(0, tm, COMBINE_ROWS):
        weight = [probs[t0:t0 + COMBINE_ROWS, k:k + 1] for k in range(TOP_K)]
        for c in range(TOKEN_ROWS):
            chunk = pl.ds(t0 * TOKEN_ROWS + c, COMBINE_ROWS, stride=TOKEN_ROWS)
            acc = weight[0] * rows.at[slot, 0][chunk, :]
            for k in range(1, TOP_K):
                acc = acc + weight[k] * rows.at[slot, k][chunk, :]
            moe_s[t0:t0 + COMBINE_ROWS, c * V7X_LANES:(c + 1) * V7X_LANES] = acc
    return moe_s[...]


def _inproj_kernel(*refs, has_res):
    if has_res:
        (dest_cur, dest_nxt, x_ref, pr_ref, ybuf, modp_ref, mod_ref, g_ref, w_ref,
         xo_ref, p_ref, rows, moe_s, sems) = refs
        x = x_ref[...] + modp_ref[5:6, :] * _combine_experts(dest_cur, dest_nxt, pr_ref, ybuf, rows, moe_s, sems)
        xo_ref[...] = x
    else:
        x_ref, mod_ref, g_ref, w_ref, p_ref = refs
        x = x_ref[...]
    hn = _rmsnorm(x, g_ref[...]) * (1.0 + mod_ref[1:2, :]) + mod_ref[0:1, :]
    p_ref[...] = jnp.dot(hn.astype(BF16), w_ref[...], preferred_element_type=F32)


def _row_tile(mod, seq):
    return ROW_TILE if mod.shape[0] == 1 else min(ROW_TILE, seq)


def _mod_spec(mod, seq, tm):
    if mod.shape[0] == 1:
        return pl.BlockSpec((None, 6, D_MODEL), lambda i: (0, 0, 0))
    return pl.BlockSpec((None, 6, D_MODEL), lambda i: ((i * tm) // seq, 0, 0))


def _per_tile(dest, tm):
    n_steps = dest.shape[1] // tm
    return dest.reshape(TOP_K, n_steps, tm).transpose(1, 0, 2).reshape(n_steps, 1, TOP_K * tm)


def _combine_operands(dest, probs, y_buf, n, tm):
    n_steps = n // tm
    dest3 = _per_tile(dest, tm)
    smem = functools.partial(pl.BlockSpec, (None, 1, tm * TOP_K), memory_space=pltpu.SMEM)
    args = [dest3, dest3, probs, y_buf]
    specs = [smem(lambda i: (i, 0, 0)),
             smem(lambda i: (jnp.minimum(i + 1, n_steps - 1), 0, 0)),
             pl.BlockSpec((2 * TOP_K, tm), lambda i: (0, i)),
             pl.BlockSpec(memory_space=pl.ANY)]
    scratch = [pltpu.VMEM((2, TOP_K, tm * TOKEN_ROWS, V7X_LANES), F32), pltpu.VMEM((tm, D_MODEL), F32),
               pltpu.SemaphoreType.DMA((2,))]
    return args, specs, scratch


def _in_proj(x, route, mod_prev, mod, g, w_bf16, seq):
    n = x.shape[0]
    tm = _row_tile(mod, seq)
    row = pl.BlockSpec((tm, D_MODEL), lambda i: (i, 0))
    has_res = route is not None
    args, specs, scratch = [], [], []
    if has_res:
        args, specs, scratch = _combine_operands(*route, n, tm)
        args = args[:2] + [x] + args[2:] + [mod_prev]
        specs = specs[:2] + [row] + specs[2:] + [_mod_spec(mod_prev, seq, tm)]
    else:
        args, specs = [x], [row]
    args += [mod, g.reshape(1, D_MODEL), w_bf16]
    specs += [_mod_spec(mod, seq, tm),
              pl.BlockSpec((1, D_MODEL), lambda i: (0, 0)),
              pl.BlockSpec((D_MODEL, N_IN), lambda i: (0, 0))]
    proj_shape = jax.ShapeDtypeStruct((n, N_IN), F32)
    proj_spec = pl.BlockSpec((tm, N_IN), lambda i: (i, 0))
    if has_res:
        out_shape = (jax.ShapeDtypeStruct((n, D_MODEL), F32), proj_shape)
        out_specs = (row, proj_spec)
    else:
        out_shape, out_specs = proj_shape, proj_spec
    out = pl.pallas_call(
        functools.partial(_inproj_kernel, has_res=has_res),
        grid=(n // tm,),
        in_specs=specs,
        out_specs=out_specs,
        out_shape=out_shape,
        scratch_shapes=scratch,
        compiler_params=pltpu.CompilerParams(dimension_semantics=("arbitrary",), vmem_limit_bytes=VMEM_LIMIT),
        name="in_proj",
    )(*args)
    return out if has_res else (x, out)


def _group_scan(a, b, reverse):
    shape = a.shape
    grouped = (shape[0] // V7X_SUBLANES, V7X_SUBLANES, shape[1])
    a, b = a.reshape(grouped), b.reshape(grouped)
    ri = lax.broadcasted_iota(jnp.int32, grouped, 1)
    for d in (1, 2, 4):
        shift = V7X_SUBLANES - d if reverse else d
        keep = ri < V7X_SUBLANES - d if reverse else ri >= d
        ra, rb = pltpu.roll(a, shift, 1), pltpu.roll(b, shift, 1)
        b = a * jnp.where(keep, rb, 0.0) + b
        a = a * jnp.where(keep, ra, 1.0)
    return a.reshape(shape), b.reshape(shape)


def _mixer_kernel(s0, s1, s2, caw, cab, nag, nab, cbw, cbb, wg, bg, lam, h0, sng, snb, sw, sbias, cdw,
                  m_head, y_ref, st_ref, *scratch, seq, on_grid):
    def one_sequence(b, carry):
        _mix_sequence(s0.at[b], s1.at[b], s2.at[b], caw, cab, nag, nab, cbw, cbb, wg, bg, lam, h0.at[b],
                      sng, snb, sw, sbias, cdw, m_head, y_ref.at[b], st_ref.at[b], *scratch,
                      seq=seq, on_grid=on_grid)
        return carry

    lax.fori_loop(0, s0.shape[0], one_sequence, 0)


def _mix_sequence(s0, s1, s2, caw, cab, nag, nab, cbw, cbb, wg, bg, lam, h0, sng, snb, sw, sbias, cdw,
                  m_head, y_ref, st_ref, pad_s, af_s, bf_s, ab_s, bb_s, *, seq, on_grid):
    mixer = pl.program_id(1)
    n_conv = seq // CONV_ROWS
    win_rows = CONV_ROWS + 2 * CONV_PAD

    def rows_at(c, size):
        return pl.ds(pl.multiple_of(c * size, size), size)

    def fill_padded(fn):
        zeros = jnp.zeros((CONV_PAD, W_GRP), F32)
        pad_s[0:CONV_PAD, :] = zeros
        pad_s[CONV_PAD + seq:2 * CONV_PAD + seq, :] = zeros

        def body(c, carry):
            dst = pl.ds(pl.multiple_of(c * CONV_ROWS + CONV_PAD, V7X_SUBLANES), CONV_ROWS)
            pad_s[dst, :] = fn(rows_at(c, CONV_ROWS))
            return carry

        lax.fori_loop(0, n_conv, body, 0)

    def conv_window(win, w_ref, taps, pad_l):
        acc = jnp.zeros((CONV_ROWS, W_GRP), F32)
        for mis in range(V7X_SUBLANES):
            starts = [(k, CONV_PAD - pad_l + k) for k in range(taps)
                      if (CONV_PAD - pad_l + k) % V7X_SUBLANES == mis]
            if not starts:
                continue
            shifted = pltpu.roll(win, win_rows - mis, 0) if mis else win
            for k, start in starts:
                acc = acc + w_ref[k:k + 1, :] * shifted[start - mis:start - mis + CONV_ROWS, :]
        return acc

    def conv_rows(c, w_ref, taps, pad_l):
        win = pad_s[pl.ds(pl.multiple_of(c * CONV_ROWS, CONV_ROWS), win_rows), :]
        return conv_window(win, w_ref, taps, pad_l)

    @pl.when(mixer == 0)
    def _conformer():
        def glu(rows):
            return s0[rows, :] * jax.nn.sigmoid(s1[rows, :])

        if not on_grid:
            fill_padded(glu)

        def body(c, carry):
            if on_grid:
                edge = jnp.zeros((CONV_PAD, W_GRP), F32)
                win = jnp.concatenate([edge, glu(rows_at(c, CONV_ROWS)), edge], axis=0)
                z = conv_window(win, caw, CONV_A, CONV_A // 2)
            else:
                z = conv_rows(c, caw, CONV_A, CONV_A // 2)
            af_s[rows_at(c, CONV_ROWS), :] = z + cab[...]
            return carry

        lax.fori_loop(0, n_conv, body, 0, unroll=2)

        def norm(c, carry):
            rows = rows_at(c, NORM_ROWS)
            z = _layernorm(af_s[rows, :], nag[...], nab[...], m_head)
            y_ref[rows, :] = (z * jax.nn.sigmoid(z)).astype(BF16)
            return carry

        lax.fori_loop(0, seq // NORM_ROWS, norm, 0, unroll=4)

    @pl.when(mixer == 1)
    def _rglru():
        fill_padded(lambda rows: s0[rows, :])
        lam_v = lam[...]
        softplus_neg = jnp.maximum(-lam_v, 0.0) + jnp.log1p(jnp.exp(-jnp.abs(lam_v)))

        def gates(c, carry):
            rows = rows_at(c, CONV_ROWS)
            xc = conv_rows(c, cbw, CONV_B, 2) + cbb[...]
            gt = jax.nn.sigmoid(jnp.dot(xc.astype(BF16), wg[...], preferred_element_type=F32) + bg[...])
            for d, (a_s, b_s) in enumerate(((af_s, bf_s), (ab_s, bb_s))):
                r_gate = gt[:, (2 * d) * W_GRP:(2 * d + 1) * W_GRP]
                i_gate = gt[:, (2 * d + 1) * W_GRP:(2 * d + 2) * W_GRP]
                log_a = -LRU_C * r_gate * softplus_neg[d:d + 1, :]
                a = jnp.exp(log_a)
                b = jnp.sqrt(jnp.maximum(-jnp.tanh(log_a) * (a * a + 1.0), 0.0)) * (i_gate * xc)
                a, b = _group_scan(a, b, reverse=(d == 1))
                a_s[rows, :] = a
                b_s[rows, :] = b
            return carry

        lax.fori_loop(0, n_conv, gates, 0, unroll=4)

        n_grp = seq // V7X_SUBLANES

        def chain(g, carry):
            cf, cb = carry
            rf = rows_at(g, V7X_SUBLANES)
            rb = rows_at(n_grp - 1 - g, V7X_SUBLANES)
            hf = af_s[rf, :] * cf + bf_s[rf, :]
            bf_s[rf, :] = hf
            hb = ab_s[rb, :] * cb + bb_s[rb, :]
            bb_s[rb, :] = hb
            cf = jnp.broadcast_to(hf[V7X_SUBLANES - 1:V7X_SUBLANES, :], (V7X_SUBLANES, W_GRP))
            cb = jnp.broadcast_to(hb[0:1, :], (V7X_SUBLANES, W_GRP))
            return cf, cb

        init = (jnp.broadcast_to(h0[0:1, :], (V7X_SUBLANES, W_GRP)),
                jnp.broadcast_to(h0[1:2, :], (V7X_SUBLANES, W_GRP)))
        cf, cb = lax.fori_loop(0, n_grp, chain, init, unroll=8)
        st_ref[0:1, :] = cf[0:1, :]
        st_ref[1:2, :] = cb[0:1, :]

        def out(c, carry):
            rows = rows_at(c, CONV_ROWS)
            y_ref[rows, :] = (jax.nn.gelu(s1[rows, :]) * (bf_s[rows, :] + bb_s[rows, :])).astype(BF16)
            return carry

        lax.fori_loop(0, n_conv, out, 0)

    @pl.when(mixer == 2)
    def _sgu():
        lane = lax.broadcasted_iota(jnp.int32, (CHUNK, W_GRP), 1)

        def body(n, carry):
            rows = rows_at(n, CHUNK)
            v = s1[rows, :]
            vc = v - jnp.mean(v, axis=-1, keepdims=True)
            var = jnp.mean(vc * vc, axis=-1, keepdims=True)
            v = (vc * lax.rsqrt(var + EPS) * sng[...] + snb[...]).astype(BF16)
            s = sbias[...]
            for h in range(HEADS):
                sh = jnp.dot(sw[h], v, preferred_element_type=F32)
                s = s + jnp.where((lane >= h * HD) & (lane < (h + 1) * HD), sh, 0.0)
            y_ref[rows, :] = (s0[rows, :] * s).astype(BF16)
            return carry

        lax.fori_loop(0, seq // CHUNK, body, 0, unroll=4)

    @pl.when(mixer == 3)
    def _gated_conv():
        fill_padded(lambda rows: s1[rows, :] * s2[rows, :])

        def body(c, carry):
            rows = rows_at(c, CONV_ROWS)
            y_ref[rows, :] = (s0[rows, :] * conv_rows(c, cdw, CONV_D, CONV_D // 2)).astype(BF16)
            return carry

        lax.fori_loop(0, n_conv, body, 0)


def _mixers(proj, h0, lw, batch, seq, on_grid):
    proj3 = proj.reshape(batch, seq, N_IN)
    seqs = max(1, MIX_ROWS // seq)
    slab = (seqs, seq, W_GRP)
    state = pl.BlockSpec((seqs, 2, W_GRP), lambda b, m: (b, 0, 0))

    def const(shape):
        return pl.BlockSpec(shape, lambda b, m: (0,) * len(shape))

    in_specs = [
        pl.BlockSpec(slab, lambda b, m: (b, 0, 2 * m)),
        pl.BlockSpec(slab, lambda b, m: (b, 0, 2 * m + 1)),
        pl.BlockSpec(slab, lambda b, m: (b, 0, 8)),
        const((CONV_A, W_GRP)), const((1, W_GRP)), const((1, W_GRP)), const((1, W_GRP)),
        const((CONV_B, W_GRP)), const((1, W_GRP)),
        const((W_GRP, 4 * W_GRP)), const((1, 4 * W_GRP)), const((2, W_GRP)),
        state,
        const((1, W_GRP)), const((1, W_GRP)), const((HEADS, CHUNK, CHUNK)), const((CHUNK, W_GRP)),
        const((CONV_D, W_GRP)), const((W_GRP, W_GRP)),
    ]
    y, st = pl.pallas_call(
        functools.partial(_mixer_kernel, seq=seq, on_grid=on_grid),
        grid=(batch // seqs, N_MIXERS),
        in_specs=in_specs,
        out_specs=(pl.BlockSpec(slab, lambda b, m: (b, 0, m)), state),
        out_shape=(jax.ShapeDtypeStruct((batch, seq, D_MODEL), BF16),
                   jax.ShapeDtypeStruct((batch, 2, W_GRP), F32)),
        scratch_shapes=[pltpu.VMEM((seq + 2 * CONV_PAD, W_GRP), F32)] + [pltpu.VMEM((seq, W_GRP), F32)] * 4,
        compiler_params=pltpu.CompilerParams(dimension_semantics=("arbitrary", "arbitrary"),
                                             vmem_limit_bytes=VMEM_LIMIT),
        name="mixers",
    )(proj3, proj3, proj3, lw["conv_a_w"], lw["conv_a_b"], lw["norm_a_g"], lw["norm_a_b"],
      lw["conv_b_w"], lw["conv_b_b"], lw["gate_w"], lw["gate_b"], lw["lru_lam"], h0,
      lw["sgu_norm_g"], lw["sgu_norm_b"], lw["sgu_w"], lw["sgu_bias"], lw["conv_d_w"],
      lw["m_head"])
    return y.reshape(batch * seq, D_MODEL), st


def _outproj_kernel(y_ref, x_ref, mod_ref, g_ref, wo_ref, wr_ref, br_ref, tri_ref, cin_ref,
                    xm_ref, hn_ref, ei_ref, pr_ref, cnt_ref, seen_s):
    tm = x_ref.shape[0]
    e = N_EXPERTS

    @pl.when(pl.program_id(0) == 0)
    def _():
        seen_s[...] = cin_ref[...]

    y = jnp.dot(y_ref[...], wo_ref[...], preferred_element_type=F32)
    x = x_ref[...] + mod_ref[2:3, :] * y
    xm_ref[...] = x
    hn = _rmsnorm(x, g_ref[...]) * (1.0 + mod_ref[4:5, :]) + mod_ref[3:4, :]
    _store_token_tiles(hn_ref, hn)

    h1 = hn.astype(BF16)
    r1 = hn - h1.astype(F32)
    h2 = r1.astype(BF16)
    h3 = (r1 - h2.astype(F32)).astype(BF16)
    w_t = wr_ref[...]
    contract_features = (((1,), (1,)), ((), ()))
    p1 = lax.dot_general(w_t, h1, contract_features, preferred_element_type=F32)
    p2 = lax.dot_general(w_t, h2, contract_features, preferred_element_type=F32)
    p3 = lax.dot_general(w_t, h3, contract_features, preferred_element_type=F32)
    logits = (p1[2 * e:3 * e] + p2[e:2 * e] + p3[0:e] + p1[e:2 * e] + p2[0:e] + p1[0:e]) + br_ref[...]

    expert = lax.broadcasted_iota(jnp.int32, (e, tm), 0)
    beats = jnp.zeros((e, tm), F32)
    for other in range(e):
        lo = logits[other:other + 1, :]
        ahead = (lo > logits) | ((lo == logits) & (expert > other))
        beats = beats + jnp.where(ahead, 1.0, 0.0)

    chosen = jnp.where(beats < float(TOP_K), 1.0, 0.0)
    earlier = jnp.dot(chosen.astype(BF16), tri_ref[...], preferred_element_type=F32) + seen_s[...]
    seen_s[...] = earlier[:, tm - 1:tm] + chosen[:, tm - 1:tm]
    cnt_ref[...] = seen_s[...]

    def pick(k, values):
        return jnp.sum(jnp.where(beats == float(k), values, 0.0), axis=0, keepdims=True)

    expert_f = expert.astype(F32)
    vals = [pick(k, logits) for k in range(TOP_K)]
    exps = [jnp.exp(v - vals[0]) for v in vals]
    den = exps[0] + exps[1] + exps[2] + exps[3]
    out_row = lax.broadcasted_iota(jnp.int32, (2 * TOP_K, tm), 0)
    ei = jnp.zeros((2 * TOP_K, tm), F32)
    pr = jnp.zeros((2 * TOP_K, tm), F32)
    for k in range(TOP_K):
        ei = jnp.where(out_row == k, pick(k, expert_f), ei)
        ei = jnp.where(out_row == TOP_K + k, pick(k, earlier), ei)
        pr = jnp.where(out_row == k, exps[k] / den, pr)
    ei_ref[...] = ei.astype(jnp.int32)
    pr_ref[...] = pr


def _out_proj(y, x, mod, g, wo_bf16, w_router, b_router, seen, seq):
    n = x.shape[0]
    tm = _row_tile(mod, seq)
    row = pl.BlockSpec((tm, D_MODEL), lambda i: (i, 0))
    route = pl.BlockSpec((2 * TOP_K, tm), lambda i: (0, i))
    cnt = pl.BlockSpec((N_EXPERTS, 1), lambda i: (0, 0))
    before = jnp.tri(tm, k=-1, dtype=BF16).T
    w1 = w_router.astype(BF16)
    w2 = (w_router - w1.astype(F32)).astype(BF16)
    w3 = (w_router - w1.astype(F32) - w2.astype(F32)).astype(BF16)
    w_t = jnp.concatenate([w1.T, w2.T, w3.T, jnp.zeros_like(w1.T)], axis=0)
    return pl.pallas_call(
        _outproj_kernel,
        grid=(n // tm,),
        in_specs=[row, row, _mod_spec(mod, seq, tm),
                  pl.BlockSpec((1, D_MODEL), lambda i: (0, 0)),
                  pl.BlockSpec((D_MODEL, D_MODEL), lambda i: (0, 0)),
                  pl.BlockSpec((4 * N_EXPERTS, D_MODEL), lambda i: (0, 0)),
                  cnt, pl.BlockSpec((tm, tm), lambda i: (0, 0)), cnt],
        out_specs=(row, pl.BlockSpec((tm * TOKEN_ROWS, V7X_LANES), lambda i: (i, 0)), route, route, cnt),
        out_shape=(jax.ShapeDtypeStruct((n, D_MODEL), F32),
                   jax.ShapeDtypeStruct((n * TOKEN_ROWS, V7X_LANES), F32),
                   jax.ShapeDtypeStruct((2 * TOP_K, n), jnp.int32),
                   jax.ShapeDtypeStruct((2 * TOP_K, n), F32),
                   jax.ShapeDtypeStruct((N_EXPERTS, 1), F32)),
        scratch_shapes=[pltpu.VMEM((N_EXPERTS, 1), F32)],
        compiler_params=pltpu.CompilerParams(dimension_semantics=("arbitrary",)),
        name="out_proj",
    )(y, x, mod, g.reshape(1, D_MODEL), wo_bf16, w_t, b_router.reshape(N_EXPERTS, 1), before, seen)


def _dispatch_kernel(zero_blk, zero_on, dest_ref, hn_p, hn_s, xbuf, zeros_s, sem, zsem, *, tiles_p):
    i = pl.program_id(0)

    @pl.when(i == 0)
    def _():
        zeros_s[...] = jnp.zeros_like(zeros_s)

        def fill(j):
            return pltpu.make_async_copy(zeros_s, xbuf.at[_token_rows(zero_blk[j] * MOE_TM, MOE_TM)], zsem)

        for j in range(zero_blk.shape[0]):
            @pl.when(zero_on[j] == 1)
            def _():
                fill(j).start()

        for j in range(zero_blk.shape[0]):
            @pl.when(zero_on[j] == 1)
            def _():
                fill(j).wait()

    def issue(src):
        def body(j, carry):
            row = src.at[_token_rows(j)]
            for k in range(TOP_K):
                pltpu.make_async_copy(row, xbuf.at[_token_rows(dest_ref[0, k * DISPATCH_TM + j])],
                                      sem).start(priority=k % DMA_PRIORITIES)
            return carry

        lax.fori_loop(0, DISPATCH_TM, body, 0, unroll=4)

    @pl.when(i < tiles_p)
    def _():
        issue(hn_p)

    @pl.when(i >= tiles_p)
    def _():
        issue(hn_s)

    for _ in range(TOP_K):
        pltpu.make_async_copy(hn_s, xbuf.at[_token_rows(0, DISPATCH_TM)], sem).wait()


def _dispatch(dest, zero_blk, zero_on, hn_p, hn_s, n_pad):
    n_p, n_s = hn_p.shape[0] // TOKEN_ROWS, hn_s.shape[0] // TOKEN_ROWS
    tiles_p = n_p // DISPATCH_TM
    n_steps = (n_p + n_s) // DISPATCH_TM
    tile = (DISPATCH_TM * TOKEN_ROWS, V7X_LANES)
    grid_spec = pltpu.PrefetchScalarGridSpec(
        num_scalar_prefetch=2,
        grid=(n_steps,),
        in_specs=[pl.BlockSpec((None, 1, DISPATCH_TM * TOP_K), lambda i, zb, zo: (i, 0, 0),
                               memory_space=pltpu.SMEM),
                  pl.BlockSpec(tile, lambda i, zb, zo: (jnp.minimum(i, tiles_p - 1), 0)),
                  pl.BlockSpec(tile, lambda i, zb, zo: (jnp.maximum(i - tiles_p, 0), 0))],
        out_specs=pl.BlockSpec(memory_space=pl.ANY),
        scratch_shapes=[pltpu.VMEM((MOE_TM * TOKEN_ROWS, V7X_LANES), F32), pltpu.SemaphoreType.DMA(()),
                        pltpu.SemaphoreType.DMA(())],
    )
    return pl.pallas_call(
        functools.partial(_dispatch_kernel, tiles_p=tiles_p),
        grid_spec=grid_spec,
        out_shape=jax.ShapeDtypeStruct((n_pad * TOKEN_ROWS, V7X_LANES), F32),
        compiler_params=pltpu.CompilerParams(dimension_semantics=("arbitrary",)),
        name="moe_dispatch",
    )(zero_blk, zero_on, _per_tile(dest, DISPATCH_TM), hn_p, hn_s)


def _moe_kernel(blk_e, blk_new, blk_on, blk_slot, blk_next, x_ref, wgu_hbm, bgu_ref, wdn_hbm, bdn_ref, o_ref,
                wgu_f, wdn_f, wgu_s, wdn_s, sems, *, layer):
    i = pl.program_id(0)

    def fetch(expert, slot):
        return (pltpu.make_async_copy(wgu_hbm.at[layer, expert], wgu_f.at[slot], sems.at[0, slot]),
                pltpu.make_async_copy(wdn_hbm.at[layer, expert], wdn_f.at[slot], sems.at[1, slot]))

    @pl.when(i == 0)
    def _():
        for copy in fetch(blk_e[0], 0):
            copy.start()

    @pl.when(blk_on[i] == BLK_UNUSED)
    def _():
        o_ref[...] = jnp.zeros_like(o_ref)

    @pl.when(blk_on[i] != BLK_UNUSED)
    def _():
        @pl.when(blk_new[i] == 1)
        def _():
            slot = blk_slot[i]
            for copy in fetch(blk_e[i], slot):
                copy.wait()

            @pl.when(blk_next[i] >= 0)
            def _():
                for copy in fetch(blk_next[i], 1 - slot):
                    copy.start()

            wgu_s[...] = wgu_f[slot].astype(BF16)
            wdn_s[...] = wdn_f[slot].astype(BF16)

        def expert_rows(n):
            x = jnp.concatenate([c.astype(BF16) for c in _load_token_tiles(x_ref, n)], axis=1)
            gu = jnp.dot(x, wgu_s[...], preferred_element_type=F32) + bgu_ref[...]
            g = jnp.minimum(gu[:, :D_FF], SWIGLU_LIMIT)
            u = jnp.clip(gu[:, D_FF:], -SWIGLU_LIMIT, SWIGLU_LIMIT)
            act = (u + 1.0) * (g * jax.nn.sigmoid(SWIGLU_ALPHA * g))
            _store_token_tiles(o_ref, jnp.dot(act.astype(BF16), wdn_s[...], preferred_element_type=F32)
                               + bdn_ref[...])

        @pl.when(blk_on[i] == BLK_FULL)
        def _():
            expert_rows(MOE_TM)

        @pl.when(blk_on[i] == BLK_HALF)
        def _():
            expert_rows(MOE_TM // 2)
            o_ref[_token_rows(MOE_TM // 2, MOE_TM // 2), :] = jnp.zeros(
                (MOE_TM // 2 * TOKEN_ROWS, V7X_LANES), F32)


def _moe_blocks(x_buf, blk, layer, w_gu, b_gu, w_dn, b_dn):
    n_pad = x_buf.shape[0] // TOKEN_ROWS
    n_blk = n_pad // MOE_TM
    tile = pl.BlockSpec((MOE_TM * TOKEN_ROWS, V7X_LANES), lambda i, e, *_: (i, 0))
    grid_spec = pltpu.PrefetchScalarGridSpec(
        num_scalar_prefetch=len(blk),
        grid=(n_blk,),
        in_specs=[
            tile,
            pl.BlockSpec(memory_space=pl.ANY),
            pl.BlockSpec((None, 1, 2 * D_FF), lambda i, e, *_: (e[i], 0, 0)),
            pl.BlockSpec(memory_space=pl.ANY),
            pl.BlockSpec((None, 1, D_MODEL), lambda i, e, *_: (e[i], 0, 0)),
        ],
        out_specs=tile,
        scratch_shapes=[pltpu.VMEM((2, D_MODEL, 2 * D_FF), F32), pltpu.VMEM((2, D_FF, D_MODEL), F32),
                        pltpu.VMEM((D_MODEL, 2 * D_FF), BF16), pltpu.VMEM((D_FF, D_MODEL), BF16),
                        pltpu.SemaphoreType.DMA((2, 2))],
    )
    return pl.pallas_call(
        functools.partial(_moe_kernel, layer=layer),
        grid_spec=grid_spec,
        out_shape=jax.ShapeDtypeStruct((n_pad * TOKEN_ROWS, V7X_LANES), F32),
        compiler_params=pltpu.CompilerParams(dimension_semantics=("arbitrary",),
                                             vmem_limit_bytes=VMEM_LIMIT),
        name="moe_experts",
    )(*blk, x_buf, w_gu, b_gu.reshape(N_EXPERTS, 1, 2 * D_FF), w_dn, b_dn.reshape(N_EXPERTS, 1, D_MODEL))


def _slot_kernel(start_ref, ei_ref, dest_ref):
    expert, rank = ei_ref[0:TOP_K, :], ei_ref[TOP_K:2 * TOP_K, :]
    start = jnp.zeros_like(expert)
    for e in range(N_EXPERTS):
        start = jnp.where(expert == e, start_ref[e], start)
    dest_ref[...] = start + rank


def _slots(ei, seg_start):
    n = ei.shape[1]
    grid_spec = pltpu.PrefetchScalarGridSpec(
        num_scalar_prefetch=1,
        grid=(n // SLOT_TN,),
        in_specs=[pl.BlockSpec((2 * TOP_K, SLOT_TN), lambda i, s: (0, i))],
        out_specs=pl.BlockSpec((TOP_K, SLOT_TN), lambda i, s: (0, i)),
    )
    return pl.pallas_call(
        _slot_kernel,
        grid_spec=grid_spec,
        out_shape=jax.ShapeDtypeStruct((TOP_K, n), jnp.int32),
        name="moe_slots",
    )(seg_start, ei)


PLAN_ROWS = ("blk_e", "blk_new", "blk_on", "blk_slot", "blk_next", "zero_blk", "zero_on", "seg_start")
PLAN_LANES = 256


def _plan_kernel(cnt_ref, plan_ref, *, n_blk, tail):
    e, lanes = N_EXPERTS, PLAN_LANES
    cnt = cnt_ref[...]
    expert = lax.broadcasted_iota(jnp.int32, (e, lanes), 0).astype(F32)
    lane = lax.broadcasted_iota(jnp.int32, (e, lanes), 1).astype(F32)
    lane_row = lane[0:1, :]

    def as_row(col):
        return jnp.sum(jnp.where(expert == lane, col, 0.0), axis=0, keepdims=True)

    nblk = jnp.floor((cnt + (MOE_TM - 1)) / MOE_TM)
    nblk_row = as_row(nblk)
    end = jnp.sum(jnp.where(lane <= expert, nblk_row, 0.0), axis=1, keepdims=True)
    start = end - nblk
    end_row = jnp.sum(jnp.where(expert <= lane, nblk, 0.0), axis=0, keepdims=True)
    total = end_row[:, e - 1:e]

    blk_e = jnp.minimum(jnp.sum(jnp.where(lane_row >= end, 1.0, 0.0), axis=0, keepdims=True), e - 1.0)
    mine = expert == blk_e

    def per_block(col):
        return jnp.sum(jnp.where(mine, col, 0.0), axis=0, keepdims=True)

    blk_start = per_block(start)
    tokens = jnp.clip(per_block(cnt) - (lane_row - blk_start) * MOE_TM, 0.0, float(MOE_TM))
    blk_on = jnp.where(lane_row >= total, float(BLK_UNUSED),
                       jnp.where(tokens <= MOE_TM // 2, float(BLK_HALF), float(BLK_FULL)))
    blk_new = jnp.where(lane_row == blk_start, 1.0, 0.0)

    used_row = jnp.where(as_row(cnt) > 0.0, 1.0, 0.0)
    runs_before = jnp.sum(jnp.where(lane < expert, used_row, 0.0), axis=1, keepdims=True)
    slot = runs_before - 2.0 * jnp.floor(runs_before / 2.0)
    after = jnp.min(jnp.where((lane > expert) & (used_row > 0.0), lane, float(lanes)), axis=1, keepdims=True)
    after = jnp.where(after >= e, -1.0, after)

    cnt_row = as_row(cnt)
    partly = jnp.where((cnt_row - MOE_TM * jnp.floor(cnt_row / MOE_TM) != 0.0) & (lane_row < e), 1.0, 0.0)
    tail_blk = total + (lane_row - e)
    in_tail = (lane_row >= e) & (lane_row < e + tail) & (tail_blk < n_blk)
    zero_blk = jnp.where(in_tail, tail_blk, (end_row - 1.0) * partly)
    zero_on = jnp.where(in_tail, 1.0, partly)
    seg_start = as_row(start) * MOE_TM

    rows = dict(blk_e=blk_e, blk_new=blk_new, blk_on=blk_on, blk_slot=per_block(slot), blk_next=per_block(after),
                zero_blk=zero_blk, zero_on=zero_on, seg_start=seg_start)
    out_row = lax.broadcasted_iota(jnp.int32, (len(PLAN_ROWS), lanes), 0)
    plan = jnp.zeros((len(PLAN_ROWS), lanes), F32)
    for r, name in enumerate(PLAN_ROWS):
        plan = jnp.where(out_row == r, rows[name], plan)
    plan_ref[...] = plan.astype(jnp.int32)


def _moe(hn_p, hn_s, ei_p, ei_s, counts, layer, w_gu, b_gu, w_dn, b_dn):
    n_asg = (ei_p.shape[1] + ei_s.shape[1]) * TOP_K
    n_pad = (n_asg + N_EXPERTS * (MOE_TM - 1) + MOE_TM - 1) // MOE_TM * MOE_TM
    n_blk = n_pad // MOE_TM
    tail = n_blk - n_asg // MOE_TM
    assert n_blk <= PLAN_LANES and N_EXPERTS + tail <= PLAN_LANES
    plan = pl.pallas_call(
        functools.partial(_plan_kernel, n_blk=n_blk, tail=tail),
        out_shape=jax.ShapeDtypeStruct((len(PLAN_ROWS), PLAN_LANES), jnp.int32),
        name="moe_plan",
    )(counts)
    row = {name: plan[r] for r, name in enumerate(PLAN_ROWS)}
    blk = tuple(row[name][:n_blk] for name in ("blk_e", "blk_new", "blk_on", "blk_slot", "blk_next"))
    n_zero = N_EXPERTS + tail

    dest = _slots(jnp.concatenate([ei_p, ei_s], axis=1), row["seg_start"][:N_EXPERTS])
    dest_p, dest_s = dest[:, :ei_p.shape[1]], dest[:, ei_p.shape[1]:]
    x_buf = _dispatch(dest, row["zero_blk"][:n_zero], row["zero_on"][:n_zero], hn_p, hn_s, n_pad)
    y_buf = _moe_blocks(x_buf, blk, layer, w_gu, b_gu, w_dn, b_dn)
    return dest_p, dest_s, y_buf


def _final_kernel(dest_cur, dest_nxt, x_ref, pr_ref, ybuf, mod_ref, g_ref, o_ref, rows, moe_s, sems):
    x = x_ref[...] + mod_ref[5:6, :] * _combine_experts(dest_cur, dest_nxt, pr_ref, ybuf, rows, moe_s, sems)
    o_ref[...] = _rmsnorm(x, g_ref[...])


def _final_norm(x, route, mod, g, seq):
    n = x.shape[0]
    tm = min(ROW_TILE, seq)
    row = pl.BlockSpec((tm, D_MODEL), lambda i: (i, 0))
    args, specs, scratch = _combine_operands(*route, n, tm)
    return pl.pallas_call(
        _final_kernel,
        grid=(n // tm,),
        in_specs=specs[:2] + [row] + specs[2:] + [_mod_spec(mod, seq, tm),
                                                  pl.BlockSpec((1, D_MODEL), lambda i: (0, 0))],
        out_specs=row,
        out_shape=jax.ShapeDtypeStruct((n, D_MODEL), F32),
        scratch_shapes=scratch,
        compiler_params=pltpu.CompilerParams(dimension_semantics=("arbitrary",), vmem_limit_bytes=VMEM_LIMIT),
        name="final_norm",
    )(*args[:2], x, *args[2:], mod, g.reshape(1, D_MODEL))


def _block_diag(w):
    eye = jnp.eye(HEADS, dtype=w.dtype)
    return (eye[:, None, :, None] * w[:, :, None, :]).reshape(W_GRP, W_GRP)


def _layer_weights(l, p):
    gate_w = jnp.concatenate([_block_diag(p["lru_wr"][l, 0]), _block_diag(p["lru_wi"][l, 0]),
                              _block_diag(p["lru_wr"][l, 1]), _block_diag(p["lru_wi"][l, 1])], axis=1)
    gate_b = jnp.concatenate([p["lru_br"][l, 0], p["lru_bi"][l, 0], p["lru_br"][l, 1], p["lru_bi"][l, 1]])
    head_of = jnp.arange(W_GRP) // HD
    row = lambda v: v.reshape(1, W_GRP)
    return dict(
        conv_a_w=p["conv_a_w"][l], conv_a_b=row(p["conv_a_b"][l]),
        norm_a_g=row(p["norm_a_g"][l]), norm_a_b=row(p["norm_a_b"][l]),
        conv_b_w=p["conv_b_w"][l], conv_b_b=row(p["conv_b_b"][l]),
        gate_w=gate_w.astype(BF16), gate_b=gate_b.reshape(1, 4 * W_GRP), lru_lam=p["lru_lam"][l],
        sgu_norm_g=row(p["sgu_norm_g"][l]), sgu_norm_b=row(p["sgu_norm_b"][l]),
        sgu_w=p["sgu_w"][l].astype(BF16), sgu_bias=jnp.repeat(p["sgu_b"][l].T, HD, axis=1),
        conv_d_w=p["conv_d_w"][l],
        m_head=((head_of[:, None] == head_of[None, :]).astype(F32) / HD).astype(BF16),
    )


def kernel(x_prompt, x_sample, state_rglru, c, c_ctx, w_ada, b_ada, norm1_g, norm2_g, w_in, conv_a_w,
           conv_a_b, norm_a_g, norm_a_b, conv_b_w, conv_b_b, lru_wr, lru_br, lru_wi, lru_bi, lru_lam,
           sgu_norm_g, sgu_norm_b, sgu_w, sgu_b, conv_d_w, w_out, w_router, b_router, w_gu, b_gu, w_dn,
           b_dn, final_g):
    p = dict(conv_a_w=conv_a_w, conv_a_b=conv_a_b, norm_a_g=norm_a_g, norm_a_b=norm_a_b,
             conv_b_w=conv_b_w, conv_b_b=conv_b_b, lru_wr=lru_wr, lru_br=lru_br, lru_wi=lru_wi,
             lru_bi=lru_bi, lru_lam=lru_lam, sgu_norm_g=sgu_norm_g, sgu_norm_b=sgu_norm_b,
             sgu_w=sgu_w, sgu_b=sgu_b, conv_d_w=conv_d_w)
    bp, tp, _ = x_prompt.shape
    bs, ts, _ = x_sample.shape
    n_p, n_s = bp * tp, bs * ts

    cond_rows = jnp.zeros((COND_ROWS, D_MODEL), F32).at[0].set(c_ctx).at[1:1 + bs].set(c)
    mod = _ada_mod(cond_rows, w_ada, b_ada).reshape(DEPTH, COND_ROWS, 6, D_MODEL)

    xp = x_prompt.reshape(n_p, D_MODEL)
    xs = x_sample.reshape(n_s, D_MODEL)
    h0_ctx = jnp.zeros((bp, 2, W_GRP), F32)
    route_p = route_s = mod_p_prev = mod_s_prev = None
    no_tokens_seen = jnp.zeros((N_EXPERTS, 1), F32)
    states = []
    for l in range(DEPTH):
        lw = _layer_weights(l, p)
        mod_p, mod_s = mod[l, 0:1], mod[l, 1:1 + bs]
        w_in_l = w_in[l].astype(BF16)
        w_out_l = w_out[l].astype(BF16)
        xp, proj_p = _in_proj(xp, route_p, mod_p_prev, mod_p, norm1_g[l], w_in_l, tp)
        xs, proj_s = _in_proj(xs, route_s, mod_s_prev, mod_s, norm1_g[l], w_in_l, ts)
        y_p, st = _mixers(proj_p, h0_ctx, lw, bp, tp, False)
        y_s, _ = _mixers(proj_s, state_rglru[:, l], lw, bs, ts, True)
        states.append(st)
        xp, hn_p, ei_p, pr_p, seen = _out_proj(y_p, xp, mod_p, norm2_g[l], w_out_l, w_router[l],
                                               b_router[l], no_tokens_seen, tp)
        xs, hn_s, ei_s, pr_s, counts = _out_proj(y_s, xs, mod_s, norm2_g[l], w_out_l, w_router[l],
                                                 b_router[l], seen, ts)
        dest_p, dest_s, y_buf = _moe(hn_p, hn_s, ei_p, ei_s, counts, l, w_gu, b_gu[l], w_dn, b_dn[l])
        route_p, route_s = (dest_p, pr_p, y_buf), (dest_s, pr_s, y_buf)
        mod_p_prev, mod_s_prev = mod_p, mod_s
    y_prompt = _final_norm(xp, route_p, mod_p_prev, final_g, tp).reshape(bp, tp, D_MODEL)
    y_sample = _final_norm(xs, route_s, mod_s_prev, final_g, ts).reshape(bs, ts, D_MODEL)
    return y_prompt, y_sample, jnp.stack(states, axis=1)
```
